```python
import math
import jax, jax.numpy as jnp
from jax import lax
import numpy as np

D_MODEL = 2048
BATCH = 16
SEQ = 2048
DEPTH = 4

GRID_W = 64
CTX_LEN = 256
MIX_W = D_MODEL
DIFF_W = MIX_W // 4
GQA_W = MIX_W // 2
HY_CH = MIX_W - DIFF_W - GQA_W
DIFF_V = 128
DIFF_QK = DIFF_V // 2
DIFF_HEADS = DIFF_W // DIFF_V
GQA_HD = 128
GQA_HEADS = GQA_W // GQA_HD
GQA_KV_HEADS = 2
GQA_GROUP = GQA_HEADS // GQA_KV_HEADS
HY_EMB = 33
HY_BANDS = (HY_EMB - 1) // 2
HY_FFN = 64
HY_FAST_DECAY = 0.3
HY_SLOW_DECAY = 1.5
HY_TARGET = 1e-2
HY_SHIFT = 0.0
SHORT_K = 3
D_FF = 4 * D_MODEL
N_MOD = 6
ROPE_THETA = 10000.0
Q_BLOCK = 128
EPS = 1e-6

DQ_W = DIFF_HEADS * 2 * DIFF_QK
GQ_W = GQA_HEADS * GQA_HD
HY_W = 3 * HY_CH
DK_W = DQ_W
DV_W = DIFF_HEADS * DIFF_V
GK_W = GQA_KV_HEADS * GQA_HD
GV_W = GK_W
KV_OFF = DQ_W + GQ_W + HY_W
KV_W = DK_W + DV_W + GK_W + GV_W
N_IN = KV_OFF + KV_W

kernel_name = "hymba_style_diffattn_gqa_hyena_dit"


def rmsnorm(x, g):
    xf = x.astype(jnp.float32)
    y = xf * lax.rsqrt(jnp.mean(xf * xf, axis=-1, keepdims=True) + EPS)
    return (y * g.astype(jnp.float32)).astype(x.dtype)


def modulate(x, shift, scale):
    return x * (1 + scale) + shift


def axial_rope(n_rows, head_dim):
    t_row = jnp.repeat(jnp.arange(n_rows, dtype=jnp.float32), GRID_W)
    t_col = jnp.tile(jnp.arange(GRID_W, dtype=jnp.float32), n_rows)
    d_axis = head_dim // 2
    inv = ROPE_THETA ** (-jnp.arange(0, d_axis, 2, dtype=jnp.float32) / d_axis)
    ang = jnp.concatenate([t_row[:, None] * inv, t_col[:, None] * inv], axis=-1)
    return jnp.cos(ang), jnp.sin(ang)


def apply_rope(x, cos, sin):
    xf = x.astype(jnp.float32).reshape(*x.shape[:-1], -1, 2)
    x0, x1 = xf[..., 0], xf[..., 1]
    out = jnp.stack([x0 * cos - x1 * sin, x0 * sin + x1 * cos], axis=-1)
    return out.reshape(x.shape).astype(x.dtype)


def heads(t, *dims):
    b, n = t.shape[:2]
    return jnp.moveaxis(t.reshape(b, n, *dims), 1, -2)


def merge(t):
    t = jnp.moveaxis(t, -2, 1)
    return t.reshape(t.shape[0], t.shape[1], -1)


def split_front(h):
    return jnp.split(h, [DQ_W, DQ_W + GQ_W], axis=-1)


def split_kv(h):
    return jnp.split(h, [DK_W, DK_W + DV_W, DK_W + DV_W + GK_W], axis=-1)


def sweep_query_blocks(fn, q):
    n = q.shape[-2]
    nb = n // Q_BLOCK
    qb = jnp.moveaxis(q.reshape(*q.shape[:-2], nb, Q_BLOCK, q.shape[-1]), -3, 0)
    out = jnp.moveaxis(lax.map(fn, qb), 0, -3)
    return out.reshape(*out.shape[:-3], n, out.shape[-1])


def diff_lambda(lam_params, lam_init):
    p = lam_params.astype(jnp.float32)
    return jnp.exp(jnp.sum(p[0] * p[1])) - jnp.exp(jnp.sum(p[2] * p[3])) + lam_init


def diff_attend(q, k, v, lam):
    s = jnp.einsum('bhmqd,bhmkd->bhmqk', q, k).astype(jnp.float32) * (DIFF_QK ** -0.5)
    p = jax.nn.softmax(s, axis=-1)
    a = p[:, :, 0] - lam * p[:, :, 1]
    return jnp.einsum('bhqk,bhkd->bhqd', a.astype(v.dtype), v)


def gqa_attend(q, k, v):
    s = jnp.einsum('bgrqd,bgkd->bgrqk', q, k).astype(jnp.float32) * (GQA_HD ** -0.5)
    p = jax.nn.softmax(s, axis=-1).astype(v.dtype)
    return jnp.einsum('bgrqk,bgkd->bgrqd', p, v)


def hyena_filters(n, w1, b1, w2, b2, w3, b3, wout, freq):
    f32 = jnp.float32
    t = jnp.linspace(0.0, 1.0, n, dtype=f32)[:, None]
    w = 2.0 * math.pi * jnp.arange(n, dtype=f32)[:, None] / n
    f = jnp.linspace(1e-4, HY_BANDS - 1, HY_BANDS, dtype=f32)[None, :]
    z = jnp.concatenate([t, jnp.cos(f * w), -jnp.sin(f * w)], axis=-1)
    fr = freq.astype(f32)
    h = jnp.sin(fr[0] * (z @ w1.astype(f32) + b1.astype(f32)))
    h = jnp.sin(fr[1] * (h @ w2.astype(f32) + b2.astype(f32)))
    h = jnp.sin(fr[2] * (h @ w3.astype(f32) + b3.astype(f32)))
    h = h @ wout.astype(f32)
    min_decay = math.log(HY_TARGET) / HY_SLOW_DECAY
    max_decay = math.log(HY_TARGET) / HY_FAST_DECAY
    deltas = jnp.linspace(min_decay, max_decay, HY_CH, dtype=f32)
    decay = jnp.exp(-t * jnp.abs(deltas))
    return h * (jnp.concatenate([decay, decay], axis=-1) + HY_SHIFT)


def bidir_long_conv(u, h2, bias):
    n, ch = u.shape[1], u.shape[2]
    hf, hb = h2[:, :ch], h2[:, ch:]
    h_full = jnp.concatenate([hf, jnp.zeros((1, ch), hf.dtype), hb[:0:-1]], axis=0)
    uf = u.astype(jnp.float32)
    y = jnp.fft.irfft(jnp.fft.rfft(uf, n=2 * n, axis=1) * jnp.fft.rfft(h_full, n=2 * n, axis=0)[None],
                      n=2 * n, axis=1)[:, :n]
    return (y + uf * bias.astype(jnp.float32)).astype(u.dtype)


def short_conv3(u, w, b):
    up = jnp.pad(u, ((0, 0), (1, 1), (0, 0)))
    return up[:, :-2] * w[0] + up[:, 1:-1] * w[1] + up[:, 2:] * w[2] + b


def hyena_mixer(hy, conv_w, conv_b, filt, bias):
    u = short_conv3(hy, conv_w, conv_b)
    x0, x1, v = jnp.split(u, 3, axis=-1)
    return x0 * bidir_long_conv(v * x1, filt, bias)


def sq_relu_mlp(x, w_up, w_down):
    return jnp.square(jax.nn.relu(x @ w_up)) @ w_down


def setup_inputs(seed: int = 0) -> dict:
    key = jax.random.key(seed)
    ks = jax.random.split(key, 32)

    def nrm(i, shape, scale):
        return jax.random.normal(ks[i], shape, jnp.float32) * scale

    nl = DEPTH
    return {
        "x": nrm(0, (BATCH, SEQ, D_MODEL), 1.0),
        "c": nrm(1, (BATCH, D_MODEL), 1.0),
        "ctx": nrm(2, (BATCH, CTX_LEN, D_MODEL), 1.0),
        "c_ctx": nrm(3, (D_MODEL,), 1.0),
        "w_mod": nrm(4, (nl, D_MODEL, N_MOD * D_MODEL), 0.5 * D_MODEL ** -0.5),
        "b_mod": nrm(5, (nl, N_MOD * D_MODEL), 0.02),
        "g_norm": 1.0 + nrm(6, (nl, 4, D_MODEL), 0.1),
        "w_in": nrm(7, (nl, D_MODEL, N_IN), D_MODEL ** -0.5),
        "w_out": nrm(8, (nl, MIX_W, D_MODEL), MIX_W ** -0.5),
        "diff_lam": nrm(9, (nl, 4, DIFF_QK), 0.1),
        "diff_subln": 1.0 + nrm(10, (nl, DIFF_V), 0.1),
        "gqa_q_norm": 1.0 + nrm(11, (nl, GQA_HD), 0.1),
        "gqa_k_norm": 1.0 + nrm(12, (nl, GQA_HD), 0.1),
        "gqa_out_norm": 1.0 + nrm(13, (nl, GQA_HD), 0.1),
        "hy_conv_w": nrm(14, (nl, SHORT_K, HY_W), SHORT_K ** -0.5),
        "hy_conv_b": nrm(15, (nl, HY_W), 0.02),
        "hy_w1": nrm(16, (nl, HY_EMB, HY_FFN), HY_EMB ** -0.5),
        "hy_b1": nrm(17, (nl, HY_FFN), 0.1),
        "hy_w2": nrm(18, (nl, HY_FFN, HY_FFN), HY_FFN ** -0.5),
        "hy_b2": nrm(19, (nl, HY_FFN), 0.1),
        "hy_w3": nrm(20, (nl, HY_FFN, HY_FFN), HY_FFN ** -0.5),
        "hy_b3": nrm(21, (nl, HY_FFN), 0.1),
        "hy_wout": nrm(22, (nl, HY_FFN, 2 * HY_CH), 0.1 * HY_FFN ** -0.5),
        "hy_freq": 1.0 + nrm(23, (nl, 3, HY_FFN), 0.1),
        "hy_bias": nrm(24, (nl, HY_CH), 0.5),
        "hy_out_norm": 1.0 + nrm(25, (nl, HY_CH), 0.1),
        "w_up": nrm(26, (nl, D_MODEL, D_FF), D_MODEL ** -0.5),
        "w_down": nrm(27, (nl, D_FF, D_MODEL), D_FF ** -0.5),
    }


def reference(x, c, ctx, c_ctx, w_mod, b_mod, g_norm, w_in, w_out, diff_lam, diff_subln,
              gqa_q_norm, gqa_k_norm, gqa_out_norm, hy_conv_w, hy_conv_b, hy_w1, hy_b1,
              hy_w2, hy_b2, hy_w3, hy_b3, hy_wout, hy_freq, hy_bias, hy_out_norm, w_up, w_down):
    n_lat = x.shape[1]
    n_ctx = ctx.shape[1]
    ROWS = n_lat // GRID_W
    cos_d, sin_d = axial_rope(ROWS, DIFF_QK)
    cos_g, sin_g = axial_rope(ROWS, GQA_HD)
    s_c = jax.nn.silu(c)
    s_cc = jax.nn.silu(c_ctx)
    xc = ctx
    for l in range(DEPTH):
        last = l == DEPTH - 1
        lam_init = 0.8 - 0.6 * math.exp(-0.3 * l)
        lam = diff_lambda(diff_lam[l], lam_init)
        sh_a, sc_a, gt_a, sh_m, sc_m, gt_m = jnp.split((s_c @ w_mod[l] + b_mod[l])[:, None, :], N_MOD, axis=-1)
        csh_a, csc_a, cgt_a, csh_m, csc_m, cgt_m = jnp.split(s_cc @ w_mod[l] + b_mod[l], N_MOD, axis=-1)
        filt_args = (hy_w1[l], hy_b1[l], hy_w2[l], hy_b2[l], hy_w3[l], hy_b3[l], hy_wout[l], hy_freq[l])

        xn = modulate(rmsnorm(x, g_norm[l, 0]), sh_a, sc_a)
        xcn = modulate(rmsnorm(xc, g_norm[l, 0]), csh_a, csc_a)
        h = xn @ w_in[l]
        hc = xcn @ (w_in[l][:, KV_OFF:] if last else w_in[l])
        dq, gq, hy = split_front(h[..., :KV_OFF])
        dk, dv, gk, gv = split_kv(h[..., KV_OFF:])
        cdk, cdv, cgk, cgv = split_kv(hc[..., -KV_W:])

        q_d = apply_rope(heads(dq, DIFF_HEADS, 2, DIFF_QK), cos_d, sin_d)
        k_d = apply_rope(heads(dk, DIFF_HEADS, 2, DIFF_QK), cos_d, sin_d)
        kc_d = heads(cdk, DIFF_HEADS, 2, DIFF_QK)
        vc_d = heads(cdv, DIFF_HEADS, DIFF_V)
        kd_all = jnp.concatenate([kc_d, k_d], axis=-2)
        vd_all = jnp.concatenate([vc_d, heads(dv, DIFF_HEADS, DIFF_V)], axis=-2)
        o_d = sweep_query_blocks(lambda qb: diff_attend(qb, kd_all, vd_all, lam), q_d)
        o_d = merge(rmsnorm(o_d, diff_subln[l]) * (1.0 - lam_init))

        q_g = apply_rope(rmsnorm(heads(gq, GQA_KV_HEADS, GQA_GROUP, GQA_HD), gqa_q_norm[l]), cos_g, sin_g)
        k_g = apply_rope(rmsnorm(heads(gk, GQA_KV_HEADS, GQA_HD), gqa_k_norm[l]), cos_g, sin_g)
        kc_g = rmsnorm(heads(cgk, GQA_KV_HEADS, GQA_HD), gqa_k_norm[l])
        vc_g = heads(cgv, GQA_KV_HEADS, GQA_HD)
        kg_all = jnp.concatenate([kc_g, k_g], axis=-2)
        vg_all = jnp.concatenate([vc_g, heads(gv, GQA_KV_HEADS, GQA_HD)], axis=-2)
        o_g = sweep_query_blocks(lambda qb: gqa_attend(qb, kg_all, vg_all), q_g)
        o_g = merge(rmsnorm(o_g, gqa_out_norm[l]))

        filt_lat = hyena_filters(n_lat, *filt_args)
        o_h = rmsnorm(hyena_mixer(hy, hy_conv_w[l], hy_conv_b[l], filt_lat, hy_bias[l]), hy_out_norm[l])

        mix = jnp.concatenate([o_d, o_g, o_h], axis=-1) @ w_out[l]
        x = x + gt_a * rmsnorm(mix, g_norm[l, 1])

        if not last:
            cdq, cgq, chy = split_front(hc[..., :KV_OFF])
            oc_d = diff_attend(heads(cdq, DIFF_HEADS, 2, DIFF_QK), kc_d, vc_d, lam)
            oc_d = merge(rmsnorm(oc_d, diff_subln[l]) * (1.0 - lam_init))
            oc_g = gqa_attend(rmsnorm(heads(cgq, GQA_KV_HEADS, GQA_GROUP, GQA_HD), gqa_q_norm[l]), kc_g, vc_g)
            oc_g = merge(rmsnorm(oc_g, gqa_out_norm[l]))
            filt_ctx = hyena_filters(n_ctx, *filt_args)
            oc_h = rmsnorm(hyena_mixer(chy, hy_conv_w[l], hy_conv_b[l], filt_ctx, hy_bias[l]), hy_out_norm[l])
            mix_c = jnp.concatenate([oc_d, oc_g, oc_h], axis=-1) @ w_out[l]
            xc = xc + cgt_a * rmsnorm(mix_c, g_norm[l, 1])

        xn = modulate(rmsnorm(x, g_norm[l, 2]), sh_m, sc_m)
        x = x + gt_m * rmsnorm(sq_relu_mlp(xn, w_up[l], w_down[l]), g_norm[l, 3])
        if not last:
            xcn = modulate(rmsnorm(xc, g_norm[l, 2]), csh_m, csc_m)
            xc = xc + cgt_m * rmsnorm(sq_relu_mlp(xcn, w_up[l], w_down[l]), g_norm[l, 3])
    return x
```

```python
import functools
import math

import jax
import jax.numpy as jnp
from jax import lax
from jax.experimental import pallas as pl
from jax.experimental.pallas import tpu as pltpu

F32 = jnp.float32
BF16 = jnp.bfloat16

D_MODEL = 2048
GRID_W = 64
DIFF_W = 512
GQA_W = 1024
HY_CH = 512
DIFF_V = 128
DIFF_QK = 64
DIFF_HEADS = 4
GQA_HD = 128
GQA_KV_HEADS = 2
GQA_GROUP = 4
HY_EMB = 33
HY_BANDS = 16
HY_FFN = 64
HY_FAST_DECAY = 0.3
HY_SLOW_DECAY = 1.5
HY_TARGET = 1e-2
HY_SHIFT = 0.0
D_FF = 4 * D_MODEL
N_MOD = 6
ROPE_THETA = 10000.0
EPS = 1e-6

LANE = 128
DQ_BLK = 0
GQ_BLK = 4
HY_OFF = 1536
KV_OFF = 3072
KV_BLK = KV_OFF // LANE
DK_REL, DV_REL, GK_REL, GV_REL = 0, 4, 8, 10
N_IN = 4608
KV_W = N_IN - KV_OFF

VMEM_CAP = 64 * 1024 * 1024
MOD_ROWS = 24


def _vmem_limit(nbytes):
    return int(min(max(nbytes * 5 // 4 + (4 << 20), 32 << 20), VMEM_CAP - (6 << 20)))


def _rms(x):
    return x * lax.rsqrt(jnp.mean(x * x, axis=-1, keepdims=True) + EPS)


def _dot(a, b):
    return jnp.dot(a, b, preferred_element_type=F32)


def _dot_nt(a, b):
    return lax.dot_general(a, b, (((1,), (1,)), ((), ())), preferred_element_type=F32)


def _split_bf16(a):
    hi = a.astype(BF16)
    lo = (a - hi.astype(F32)).astype(BF16)
    return hi, lo


def _dot3(a, b):
    ah, al = _split_bf16(a)
    bh, bl = _split_bf16(b)
    return _dot(ah, bh) + (_dot(al, bh) + _dot(ah, bl))


def _rope(x, cos, sin_signed):
    lane = lax.broadcasted_iota(jnp.int32, x.shape, 1)
    nxt = pltpu.roll(x, LANE - 1, axis=1)
    prv = pltpu.roll(x, 1, axis=1)
    swapped = jnp.where(lane % 2 == 0, nxt, prv)
    return x * cos + swapped * sin_signed


def _mod_kernel(c_ref, w_ref, b_ref, o_ref):
    c = c_ref[...]
    s = c * (1.0 / (1.0 + jnp.exp(-c)))
    o_ref[0] = _dot(s.astype(BF16), w_ref[0].astype(BF16)) + b_ref[0]


def _modulations(cpad, w_mod, b_mod):
    nl, d, n = w_mod.shape
    tn = 1024
    return pl.pallas_call(
        _mod_kernel,
        grid=(nl, n // tn),
        in_specs=[
            pl.BlockSpec((MOD_ROWS, d), lambda l, j: (0, 0)),
            pl.BlockSpec((1, d, tn), lambda l, j: (l, 0, j)),
            pl.BlockSpec((1, 1, tn), lambda l, j: (l, 0, j)),
        ],
        out_specs=pl.BlockSpec((1, MOD_ROWS, tn), lambda l, j: (l, 0, j)),
        out_shape=jax.ShapeDtypeStruct((nl, MOD_ROWS, n), F32),
        compiler_params=pltpu.CompilerParams(
            dimension_semantics=("arbitrary", "arbitrary"),
            vmem_limit_bytes=_vmem_limit(2 * d * tn * 4 + d * tn * 2)),
        name="modulations",
    )(cpad, w_mod, b_mod.reshape(nl, 1, n))


def _in_proj_kernel(x_ref, g_ref, m_ref, w_ref, o_ref, xn_ref):
    @pl.when(pl.program_id(2) == 0)
    def _():
        y = _rms(x_ref[...]) * g_ref[...]
        xn = y * (1.0 + m_ref[1:2, :]) + m_ref[0:1, :]
        xn_ref[...] = xn.astype(BF16)

    o_ref[...] = _dot(xn_ref[...], w_ref[...]).astype(o_ref.dtype)


def _in_proj(x, g, mods, w, per_batch):
    b, s, d = x.shape
    n = w.shape[1]
    tm = min(s, 1024)
    tn = 512
    est = 2 * tm * d * 4 + tm * d * 2 + 2 * d * tn * 2 + 2 * tm * tn * 2 + tm * tn * 4
    return pl.pallas_call(
        _in_proj_kernel,
        grid=(b, s // tm, n // tn),
        in_specs=[
            pl.BlockSpec((None, tm, d), lambda bi, i, j: (bi, i, 0)),
            pl.BlockSpec((1, d), lambda bi, i, j: (0, 0)),
            pl.BlockSpec((None, N_MOD, d), lambda bi, i, j: (bi * per_batch, 0, 0)),
            pl.BlockSpec((d, tn), lambda bi, i, j: (0, j)),
        ],
        out_specs=pl.BlockSpec((None, tm, tn), lambda bi, i, j: (bi, i, j)),
        out_shape=jax.ShapeDtypeStruct((b, s, n), BF16),
        scratch_shapes=[pltpu.VMEM((tm, d), BF16)],
        compiler_params=pltpu.CompilerParams(
            dimension_semantics=("parallel", "parallel", "arbitrary"),
            vmem_limit_bytes=_vmem_limit(est)),
        name="in_proj",
    )(x, g.reshape(1, d), mods, w)


def _gqa_kernel(*refs, rope, prefix):
    it = iter(refs)
    q_ref, k_ref, v_ref = next(it), next(it), next(it)
    kc_ref = vc_ref = None
    if prefix:
        kc_ref, vc_ref = next(it), next(it)
    cq_ref = sq_ref = ck_ref = sk_ref = None
    if rope:
        cq_ref, sq_ref, ck_ref, sk_ref = next(it), next(it), next(it), next(it)
    gq_ref, gk_ref, go_ref = next(it), next(it), next(it)
    o_ref = next(it)
    kr_ref = next(it)
    kcr_ref = next(it) if prefix else None

    @pl.when((pl.program_id(2) == 0) & (pl.program_id(3) == 0))
    def _():
        k = _rms(k_ref[...].astype(F32)) * gk_ref[...]
        if rope:
            k = _rope(k, ck_ref[...], sk_ref[...])
        kr_ref[...] = k.astype(BF16)
        if prefix:
            kc = _rms(kc_ref[...].astype(F32)) * gk_ref[...]
            kcr_ref[...] = kc.astype(BF16)

    q = _rms(q_ref[...].astype(F32)) * gq_ref[...]
    if rope:
        q = _rope(q, cq_ref[...], sq_ref[...])
    q = (q * (GQA_HD ** -0.5)).astype(BF16)

    s_l = _dot_nt(q, kr_ref[...])
    m = jnp.max(s_l, axis=-1, keepdims=True)
    if prefix:
        s_c = _dot_nt(q, kcr_ref[...])
        m = jnp.maximum(m, jnp.max(s_c, axis=-1, keepdims=True))
    p_l = jnp.exp(s_l - m)
    denom = jnp.sum(p_l, axis=-1, keepdims=True)
    o = _dot(p_l.astype(BF16), v_ref[...])
    if prefix:
        p_c = jnp.exp(s_c - m)
        denom = denom + jnp.sum(p_c, axis=-1, keepdims=True)
        o = o + _dot(p_c.astype(BF16), vc_ref[...])
    o = o / denom
    o_ref[...] = (_rms(o) * go_ref[...]).astype(o_ref.dtype)


def _gqa_attention(hq, hk, kv_blk, hc, c_blk, tables, gq, gk, go, tq):
    b, s, _ = hq.shape
    prefix = hc is not None
    rope = tables is not None
    grid = (b, GQA_KV_HEADS, GQA_GROUP, s // tq)
    in_specs = [
        pl.BlockSpec((None, tq, LANE), lambda bi, g, r, i: (bi, i, GQ_BLK + g * GQA_GROUP + r)),
        pl.BlockSpec((None, s, LANE), lambda bi, g, r, i: (bi, 0, kv_blk + GK_REL + g)),
        pl.BlockSpec((None, s, LANE), lambda bi, g, r, i: (bi, 0, kv_blk + GV_REL + g)),
    ]
    args = [hq, hk, hk]
    scratch = [pltpu.VMEM((s, LANE), BF16)]
    c = 0
    if prefix:
        c = hc.shape[1]
        in_specs += [
            pl.BlockSpec((None, c, LANE), lambda bi, g, r, i: (bi, 0, c_blk + GK_REL + g)),
            pl.BlockSpec((None, c, LANE), lambda bi, g, r, i: (bi, 0, c_blk + GV_REL + g)),
        ]
        args += [hc, hc]
        scratch.append(pltpu.VMEM((c, LANE), BF16))
    if rope:
        cos, sin = tables
        in_specs += [
            pl.BlockSpec((tq, LANE), lambda bi, g, r, i: (i, 0)),
            pl.BlockSpec((tq, LANE), lambda bi, g, r, i: (i, 0)),
            pl.BlockSpec((s, LANE), lambda bi, g, r, i: (0, 0)),
            pl.BlockSpec((s, LANE), lambda bi, g, r, i: (0, 0)),
        ]
        args += [cos, sin, cos, sin]
    vec = pl.BlockSpec((1, LANE), lambda bi, g, r, i: (0, 0))
    in_specs += [vec, vec, vec]
    args += [gq.reshape(1, LANE), gk.reshape(1, LANE), go.reshape(1, LANE)]
    est = 6 * s * LANE * 2 + 4 * s * LANE * 4 + 4 * tq * (s + c) * 4
    return pl.pallas_call(
        functools.partial(_gqa_kernel, rope=rope, prefix=prefix),
        grid=grid,
        in_specs=in_specs,
        out_specs=pl.BlockSpec((None, tq, LANE), lambda bi, g, r, i: (bi, i, g * GQA_GROUP + r)),
        out_shape=jax.ShapeDtypeStruct((b, s, GQA_W), BF16),
        scratch_shapes=scratch,
        compiler_params=pltpu.CompilerParams(
            dimension_semantics=("arbitrary",) * 4,
            vmem_limit_bytes=_vmem_limit(est)),
        name="gqa_attention",
    )(*args)


def _diff_kernel(*refs, rope, prefix, lam_init):
    it = iter(refs)
    lp_ref, q_ref, k_ref, v_ref = next(it), next(it), next(it), next(it)
    kc_ref = vc_ref = None
    if prefix:
        kc_ref, vc_ref = next(it), next(it)
    cq_ref = sq_ref = ck_ref = sk_ref = None
    if rope:
        cq_ref, sq_ref, ck_ref, sk_ref = next(it), next(it), next(it), next(it)
    gs_ref = next(it)
    o_ref = next(it)
    kr_ref, lam_ref = next(it), next(it)

    first = (pl.program_id(0) == 0) & (pl.program_id(1) == 0) & (pl.program_id(2) == 0)

    @pl.when(first)
    def _():
        p = lp_ref[...]
        t1 = jnp.sum(p[0:1] * p[1:2], axis=-1, keepdims=True)
        t2 = jnp.sum(p[2:3] * p[3:4], axis=-1, keepdims=True)
        lam_ref[...] = jnp.broadcast_to(jnp.exp(t1) - jnp.exp(t2) + lam_init, lam_ref.shape)

    @pl.when(pl.program_id(2) == 0)
    def _():
        if rope:
            kr_ref[...] = _rope(k_ref[...].astype(F32), ck_ref[...], sk_ref[...]).astype(BF16)
        else:
            kr_ref[...] = k_ref[...]

    lam = lam_ref[0:1, 0:1]
    q = q_ref[...].astype(F32)
    if rope:
        q = _rope(q, cq_ref[...], sq_ref[...])
    q = q * (DIFF_QK ** -0.5)
    lane = lax.broadcasted_iota(jnp.int32, q.shape, 1)
    q1 = jnp.where(lane < DIFF_QK, q, 0.0).astype(BF16)
    q2 = jnp.where(lane >= DIFF_QK, q, 0.0).astype(BF16)

    kr = kr_ref[...]
    s1 = _dot_nt(q1, kr)
    s2 = _dot_nt(q2, kr)
    m1 = jnp.max(s1, axis=-1, keepdims=True)
    m2 = jnp.max(s2, axis=-1, keepdims=True)
    if prefix:
        kc = kc_ref[...]
        c1 = _dot_nt(q1, kc)
        c2 = _dot_nt(q2, kc)
        m1 = jnp.maximum(m1, jnp.max(c1, axis=-1, keepdims=True))
        m2 = jnp.maximum(m2, jnp.max(c2, axis=-1, keepdims=True))
    e1 = jnp.exp(s1 - m1)
    e2 = jnp.exp(s2 - m2)
    l1 = jnp.sum(e1, axis=-1, keepdims=True)
    l2 = jnp.sum(e2, axis=-1, keepdims=True)
    if prefix:
        f1 = jnp.exp(c1 - m1)
        f2 = jnp.exp(c2 - m2)
        l1 = l1 + jnp.sum(f1, axis=-1, keepdims=True)
        l2 = l2 + jnp.sum(f2, axis=-1, keepdims=True)
    w1 = 1.0 / l1
    w2 = lam / l2
    o = _dot((e1 * w1 - e2 * w2).astype(BF16), v_ref[...])
    if prefix:
        o = o + _dot((f1 * w1 - f2 * w2).astype(BF16), vc_ref[...])
    o = _rms(o) * gs_ref[...] * (1.0 - lam_init)
    o_ref[...] = o.astype(o_ref.dtype)


def _diff_attention(lam_params, hq, hk, kv_blk, hc, c_blk, tables, gsub, lam_init, tq):
    b, s, _ = hq.shape
    prefix = hc is not None
    rope = tables is not None
    grid = (b, DIFF_HEADS, s // tq)
    in_specs = [
        pl.BlockSpec((8, LANE), lambda bi, h, i: (0, 0)),
        pl.BlockSpec((None, tq, LANE), lambda bi, h, i: (bi, i, DQ_BLK + h)),
        pl.BlockSpec((None, s, LANE), lambda bi, h, i: (bi, 0, kv_blk + DK_REL + h)),
        pl.BlockSpec((None, s, LANE), lambda bi, h, i: (bi, 0, kv_blk + DV_REL + h)),
    ]
    args = [lam_params, hq, hk, hk]
    c = 0
    if prefix:
        c = hc.shape[1]
        in_specs += [
            pl.BlockSpec((None, c, LANE), lambda bi, h, i: (bi, 0, c_blk + DK_REL + h)),
            pl.BlockSpec((None, c, LANE), lambda bi, h, i: (bi, 0, c_blk + DV_REL + h)),
        ]
        args += [hc, hc]
    if rope:
        cos, sin = tables
        in_specs += [
            pl.BlockSpec((tq, LANE), lambda bi, h, i: (i, 0)),
            pl.BlockSpec((tq, LANE), lambda bi, h, i: (i, 0)),
            pl.BlockSpec((s, LANE), lambda bi, h, i: (0, 0)),
            pl.BlockSpec((s, LANE), lambda bi, h, i: (0, 0)),
        ]
        args += [cos, sin, cos, sin]
    in_specs.append(pl.BlockSpec((1, LANE), lambda bi, h, i: (0, 0)))
    args.append(gsub.reshape(1, LANE))
    est = 6 * s * LANE * 2 + 4 * s * LANE * 4 + 8 * tq * (s + c) * 4
    return pl.pallas_call(
        functools.partial(_diff_kernel, rope=rope, prefix=prefix, lam_init=lam_init),
        grid=grid,
        in_specs=in_specs,
        out_specs=pl.BlockSpec((None, tq, LANE), lambda bi, h, i: (bi, i, h)),
        out_shape=jax.ShapeDtypeStruct((b, s, DIFF_W), BF16),
        scratch_shapes=[pltpu.VMEM((s, LANE), BF16), pltpu.VMEM((8, LANE), F32)],
        compiler_params=pltpu.CompilerParams(
            dimension_semantics=("arbitrary",) * 3,
            vmem_limit_bytes=_vmem_limit(est)),
        name="diff_attention",
    )(*args)


def _hy_filter_kernel(z_ref, w1_ref, b1_ref, w2_ref, b2_ref, w3_ref, b3_ref, wo_ref, fr_ref,
                      bias_ref, dl_ref, fp_hi_ref, fp_lo_ref, fq_hi_ref, fq_lo_ref,
                      hp_ref, hq_ref, hs_hi, hs_lo, hd_hi, hd_lo, alt_ref):
    j = pl.program_id(0)

    @pl.when(j == 0)
    def _():
        z = z_ref[...]
        h = jnp.sin(fr_ref[0:1, :] * (_dot3(z, w1_ref[...]) + b1_ref[...]))
        h = jnp.sin(fr_ref[1:2, :] * (_dot3(h, w2_ref[...]) + b2_ref[...]))
        h = jnp.sin(fr_ref[2:3, :] * (_dot3(h, w3_ref[...]) + b3_ref[...]))
        h = _dot3(h, wo_ref[...])
        decay = jnp.exp(-z[:, 0:1] * jnp.abs(dl_ref[...])) + HY_SHIFT
        hf = h[:, :HY_CH] * decay
        hb = h[:, HY_CH:] * decay
        row = lax.broadcasted_iota(jnp.int32, hf.shape, 0)
        hf = hf + jnp.where(row == 0, bias_ref[...], 0.0)
        hb = jnp.where(row == 0, 0.0, hb)
        hs = hf + hb
        hd = hf - hb
        sign = jnp.where(row % 2 == 0, 1.0, -1.0)
        alt_ref[...] = jnp.broadcast_to(jnp.sum(sign * hs, axis=0, keepdims=True), alt_ref.shape)
        a, bb = _split_bf16(hs)
        hs_hi[...] = a
        hs_lo[...] = bb
        a, bb = _split_bf16(hd)
        hd_hi[...] = a
        hd_lo[...] = bb

    xp = _dot(fp_hi_ref[...], hs_hi[...]) + (_dot(fp_lo_ref[...], hs_hi[...]) + _dot(fp_hi_ref[...], hs_lo[...]))
    xq = _dot(fq_hi_ref[...], hd_hi[...]) + (_dot(fq_lo_ref[...], hd_hi[...]) + _dot(fq_hi_ref[...], hd_lo[...]))
    row = lax.broadcasted_iota(jnp.int32, xq.shape, 0)
    is0 = (row == 0) & (j == 0)
    hp_ref[...] = xp
    hq_ref[...] = jnp.where(is0, alt_ref[0:1, :], xq)


def _hy_filters(consts, w1, b1, w2, b2, w3, b3, wout, freq, bias):
    n = consts["n"]
    fc = consts["fc"]
    nf = n // fc
    pad = LANE - HY_FFN
    w1p = jnp.pad(w1, ((0, LANE - HY_EMB), (0, pad)))
    w2p = jnp.pad(w2, ((0, pad), (0, pad)))
    w3p = jnp.pad(w3, ((0, pad), (0, pad)))
    wop = jnp.pad(wout, ((0, pad), (0, 0)))
    b1p = jnp.pad(b1, (0, pad)).reshape(1, LANE)
    b2p = jnp.pad(b2, (0, pad)).reshape(1, LANE)
    b3p = jnp.pad(b3, (0, pad)).reshape(1, LANE)
    frp = jnp.pad(freq, ((0, 0), (0, pad)))
    full = lambda shape: pl.BlockSpec(shape, lambda j: (0,) * len(shape))
    fspec_p = pl.BlockSpec((fc, n), lambda j: (j, 0))
    fspec_q = pl.BlockSpec((fc, n), lambda j: (nf + j, 0))
    est = n * LANE * 4 * 2 + 8 * fc * n * 2 + 4 * n * HY_CH * 2 + 4 * fc * HY_CH * 4 + 8 * n * HY_CH * 4
    return pl.pallas_call(
        _hy_filter_kernel,
        grid=(nf,),
        in_specs=[
            full((n, LANE)), full((LANE, LANE)), full((1, LANE)), full((LANE, LANE)), full((1, LANE)),
            full((LANE, LANE)), full((1, LANE)), full((LANE, 2 * HY_CH)), full((3, LANE)),
            full((1, HY_CH)), full((1, HY_CH)),
            fspec_p, fspec_p, fspec_q, fspec_q,
        ],
        out_specs=[pl.BlockSpec((fc, HY_CH), lambda j: (j, 0)), pl.BlockSpec((fc, HY_CH), lambda j: (j, 0))],
        out_shape=[jax.ShapeDtypeStruct((n, HY_CH), F32), jax.ShapeDtypeStruct((n, HY_CH), F32)],
        scratch_shapes=[pltpu.VMEM((n, HY_CH), BF16)] * 4 + [pltpu.VMEM((8, HY_CH), F32)],
        compiler_params=pltpu.CompilerParams(
            dimension_semantics=("arbitrary",),
            vmem_limit_bytes=_vmem_limit(est)),
        name="hyena_filters",
    )(consts["zfeat"], w1p, b1p, w2p, b2p, w3p, b3p, wop, frp, bias.reshape(1, HY_CH), consts["deltas"],
      consts["f_hi"], consts["f_lo"], consts["f_hi"], consts["f_lo"])


def _short_conv(hy_ref, cw_ref, cb_ref, c0, n):
    x = hy_ref[:, c0:c0 + LANE].astype(F32)
    row = lax.broadcasted_iota(jnp.int32, x.shape, 0)
    prv = jnp.where(row == 0, 0.0, pltpu.roll(x, 1, axis=0))
    nxt = jnp.where(row == n - 1, 0.0, pltpu.roll(x, n - 1, axis=0))
    w = cw_ref[:, c0:c0 + LANE]
    return prv * w[0:1] + x * w[1:2] + nxt * w[2:3] + cb_ref[:, c0:c0 + LANE]


def _hy_mixer_kernel(hy_ref, cw_ref, cb_ref, fp_ref, fq_ref, hp_ref, hq_ref, gp_ref, gq_ref, gn_ref,
                     o_ref, z_ref, y_ref, *, n, nf):
    j = pl.program_id(1)
    nchunk = HY_CH // LANE

    @pl.when(j == 0)
    def _():
        for c in range(nchunk):
            x1 = _short_conv(hy_ref, cw_ref, cb_ref, HY_CH + c * LANE, n)
            v = _short_conv(hy_ref, cw_ref, cb_ref, 2 * HY_CH + c * LANE, n)
            z_ref[:, c * LANE:(c + 1) * LANE] = (v * x1).astype(BF16)
        y_ref[...] = jnp.zeros_like(y_ref)

    z = z_ref[...]
    xp = _dot(fp_ref[...], z)
    xq = _dot(fq_ref[...], z)
    hp = hp_ref[...]
    hq = hq_ref[...]
    row = lax.broadcasted_iota(jnp.int32, xp.shape, 0)
    is0 = (row == 0) & (j == 0)
    qq = xq * hq
    yp = xp * hp - jnp.where(is0, 0.0, qq)
    yq = jnp.where(is0, qq, xp * hq + xq * hp)
    y_ref[...] += _dot(gp_ref[...], yp.astype(BF16)) + _dot(gq_ref[...], yq.astype(BF16))

    @pl.when(j == nf - 1)
    def _():
        ss = jnp.zeros((n, 1), F32)
        for c in range(nchunk):
            x0 = _short_conv(hy_ref, cw_ref, cb_ref, c * LANE, n)
            prod = x0 * y_ref[:, c * LANE:(c + 1) * LANE]
            y_ref[:, c * LANE:(c + 1) * LANE] = prod
            ss = ss + jnp.sum(prod * prod, axis=-1, keepdims=True)
        inv = lax.rsqrt(ss * (1.0 / HY_CH) + EPS)
        o_ref[...] = (y_ref[...] * inv * gn_ref[...]).astype(o_ref.dtype)


def _hy_mixer(h, consts, conv_w, conv_b, hp, hq, gnorm):
    b, n, _ = h.shape
    fc = consts["fc"]
    nf = n // fc
    est = (n * 3 * HY_CH * 2 + 8 * fc * n * 2 + 4 * fc * HY_CH * 4 + n * HY_CH * (2 + 4 + 4)
           + 6 * fc * HY_CH * 4 + 6 * n * LANE * 4)
    return pl.pallas_call(
        functools.partial(_hy_mixer_kernel, n=n, nf=nf),
        grid=(b, nf),
        in_specs=[
            pl.BlockSpec((None, n, 3 * HY_CH), lambda bi, j: (bi, 0, HY_OFF // (3 * HY_CH)),
                         pipeline_mode=pl.Buffered(1)),
            pl.BlockSpec((3, 3 * HY_CH), lambda bi, j: (0, 0)),
            pl.BlockSpec((1, 3 * HY_CH), lambda bi, j: (0, 0)),
            pl.BlockSpec((fc, n), lambda bi, j: (j, 0)),
            pl.BlockSpec((fc, n), lambda bi, j: (nf + j, 0)),
            pl.BlockSpec((fc, HY_CH), lambda bi, j: (j, 0)),
            pl.BlockSpec((fc, HY_CH), lambda bi, j: (j, 0)),
            pl.BlockSpec((n, fc), lambda bi, j: (0, j)),
            pl.BlockSpec((n, fc), lambda bi, j: (0, nf + j)),
            pl.BlockSpec((1, HY_CH), lambda bi, j: (0, 0)),
        ],
        out_specs=pl.BlockSpec((None, n, HY_CH), lambda bi, j: (bi, 0, 0)),
        out_shape=jax.ShapeDtypeStruct((b, n, HY_CH), BF16),
        scratch_shapes=[pltpu.VMEM((n, HY_CH), BF16), pltpu.VMEM((n, HY_CH), F32)],
        compiler_params=pltpu.CompilerParams(
            dimension_semantics=("arbitrary", "arbitrary"),
            vmem_limit_bytes=_vmem_limit(est)),
        name="hyena_mixer",
    )(h, conv_w, conv_b.reshape(1, 3 * HY_CH), consts["f_hi"], consts["f_hi"], hp, hq,
      consts["g"], consts["g"], gnorm.reshape(1, HY_CH))


def _hy_constants(n):
    big = 2 * n
    t = jnp.linspace(0.0, 1.0, n, dtype=F32)[:, None]
    w = 2.0 * math.pi * jnp.arange(n, dtype=F32)[:, None] / n
    f = jnp.linspace(1e-4, HY_BANDS - 1, HY_BANDS, dtype=F32)[None, :]
    zfeat = jnp.concatenate([t, jnp.cos(f * w), -jnp.sin(f * w)], axis=-1)
    zfeat = jnp.pad(zfeat, ((0, 0), (0, LANE - HY_EMB)))
    min_decay = math.log(HY_TARGET) / HY_SLOW_DECAY
    max_decay = math.log(HY_TARGET) / HY_FAST_DECAY
    deltas = jnp.linspace(min_decay, max_decay, HY_CH, dtype=F32)[None, :]

    k = jnp.arange(n, dtype=jnp.int32)
    theta = ((k[:, None] * k[None, :]) % big).astype(F32) * (2.0 * math.pi / big)
    cosm = jnp.cos(theta)
    sinm = jnp.sin(theta)
    alt = jnp.where(k % 2 == 0, 1.0, -1.0).astype(F32)
    fq = jnp.where((k == 0)[:, None], alt[None, :], -sinm)
    fmat = jnp.concatenate([cosm, fq], axis=0)
    f_hi = fmat.astype(BF16)
    f_lo = (fmat - f_hi.astype(F32)).astype(BF16)
    wk = jnp.where(k == 0, 1.0, 2.0).astype(F32)[None, :] / big
    gp = cosm * wk
    gq = jnp.where((k == 0)[None, :], alt[:, None] / big, -sinm * wk)
    g = jnp.concatenate([gp, gq], axis=1).astype(BF16)
    return dict(n=n, fc=min(n, 512), zfeat=zfeat, deltas=deltas, f_hi=f_hi, f_lo=f_lo, g=g)


def _out_proj_kernel(od_ref, og_ref, oh_ref, w_ref, x_ref, g_ref, m_ref, o_ref):
    mix = _dot(od_ref[...], w_ref[0:DIFF_W, :])
    mix = mix + _dot(og_ref[...], w_ref[DIFF_W:DIFF_W + GQA_W, :])
    mix = mix + _dot(oh_ref[...], w_ref[DIFF_W + GQA_W:, :])
    o_ref[...] = x_ref[...] + m_ref[2:3, :] * (_rms(mix) * g_ref[...])


def _out_proj(o_d, o_g, o_h, w, x, g, mods, per_batch):
    b, s, d = x.shape
    tm = min(s, 512)
    est = d * d * 2 + 2 * tm * d * 2 + 4 * tm * d * 4 + 2 * tm * d * 4
    return pl.pallas_call(
        _out_proj_kernel,
        grid=(b, s // tm),
        in_specs=[
            pl.BlockSpec((None, tm, DIFF_W), lambda bi, i: (bi, i, 0)),
            pl.BlockSpec((None, tm, GQA_W), lambda bi, i: (bi, i, 0)),
            pl.BlockSpec((None, tm, HY_CH), lambda bi, i: (bi, i, 0)),
            pl.BlockSpec((d, d), lambda bi, i: (0, 0), pipeline_mode=pl.Buffered(1)),
            pl.BlockSpec((None, tm, d), lambda bi, i: (bi, i, 0)),
            pl.BlockSpec((1, d), lambda bi, i: (0, 0)),
            pl.BlockSpec((None, N_MOD, d), lambda bi, i: (bi * per_batch, 0, 0)),
        ],
        out_specs=pl.BlockSpec((None, tm, d), lambda bi, i: (bi, i, 0)),
        out_shape=jax.ShapeDtypeStruct((b, s, d), F32),
        compiler_params=pltpu.CompilerParams(
            dimension_semantics=("parallel", "parallel"),
            vmem_limit_bytes=_vmem_limit(est)),
        name="out_proj",
    )(o_d, o_g, o_h, w, x, g.reshape(1, d), mods)


def _mlp_kernel(x_ref, g2_ref, g3_ref, m_ref, wu_ref, wd_ref, o_ref, xn_ref, *, nf, ncol):
    f = pl.program_id(2)

    @pl.when(f == 0)
    def _():
        y = _rms(x_ref[...]) * g2_ref[...]
        xn_ref[...] = (y * (1.0 + m_ref[4:5, :]) + m_ref[3:4, :]).astype(BF16)
        o_ref[...] = jnp.zeros_like(o_ref)

    hmid = jnp.square(jnp.maximum(_dot(xn_ref[...], wu_ref[...]), 0.0)).astype(BF16)
    d = o_ref.shape[-1]
    for c in range(d // ncol):
        o_ref[:, c * ncol:(c + 1) * ncol] += _dot(hmid, wd_ref[:, c * ncol:(c + 1) * ncol])

    @pl.when(f == nf - 1)
    def _():
        o_ref[...] = x_ref[...] + m_ref[5:6, :] * (_rms(o_ref[...]) * g3_ref[...])


def _mlp(x, g2, g3, mods, w_up, w_down, per_batch):
    b, s, d = x.shape
    dff = w_up.shape[1]
    tm = min(s, 1024)
    tf = 512
    nf = dff // tf
    est = tm * d * 4 + 2 * tm * d * 4 + tm * d * 2 + 4 * d * tf * 2 + tm * tf * 6 + tm * 512 * 4
    return pl.pallas_call(
        functools.partial(_mlp_kernel, nf=nf, ncol=512),
        grid=(b, s // tm, nf),
        in_specs=[
            pl.BlockSpec((None, tm, d), lambda bi, i, f: (bi, i, 0), pipeline_mode=pl.Buffered(1)),
            pl.BlockSpec((1, d), lambda bi, i, f: (0, 0)),
            pl.BlockSpec((1, d), lambda bi, i, f: (0, 0)),
            pl.BlockSpec((None, N_MOD, d), lambda bi, i, f: (bi * per_batch, 0, 0)),
            pl.BlockSpec((d, tf), lambda bi, i, f: (0, f)),
            pl.BlockSpec((tf, d), lambda bi, i, f: (f, 0)),
        ],
        out_specs=pl.BlockSpec((None, tm, d), lambda bi, i, f: (bi, i, 0)),
        out_shape=jax.ShapeDtypeStruct((b, s, d), F32),
        scratch_shapes=[pltpu.VMEM((tm, d), BF16)],
        compiler_params=pltpu.CompilerParams(
            dimension_semantics=("parallel", "parallel", "arbitrary"),
            vmem_limit_bytes=_vmem_limit(est)),
        name="mlp",
    )(x, g2.reshape(1, d), g3.reshape(1, d), mods, w_up, w_down)


def _rope_tables(n_rows, head_dim):
    t_row = jnp.repeat(jnp.arange(n_rows, dtype=F32), GRID_W)
    t_col = jnp.tile(jnp.arange(GRID_W, dtype=F32), n_rows)
    d_axis = head_dim // 2
    inv = ROPE_THETA ** (-jnp.arange(0, d_axis, 2, dtype=F32) / d_axis)
    ang = jnp.concatenate([t_row[:, None] * inv, t_col[:, None] * inv], axis=-1)
    cos = jnp.repeat(jnp.cos(ang), 2, axis=-1)
    sin = jnp.repeat(jnp.sin(ang), 2, axis=-1)
    sign = jnp.where(jnp.arange(head_dim) % 2 == 0, -1.0, 1.0).astype(F32)
    sin = sin * sign
    reps = LANE // head_dim
    return jnp.tile(cos, (1, reps)), jnp.tile(sin, (1, reps))


def kernel(x, c, ctx, c_ctx, w_mod, b_mod, g_norm, w_in, w_out, diff_lam, diff_subln, gqa_q_norm,
           gqa_k_norm, gqa_out_norm, hy_conv_w, hy_conv_b, hy_w1, hy_b1, hy_w2, hy_b2, hy_w3, hy_b3,
           hy_wout, hy_freq, hy_bias, hy_out_norm, w_up, w_down):
    bsz, n_lat, d = x.shape
    n_ctx = ctx.shape[1]
    depth = w_mod.shape[0]
    assert d == D_MODEL and bsz + 1 <= MOD_ROWS
    assert n_lat % GRID_W == 0 and n_lat % 256 == 0 and n_ctx % 128 == 0

    tab_d = _rope_tables(n_lat // GRID_W, DIFF_QK)
    tab_g = _rope_tables(n_lat // GRID_W, GQA_HD)
    hyc_lat = _hy_constants(n_lat)
    hyc_ctx = _hy_constants(n_ctx)

    cpad = jnp.zeros((MOD_ROWS, d), F32).at[:bsz].set(c).at[bsz].set(c_ctx)
    mods = _modulations(cpad, w_mod, b_mod)
    lam_pad = jnp.pad(diff_lam, ((0, 0), (0, 4), (0, LANE - DIFF_QK)))

    tq_lat = min(n_lat, 512)
    tq_ctx = min(n_ctx, 256)
    xc = ctx
    for l in range(depth):
        last = l == depth - 1
        lam_init = 0.8 - 0.6 * math.exp(-0.3 * l)
        m_lat = mods[l, :bsz].reshape(bsz, N_MOD, d)
        m_ctx = mods[l, bsz:bsz + 1].reshape(1, N_MOD, d)
        w_in_l = w_in[l].astype(BF16)
        w_out_l = w_out[l].astype(BF16)
        w_up_l = w_up[l].astype(BF16)
        w_down_l = w_down[l].astype(BF16)
        filt = (hy_w1[l], hy_b1[l], hy_w2[l], hy_b2[l], hy_w3[l], hy_b3[l], hy_wout[l], hy_freq[l], hy_bias[l])

        h = _in_proj(x, g_norm[l, 0], m_lat, w_in_l, 1)
        xc_flat = xc.reshape(1, bsz * n_ctx, d)
        hc = _in_proj(xc_flat, g_norm[l, 0], m_ctx, w_in_l[:, KV_OFF:] if last else w_in_l, 0)
        hc = hc.reshape(bsz, n_ctx, -1)
        c_blk = 0 if last else KV_BLK

        o_d = _diff_attention(lam_pad[l], h, h, KV_BLK, hc, c_blk, tab_d, diff_subln[l], lam_init, tq_lat)
        o_g = _gqa_attention(h, h, KV_BLK, hc, c_blk, tab_g, gqa_q_norm[l], gqa_k_norm[l], gqa_out_norm[l],
                             tq_lat)
        hp, hq = _hy_filters(hyc_lat, *filt)
        o_h = _hy_mixer(h, hyc_lat, hy_conv_w[l], hy_conv_b[l], hp, hq, hy_out_norm[l])
        x_new = _out_proj(o_d, o_g, o_h, w_out_l, x, g_norm[l, 1], m_lat, 1)

        if not last:
            oc_d = _diff_attention(lam_pad[l], hc, hc, KV_BLK, None, 0, None, diff_subln[l], lam_init, tq_ctx)
            oc_g = _gqa_attention(hc, hc, KV_BLK, None, 0, None, gqa_q_norm[l], gqa_k_norm[l],
                                  gqa_out_norm[l], tq_ctx)
            hpc, hqc = _hy_filters(hyc_ctx, *filt)
            oc_h = _hy_mixer(hc, hyc_ctx, hy_conv_w[l], hy_conv_b[l], hpc, hqc, hy_out_norm[l])
            flat = lambda a: a.reshape(1, bsz * n_ctx, a.shape[-1])
            xc_flat = _out_proj(flat(oc_d), flat(oc_g), flat(oc_h), w_out_l, xc_flat, g_norm[l, 1], m_ctx, 0)
        x = x_new

        x = _mlp(x, g_norm[l, 2], g_norm[l, 3], m_lat, w_up_l, w_down_l, 1)
        if not last:
            xc_flat = _mlp(xc_flat, g_norm[l, 2], g_norm[l, 3], m_ctx, w_up_l, w_down_l, 0)
            xc = xc_flat.reshape(bsz, n_ctx, d)
    return x
```

```python
import functools
import math

import jax
import jax.numpy as jnp
from jax import lax
from jax.experimental import pallas as pl
from jax.experimental.pallas import tpu as pltpu

F32 = jnp.float32
BF16 = jnp.bfloat16

D_MODEL = 2048
GRID_W = 64
DIFF_W = 512
GQA_W = 1024
HY_CH = 512
DIFF_V = 128
DIFF_QK = 64
DIFF_HEADS = 4
GQA_HD = 128
GQA_KV_HEADS = 2
GQA_GROUP = 4
HY_EMB = 33
HY_BANDS = 16
HY_FFN = 64
HY_FAST_DECAY = 0.3
HY_SLOW_DECAY = 1.5
HY_TARGET = 1e-2
HY_SHIFT = 0.0
D_FF = 4 * D_MODEL
N_MOD = 6
ROPE_THETA = 10000.0
EPS = 1e-6

LANE = 128
DQ_BLK = 0
GQ_BLK = 4
HY_OFF = 1536
KV_OFF = 3072
KV_BLK = KV_OFF // LANE
DK_REL, DV_REL, GK_REL, GV_REL = 0, 4, 8, 10
N_IN = 4608
KV_W = N_IN - KV_OFF

VMEM_CAP = 64 * 1024 * 1024
MOD_ROWS = 24
PREP_ROWS = 256
LOG2E = 1.4426950408889634


def _vmem_limit(nbytes):
    return int(min(max(nbytes * 5 // 4 + (4 << 20), 32 << 20), VMEM_CAP - (6 << 20)))


def _rms(x):
    return x * lax.rsqrt(jnp.mean(x * x, axis=-1, keepdims=True) + EPS)


def _dot(a, b):
    return jnp.dot(a, b, preferred_element_type=F32)


def _dot_nt(a, b):
    return lax.dot_general(a, b, (((1,), (1,)), ((), ())), preferred_element_type=F32)


def _split_bf16(a):
    hi = a.astype(BF16)
    lo = (a - hi.astype(F32)).astype(BF16)
    return hi, lo


def _dot3(a, b):
    ah, al = _split_bf16(a)
    bh, bl = _split_bf16(b)
    return _dot(ah, bh) + (_dot(al, bh) + _dot(ah, bl))


def _rope(x, cos, sin_signed):
    lane = lax.broadcasted_iota(jnp.int32, x.shape, 1)
    nxt = pltpu.roll(x, LANE - 1, axis=1)
    prv = pltpu.roll(x, 1, axis=1)
    swapped = jnp.where(lane % 2 == 0, nxt, prv)
    return x * cos + swapped * sin_signed


def _mod_kernel(c_ref, w_ref, b_ref, o_ref):
    c = c_ref[...]
    s = c * (1.0 / (1.0 + jnp.exp(-c)))
    o_ref[0] = _dot(s.astype(BF16), w_ref[0].astype(BF16)) + b_ref[0]


def _modulations(cpad, w_mod, b_mod):
    nl, d, n = w_mod.shape
    tn = 1024
    return pl.pallas_call(
        _mod_kernel,
        grid=(nl, n // tn),
        in_specs=[
            pl.BlockSpec((MOD_ROWS, d), lambda l, j: (0, 0)),
            pl.BlockSpec((1, d, tn), lambda l, j: (l, 0, j)),
            pl.BlockSpec((1, 1, tn), lambda l, j: (l, 0, j)),
        ],
        out_specs=pl.BlockSpec((1, MOD_ROWS, tn), lambda l, j: (l, 0, j)),
        out_shape=jax.ShapeDtypeStruct((nl, MOD_ROWS, n), F32),
        compiler_params=pltpu.CompilerParams(
            dimension_semantics=("arbitrary", "arbitrary"),
            vmem_limit_bytes=_vmem_limit(2 * d * tn * 4 + d * tn * 2)),
        name="modulations",
    )(cpad, w_mod, b_mod.reshape(nl, 1, n))


def _in_proj_kernel(x_ref, g_ref, m_ref, w_ref, o_ref, xn_ref):
    @pl.when(pl.program_id(2) == 0)
    def _():
        y = _rms(x_ref[...]) * g_ref[...]
        xn = y * (1.0 + m_ref[1:2, :]) + m_ref[0:1, :]
        xn_ref[...] = xn.astype(BF16)

    o_ref[...] = _dot(xn_ref[...], w_ref[...]).astype(o_ref.dtype)


def _in_proj(x, g, mods, w, per_batch):
    b, s, d = x.shape
    n = w.shape[1]
    tm = min(s, 1024)
    tn = 512
    est = 2 * tm * d * 4 + tm * d * 2 + 2 * d * tn * 2 + 2 * tm * tn * 2 + tm * tn * 4
    return pl.pallas_call(
        _in_proj_kernel,
        grid=(b, s // tm, n // tn),
        in_specs=[
            pl.BlockSpec((None, tm, d), lambda bi, i, j: (bi, i, 0)),
            pl.BlockSpec((1, d), lambda bi, i, j: (0, 0)),
            pl.BlockSpec((None, N_MOD, d), lambda bi, i, j: (bi * per_batch, 0, 0)),
            pl.BlockSpec((d, tn), lambda bi, i, j: (0, j)),
        ],
        out_specs=pl.BlockSpec((None, tm, tn), lambda bi, i, j: (bi, i, j)),
        out_shape=jax.ShapeDtypeStruct((b, s, n), BF16),
        scratch_shapes=[pltpu.VMEM((tm, d), BF16)],
        compiler_params=pltpu.CompilerParams(
            dimension_semantics=("parallel", "parallel", "arbitrary"),
            vmem_limit_bytes=_vmem_limit(est)),
        name="in_proj",
    )(x, g.reshape(1, d), mods, w)


def _gqa_kernel(*refs, rope, prefix, heads_per_step):
    it = iter(refs)
    q_ref, k_ref, v_ref = next(it), next(it), next(it)
    kc_ref = vc_ref = None
    if prefix:
        kc_ref, vc_ref = next(it), next(it)
    cq_ref = sq_ref = ck_ref = sk_ref = None
    if rope:
        cq_ref, sq_ref, ck_ref, sk_ref = next(it), next(it), next(it), next(it)
    gq_ref, gk_ref, go_ref = next(it), next(it), next(it)
    o_ref = next(it)
    kall_ref, vt_ref = next(it), next(it)

    c = kc_ref.shape[0] if prefix else 0
    s = k_ref.shape[0]

    @pl.when((pl.program_id(2) == 0) & (pl.program_id(3) == 0))
    def _():
        if prefix:
            kall_ref[0:c, :] = (_rms(kc_ref[...].astype(F32)) * gk_ref[...]).astype(BF16)
            vt_ref[:, 0:c] = vc_ref[...].astype(F32).T.astype(BF16)
        pr = min(s, PREP_ROWS)
        for r0 in range(0, s, pr):
            rows = slice(r0, r0 + pr)
            k = _rms(k_ref[rows, :].astype(F32)) * gk_ref[...]
            if rope:
                k = _rope(k, ck_ref[rows, :], sk_ref[rows, :])
            kall_ref[c + r0:c + r0 + pr, :] = k.astype(BF16)
            vt_ref[:, c + r0:c + r0 + pr] = v_ref[rows, :].astype(F32).T.astype(BF16)

    for hh in range(heads_per_step):
        cols = slice(hh * LANE, (hh + 1) * LANE)
        q = _rms(q_ref[:, cols].astype(F32)) * gq_ref[...]
        if rope:
            q = _rope(q, cq_ref[...], sq_ref[...])
        qt = (q * (GQA_HD ** -0.5 * LOG2E)).T.astype(BF16)
        st = _dot(kall_ref[...], qt)
        m = jnp.max(st, axis=0, keepdims=True)
        p = jnp.exp2(st - m)
        denom = jnp.sum(p, axis=0, keepdims=True)
        ot = _dot(vt_ref[...], p.astype(BF16))
        ot = ot / denom
        ot = ot * lax.rsqrt(jnp.mean(ot * ot, axis=0, keepdims=True) + EPS) * go_ref[...]
        o_ref[:, cols] = ot.T.astype(o_ref.dtype)


def _gqa_attention(hq, hk, kv_blk, hc, c_blk, tables, gq, gk, go, tq, heads_per_step):
    b, s, _ = hq.shape
    prefix = hc is not None
    rope = tables is not None
    hps = heads_per_step
    grid = (b, GQA_KV_HEADS, GQA_GROUP // hps, s // tq)
    in_specs = [
        pl.BlockSpec((None, tq, hps * LANE), lambda bi, g, r, i: (bi, i, (GQ_BLK + g * GQA_GROUP) // hps + r)),
        pl.BlockSpec((None, s, LANE), lambda bi, g, r, i: (bi, 0, kv_blk + GK_REL + g)),
        pl.BlockSpec((None, s, LANE), lambda bi, g, r, i: (bi, 0, kv_blk + GV_REL + g)),
    ]
    args = [hq, hk, hk]
    c = 0
    if prefix:
        c = hc.shape[1]
        in_specs += [
            pl.BlockSpec((None, c, LANE), lambda bi, g, r, i: (bi, 0, c_blk + GK_REL + g)),
            pl.BlockSpec((None, c, LANE), lambda bi, g, r, i: (bi, 0, c_blk + GV_REL + g)),
        ]
        args += [hc, hc]
    if rope:
        cos, sin = tables
        in_specs += [
            pl.BlockSpec((tq, LANE), lambda bi, g, r, i: (i, 0)),
            pl.BlockSpec((tq, LANE), lambda bi, g, r, i: (i, 0)),
            pl.BlockSpec((s, LANE), lambda bi, g, r, i: (0, 0)),
            pl.BlockSpec((s, LANE), lambda bi, g, r, i: (0, 0)),
        ]
        args += [cos, sin, cos, sin]
    vec = pl.BlockSpec((1, LANE), lambda bi, g, r, i: (0, 0))
    in_specs += [vec, vec, pl.BlockSpec((LANE, 1), lambda bi, g, r, i: (0, 0))]
    args += [gq.reshape(1, LANE), gk.reshape(1, LANE), go.reshape(LANE, 1)]
    lk = s + c
    est = 4 * s * LANE * 2 + 4 * s * LANE * 4 + 2 * lk * LANE * 2 + hps * 3 * tq * lk * 4
    return pl.pallas_call(
        functools.partial(_gqa_kernel, rope=rope, prefix=prefix, heads_per_step=hps),
        grid=grid,
        in_specs=in_specs,
        out_specs=pl.BlockSpec((None, tq, hps * LANE), lambda bi, g, r, i: (bi, i, g * GQA_GROUP // hps + r)),
        out_shape=jax.ShapeDtypeStruct((b, s, GQA_W), BF16),
        scratch_shapes=[pltpu.VMEM((lk, LANE), BF16), pltpu.VMEM((LANE, lk), BF16)],
        compiler_params=pltpu.CompilerParams(
            dimension_semantics=("arbitrary",) * 4,
            vmem_limit_bytes=_vmem_limit(est)),
        name="gqa_attention",
    )(*args)


def _diff_kernel(*refs, rope, prefix, lam_init):
    it = iter(refs)
    lp_ref, q_ref, k_ref, v_ref = next(it), next(it), next(it), next(it)
    kc_ref = vc_ref = None
    if prefix:
        kc_ref, vc_ref = next(it), next(it)
    cq_ref = sq_ref = ck_ref = sk_ref = None
    if rope:
        cq_ref, sq_ref, ck_ref, sk_ref = next(it), next(it), next(it), next(it)
    gs_ref = next(it)
    o_ref = next(it)
    kall_ref, vt_ref, lam_ref = next(it), next(it), next(it)
    c = kc_ref.shape[0] if prefix else 0
    s = k_ref.shape[0]

    first = (pl.program_id(0) == 0) & (pl.program_id(1) == 0) & (pl.program_id(2) == 0)

    @pl.when(first)
    def _():
        p = lp_ref[...]
        t1 = jnp.sum(p[0:1] * p[1:2], axis=-1, keepdims=True)
        t2 = jnp.sum(p[2:3] * p[3:4], axis=-1, keepdims=True)
        lam_ref[...] = jnp.broadcast_to(jnp.exp(t1) - jnp.exp(t2) + lam_init, lam_ref.shape)

    @pl.when(pl.program_id(2) == 0)
    def _():
        if prefix:
            kall_ref[0:c, :] = kc_ref[...]
            vt_ref[:, 0:c] = vc_ref[...].astype(F32).T.astype(BF16)
        pr = min(s, PREP_ROWS)
        for r0 in range(0, s, pr):
            rows = slice(r0, r0 + pr)
            if rope:
                k = _rope(k_ref[rows, :].astype(F32), ck_ref[rows, :], sk_ref[rows, :]).astype(BF16)
            else:
                k = k_ref[rows, :]
            kall_ref[c + r0:c + r0 + pr, :] = k
            vt_ref[:, c + r0:c + r0 + pr] = v_ref[rows, :].astype(F32).T.astype(BF16)

    lam = lam_ref[0:1, 0:1]
    q = q_ref[...].astype(F32)
    if rope:
        q = _rope(q, cq_ref[...], sq_ref[...])
    qt = (q * (DIFF_QK ** -0.5 * LOG2E)).T
    row = lax.broadcasted_iota(jnp.int32, qt.shape, 0)
    q1 = jnp.where(row < DIFF_QK, qt, 0.0).astype(BF16)
    q2 = jnp.where(row >= DIFF_QK, qt, 0.0).astype(BF16)

    kall = kall_ref[...]
    s1 = _dot(kall, q1)
    s2 = _dot(kall, q2)
    e1 = jnp.exp2(s1 - jnp.max(s1, axis=0, keepdims=True))
    e2 = jnp.exp2(s2 - jnp.max(s2, axis=0, keepdims=True))
    w1 = 1.0 / jnp.sum(e1, axis=0, keepdims=True)
    w2 = lam / jnp.sum(e2, axis=0, keepdims=True)
    ot = _dot(vt_ref[...], (e1 * w1 - e2 * w2).astype(BF16))
    ot = ot * lax.rsqrt(jnp.mean(ot * ot, axis=0, keepdims=True) + EPS) * (gs_ref[...] * (1.0 - lam_init))
    o_ref[...] = ot.T.astype(o_ref.dtype)


def _diff_attention(lam_params, hq, hk, kv_blk, hc, c_blk, tables, gsub, lam_init, tq):
    b, s, _ = hq.shape
    prefix = hc is not None
    rope = tables is not None
    grid = (b, DIFF_HEADS, s // tq)
    in_specs = [
        pl.BlockSpec((8, LANE), lambda bi, h, i: (0, 0)),
        pl.BlockSpec((None, tq, LANE), lambda bi, h, i: (bi, i, DQ_BLK + h)),
        pl.BlockSpec((None, s, LANE), lambda bi, h, i: (bi, 0, kv_blk + DK_REL + h)),
        pl.BlockSpec((None, s, LANE), lambda bi, h, i: (bi, 0, kv_blk + DV_REL + h)),
    ]
    args = [lam_params, hq, hk, hk]
    c = 0
    if prefix:
        c = hc.shape[1]
        in_specs += [
            pl.BlockSpec((None, c, LANE), lambda bi, h, i: (bi, 0, c_blk + DK_REL + h)),
            pl.BlockSpec((None, c, LANE), lambda bi, h, i: (bi, 0, c_blk + DV_REL + h)),
        ]
        args += [hc, hc]
    if rope:
        cos, sin = tables
        in_specs += [
            pl.BlockSpec((tq, LANE), lambda bi, h, i: (i, 0)),
            pl.BlockSpec((tq, LANE), lambda bi, h, i: (i, 0)),
            pl.BlockSpec((s, LANE), lambda bi, h, i: (0, 0)),
            pl.BlockSpec((s, LANE), lambda bi, h, i: (0, 0)),
        ]
        args += [cos, sin, cos, sin]
    in_specs.append(pl.BlockSpec((LANE, 1), lambda bi, h, i: (0, 0)))
    args.append(gsub.reshape(LANE, 1))
    lk = s + c
    est = 4 * s * LANE * 2 + 4 * s * LANE * 4 + 2 * lk * LANE * 2 + 5 * tq * lk * 4
    return pl.pallas_call(
        functools.partial(_diff_kernel, rope=rope, prefix=prefix, lam_init=lam_init),
        grid=grid,
        in_specs=in_specs,
        out_specs=pl.BlockSpec((None, tq, LANE), lambda bi, h, i: (bi, i, h)),
        out_shape=jax.ShapeDtypeStruct((b, s, DIFF_W), BF16),
        scratch_shapes=[pltpu.VMEM((lk, LANE), BF16), pltpu.VMEM((LANE, lk), BF16), pltpu.VMEM((8, LANE), F32)],
        compiler_params=pltpu.CompilerParams(
            dimension_semantics=("arbitrary",) * 3,
            vmem_limit_bytes=_vmem_limit(est)),
        name="diff_attention",
    )(*args)


def _hy_filter_kernel(z_ref, w1_ref, b1_ref, w2_ref, b2_ref, w3_ref, b3_ref, wo_ref, fr_ref,
                      bias_ref, dl_ref, fp_hi_ref, fp_lo_ref, fq_hi_ref, fq_lo_ref,
                      hp_ref, hq_ref, hs_hi, hs_lo, hd_hi, hd_lo, alt_ref):
    j = pl.program_id(0)

    @pl.when(j == 0)
    def _():
        z = z_ref[...]
        h = jnp.sin(fr_ref[0:1, :] * (_dot3(z, w1_ref[...]) + b1_ref[...]))
        h = jnp.sin(fr_ref[1:2, :] * (_dot3(h, w2_ref[...]) + b2_ref[...]))
        h = jnp.sin(fr_ref[2:3, :] * (_dot3(h, w3_ref[...]) + b3_ref[...]))
        h = _dot3(h, wo_ref[...])
        decay = jnp.exp(-z[:, 0:1] * jnp.abs(dl_ref[...])) + HY_SHIFT
        hf = h[:, :HY_CH] * decay
        hb = h[:, HY_CH:] * decay
        row = lax.broadcasted_iota(jnp.int32, hf.shape, 0)
        hf = hf + jnp.where(row == 0, bias_ref[...], 0.0)
        hb = jnp.where(row == 0, 0.0, hb)
        hs = hf + hb
        hd = hf - hb
        sign = jnp.where(row % 2 == 0, 1.0, -1.0)
        alt_ref[...] = jnp.broadcast_to(jnp.sum(sign * hs, axis=0, keepdims=True), alt_ref.shape)
        a, bb = _split_bf16(hs)
        hs_hi[...] = a
        hs_lo[...] = bb
        a, bb = _split_bf16(hd)
        hd_hi[...] = a
        hd_lo[...] = bb

    xp = _dot(fp_hi_ref[...], hs_hi[...]) + (_dot(fp_lo_ref[...], hs_hi[...]) + _dot(fp_hi_ref[...], hs_lo[...]))
    xq = _dot(fq_hi_ref[...], hd_hi[...]) + (_dot(fq_lo_ref[...], hd_hi[...]) + _dot(fq_hi_ref[...], hd_lo[...]))
    row = lax.broadcasted_iota(jnp.int32, xq.shape, 0)
    is0 = (row == 0) & (j == 0)
    hp_ref[...] = xp
    hq_ref[...] = jnp.where(is0, alt_ref[0:1, :], xq)


def _hy_filters(consts, w1, b1, w2, b2, w3, b3, wout, freq, bias):
    n = consts["n"]
    fc = consts["fc"]
    nf = n // fc
    pad = LANE - HY_FFN
    w1p = jnp.pad(w1, ((0, LANE - HY_EMB), (0, pad)))
    w2p = jnp.pad(w2, ((0, pad), (0, pad)))
    w3p = jnp.pad(w3, ((0, pad), (0, pad)))
    wop = jnp.pad(wout, ((0, pad), (0, 0)))
    b1p = jnp.pad(b1, (0, pad)).reshape(1, LANE)
    b2p = jnp.pad(b2, (0, pad)).reshape(1, LANE)
    b3p = jnp.pad(b3, (0, pad)).reshape(1, LANE)
    frp = jnp.pad(freq, ((0, 0), (0, pad)))
    full = lambda shape: pl.BlockSpec(shape, lambda j: (0,) * len(shape))
    fspec_p = pl.BlockSpec((fc, n), lambda j: (j, 0))
    fspec_q = pl.BlockSpec((fc, n), lambda j: (nf + j, 0))
    est = n * LANE * 4 * 2 + 8 * fc * n * 2 + 4 * n * HY_CH * 2 + 4 * fc * HY_CH * 4 + 8 * n * HY_CH * 4
    return pl.pallas_call(
        _hy_filter_kernel,
        grid=(nf,),
        in_specs=[
            full((n, LANE)), full((LANE, LANE)), full((1, LANE)), full((LANE, LANE)), full((1, LANE)),
            full((LANE, LANE)), full((1, LANE)), full((LANE, 2 * HY_CH)), full((3, LANE)),
            full((1, HY_CH)), full((1, HY_CH)),
            fspec_p, fspec_p, fspec_q, fspec_q,
        ],
        out_specs=[pl.BlockSpec((fc, HY_CH), lambda j: (j, 0)), pl.BlockSpec((fc, HY_CH), lambda j: (j, 0))],
        out_shape=[jax.ShapeDtypeStruct((n, HY_CH), F32), jax.ShapeDtypeStruct((n, HY_CH), F32)],
        scratch_shapes=[pltpu.VMEM((n, HY_CH), BF16)] * 4 + [pltpu.VMEM((8, HY_CH), F32)],
        compiler_params=pltpu.CompilerParams(
            dimension_semantics=("arbitrary",),
            vmem_limit_bytes=_vmem_limit(est)),
        name="hyena_filters",
    )(consts["zfeat"], w1p, b1p, w2p, b2p, w3p, b3p, wop, frp, bias.reshape(1, HY_CH), consts["deltas"],
      consts["f_hi"], consts["f_lo"], consts["f_hi"], consts["f_lo"])


def _short_conv(hy_ref, cw_ref, cb_ref, c0, n):
    x = hy_ref[:, c0:c0 + LANE].astype(F32)
    row = lax.broadcasted_iota(jnp.int32, x.shape, 0)
    prv = jnp.where(row == 0, 0.0, pltpu.roll(x, 1, axis=0))
    nxt = jnp.where(row == n - 1, 0.0, pltpu.roll(x, n - 1, axis=0))
    w = cw_ref[:, c0:c0 + LANE]
    return prv * w[0:1] + x * w[1:2] + nxt * w[2:3] + cb_ref[:, c0:c0 + LANE]


def _hy_mixer_kernel(hy_ref, cw_ref, cb_ref, fp_ref, fq_ref, hp_ref, hq_ref, gp_ref, gq_ref, gn_ref,
                     o_ref, z_ref, y_ref, *, n, nf):
    j = pl.program_id(1)
    nchunk = HY_CH // LANE

    @pl.when(j == 0)
    def _():
        for c in range(nchunk):
            x1 = _short_conv(hy_ref, cw_ref, cb_ref, HY_CH + c * LANE, n)
            v = _short_conv(hy_ref, cw_ref, cb_ref, 2 * HY_CH + c * LANE, n)
            z_ref[:, c * LANE:(c + 1) * LANE] = (v * x1).astype(BF16)
        y_ref[...] = jnp.zeros_like(y_ref)

    z = z_ref[...]
    xp = _dot(fp_ref[...], z)
    xq = _dot(fq_ref[...], z)
    hp = hp_ref[...]
    hq = hq_ref[...]
    row = lax.broadcasted_iota(jnp.int32, xp.shape, 0)
    is0 = (row == 0) & (j == 0)
    qq = xq * hq
    yp = xp * hp - jnp.where(is0, 0.0, qq)
    yq = jnp.where(is0, qq, xp * hq + xq * hp)
    y_ref[...] += _dot(gp_ref[...], yp.astype(BF16)) + _dot(gq_ref[...], yq.astype(BF16))

    @pl.when(j == nf - 1)
    def _():
        ss = jnp.zeros((n, 1), F32)
        for c in range(nchunk):
            x0 = _short_conv(hy_ref, cw_ref, cb_ref, c * LANE, n)
            prod = x0 * y_ref[:, c * LANE:(c + 1) * LANE]
            y_ref[:, c * LANE:(c + 1) * LANE] = prod
            ss = ss + jnp.sum(prod * prod, axis=-1, keepdims=True)
        inv = lax.rsqrt(ss * (1.0 / HY_CH) + EPS)
        o_ref[...] = (y_ref[...] * inv * gn_ref[...]).astype(o_ref.dtype)


def _hy_mixer(h, consts, conv_w, conv_b, hp, hq, gnorm):
    b, n, _ = h.shape
    fc = consts["fc"]
    nf = n // fc
    est = (n * 3 * HY_CH * 2 + 8 * fc * n * 2 + 4 * fc * HY_CH * 4 + n * HY_CH * (2 + 4 + 4)
           + 6 * fc * HY_CH * 4 + 6 * n * LANE * 4)
    return pl.pallas_call(
        functools.partial(_hy_mixer_kernel, n=n, nf=nf),
        grid=(b, nf),
        in_specs=[
            pl.BlockSpec((None, n, 3 * HY_CH), lambda bi, j: (bi, 0, HY_OFF // (3 * HY_CH)),
                         pipeline_mode=pl.Buffered(1)),
            pl.BlockSpec((3, 3 * HY_CH), lambda bi, j: (0, 0)),
            pl.BlockSpec((1, 3 * HY_CH), lambda bi, j: (0, 0)),
            pl.BlockSpec((fc, n), lambda bi, j: (j, 0)),
            pl.BlockSpec((fc, n), lambda bi, j: (nf + j, 0)),
            pl.BlockSpec((fc, HY_CH), lambda bi, j: (j, 0)),
            pl.BlockSpec((fc, HY_CH), lambda bi, j: (j, 0)),
            pl.BlockSpec((n, fc), lambda bi, j: (0, j)),
            pl.BlockSpec((n, fc), lambda bi, j: (0, nf + j)),
            pl.BlockSpec((1, HY_CH), lambda bi, j: (0, 0)),
        ],
        out_specs=pl.BlockSpec((None, n, HY_CH), lambda bi, j: (bi, 0, 0)),
        out_shape=jax.ShapeDtypeStruct((b, n, HY_CH), BF16),
        scratch_shapes=[pltpu.VMEM((n, HY_CH), BF16), pltpu.VMEM((n, HY_CH), F32)],
        compiler_params=pltpu.CompilerParams(
            dimension_semantics=("arbitrary", "arbitrary"),
            vmem_limit_bytes=_vmem_limit(est)),
        name="hyena_mixer",
    )(h, conv_w, conv_b.reshape(1, 3 * HY_CH), consts["f_hi"], consts["f_hi"], hp, hq,
      consts["g"], consts["g"], gnorm.reshape(1, HY_CH))


def _hy_constants(n):
    big = 2 * n
    t = jnp.linspace(0.0, 1.0, n, dtype=F32)[:, None]
    w = 2.0 * math.pi * jnp.arange(n, dtype=F32)[:, None] / n
    f = jnp.linspace(1e-4, HY_BANDS - 1, HY_BANDS, dtype=F32)[None, :]
    zfeat = jnp.concatenate([t, jnp.cos(f * w), -jnp.sin(f * w)], axis=-1)
    zfeat = jnp.pad(zfeat, ((0, 0), (0, LANE - HY_EMB)))
    min_decay = math.log(HY_TARGET) / HY_SLOW_DECAY
    max_decay = math.log(HY_TARGET) / HY_FAST_DECAY
    deltas = jnp.linspace(min_decay, max_decay, HY_CH, dtype=F32)[None, :]

    k = jnp.arange(n, dtype=jnp.int32)
    theta = ((k[:, None] * k[None, :]) % big).astype(F32) * (2.0 * math.pi / big)
    cosm = jnp.cos(theta)
    sinm = jnp.sin(theta)
    alt = jnp.where(k % 2 == 0, 1.0, -1.0).astype(F32)
    fq = jnp.where((k == 0)[:, None], alt[None, :], -sinm)
    fmat = jnp.concatenate([cosm, fq], axis=0)
    f_hi = fmat.astype(BF16)
    f_lo = (fmat - f_hi.astype(F32)).astype(BF16)
    wk = jnp.where(k == 0, 1.0, 2.0).astype(F32)[None, :] / big
    gp = cosm * wk
    gq = jnp.where((k == 0)[None, :], alt[:, None] / big, -sinm * wk)
    g = jnp.concatenate([gp, gq], axis=1).astype(BF16)
    return dict(n=n, fc=min(n, 512), zfeat=zfeat, deltas=deltas, f_hi=f_hi, f_lo=f_lo, g=g)


def _out_proj_kernel(od_ref, og_ref, oh_ref, w_ref, x_ref, g_ref, m_ref, o_ref):
    mix = _dot(od_ref[...], w_ref[0:DIFF_W, :])
    mix = mix + _dot(og_ref[...], w_ref[DIFF_W:DIFF_W + GQA_W, :])
    mix = mix + _dot(oh_ref[...], w_ref[DIFF_W + GQA_W:, :])
    o_ref[...] = x_ref[...] + m_ref[2:3, :] * (_rms(mix) * g_ref[...])


def _out_proj(o_d, o_g, o_h, w, x, g, mods, per_batch):
    b, s, d = x.shape
    tm = min(s, 512)
    est = d * d * 2 + 2 * tm * d * 2 + 4 * tm * d * 4 + 2 * tm * d * 4
    return pl.pallas_call(
        _out_proj_kernel,
        grid=(b, s // tm),
        in_specs=[
            pl.BlockSpec((None, tm, DIFF_W), lambda bi, i: (bi, i, 0)),
            pl.BlockSpec((None, tm, GQA_W), lambda bi, i: (bi, i, 0)),
            pl.BlockSpec((None, tm, HY_CH), lambda bi, i: (bi, i, 0)),
            pl.BlockSpec((d, d), lambda bi, i: (0, 0), pipeline_mode=pl.Buffered(1)),
            pl.BlockSpec((None, tm, d), lambda bi, i: (bi, i, 0)),
            pl.BlockSpec((1, d), lambda bi, i: (0, 0)),
            pl.BlockSpec((None, N_MOD, d), lambda bi, i: (bi * per_batch, 0, 0)),
        ],
        out_specs=pl.BlockSpec((None, tm, d), lambda bi, i: (bi, i, 0)),
        out_shape=jax.ShapeDtypeStruct((b, s, d), F32),
        compiler_params=pltpu.CompilerParams(
            dimension_semantics=("parallel", "parallel"),
            vmem_limit_bytes=_vmem_limit(est)),
        name="out_proj",
    )(o_d, o_g, o_h, w, x, g.reshape(1, d), mods)


def _mlp_kernel(x_ref, g2_ref, g3_ref, m_ref, wu_ref, wd_ref, o_ref, xn_ref, *, nf, ncol):
    f = pl.program_id(2)

    @pl.when(f == 0)
    def _():
        y = _rms(x_ref[...]) * g2_ref[...]
        xn_ref[...] = (y * (1.0 + m_ref[4:5, :]) + m_ref[3:4, :]).astype(BF16)
        o_ref[...] = jnp.zeros_like(o_ref)

    hmid = jnp.square(jnp.maximum(_dot(xn_ref[...], wu_ref[...]), 0.0)).astype(BF16)
    d = o_ref.shape[-1]
    for c in range(d // ncol):
        o_ref[:, c * ncol:(c + 1) * ncol] += _dot(hmid, wd_ref[:, c * ncol:(c + 1) * ncol])

    @pl.when(f == nf - 1)
    def _():
        o_ref[...] = x_ref[...] + m_ref[5:6, :] * (_rms(o_ref[...]) * g3_ref[...])


def _mlp(x, g2, g3, mods, w_up, w_down, per_batch):
    b, s, d = x.shape
    dff = w_up.shape[1]
    tm = min(s, 1024)
    tf = 512
    nf = dff // tf
    est = tm * d * 4 + 2 * tm * d * 4 + tm * d * 2 + 4 * d * tf * 2 + tm * tf * 6 + tm * 512 * 4
    return pl.pallas_call(
        functools.partial(_mlp_kernel, nf=nf, ncol=512),
        grid=(b, s // tm, nf),
        in_specs=[
            pl.BlockSpec((None, tm, d), lambda bi, i, f: (bi, i, 0), pipeline_mode=pl.Buffered(1)),
            pl.BlockSpec((1, d), lambda bi, i, f: (0, 0)),
            pl.BlockSpec((1, d), lambda bi, i, f: (0, 0)),
            pl.BlockSpec((None, N_MOD, d), lambda bi, i, f: (bi * per_batch, 0, 0)),
            pl.BlockSpec((d, tf), lambda bi, i, f: (0, f)),
            pl.BlockSpec((tf, d), lambda bi, i, f: (f, 0)),
        ],
        out_specs=pl.BlockSpec((None, tm, d), lambda bi, i, f: (bi, i, 0)),
        out_shape=jax.ShapeDtypeStruct((b, s, d), F32),
        scratch_shapes=[pltpu.VMEM((tm, d), BF16)],
        compiler_params=pltpu.CompilerParams(
            dimension_semantics=("parallel", "parallel", "arbitrary"),
            vmem_limit_bytes=_vmem_limit(est)),
        name="mlp",
    )(x, g2.reshape(1, d), g3.reshape(1, d), mods, w_up, w_down)


def _rope_tables(n_rows, head_dim):
    t_row = jnp.repeat(jnp.arange(n_rows, dtype=F32), GRID_W)
    t_col = jnp.tile(jnp.arange(GRID_W, dtype=F32), n_rows)
    d_axis = head_dim // 2
    inv = ROPE_THETA ** (-jnp.arange(0, d_axis, 2, dtype=F32) / d_axis)
    ang = jnp.concatenate([t_row[:, None] * inv, t_col[:, None] * inv], axis=-1)
    cos = jnp.repeat(jnp.cos(ang), 2, axis=-1)
    sin = jnp.repeat(jnp.sin(ang), 2, axis=-1)
    sign = jnp.where(jnp.arange(head_dim) % 2 == 0, -1.0, 1.0).astype(F32)
    sin = sin * sign
    reps = LANE // head_dim
    return jnp.tile(cos, (1, reps)), jnp.tile(sin, (1, reps))


def kernel(x, c, ctx, c_ctx, w_mod, b_mod, g_norm, w_in, w_out, diff_lam, diff_subln, gqa_q_norm,
           gqa_k_norm, gqa_out_norm, hy_conv_w, hy_conv_b, hy_w1, hy_b1, hy_w2, hy_b2, hy_w3, hy_b3,
           hy_wout, hy_freq, hy_bias, hy_out_norm, w_up, w_down):
    bsz, n_lat, d = x.shape
    n_ctx = ctx.shape[1]
    depth = w_mod.shape[0]
    assert d == D_MODEL and bsz + 1 <= MOD_ROWS
    assert n_lat % GRID_W == 0 and n_lat % 256 == 0 and n_ctx % 128 == 0

    tab_d = _rope_tables(n_lat // GRID_W, DIFF_QK)
    tab_g = _rope_tables(n_lat // GRID_W, GQA_HD)
    hyc_lat = _hy_constants(n_lat)
    hyc_ctx = _hy_constants(n_ctx)

    cpad = jnp.zeros((MOD_ROWS, d), F32).at[:bsz].set(c).at[bsz].set(c_ctx)
    mods = _modulations(cpad, w_mod, b_mod)
    lam_pad = jnp.pad(diff_lam, ((0, 0), (0, 4), (0, LANE - DIFF_QK)))

    tq_lat = min(n_lat, 512)
    tq_ctx = min(n_ctx, 256)
    xc = ctx
    for l in range(depth):
        last = l == depth - 1
        lam_init = 0.8 - 0.6 * math.exp(-0.3 * l)
        m_lat = mods[l, :bsz].reshape(bsz, N_MOD, d)
        m_ctx = mods[l, bsz:bsz + 1].reshape(1, N_MOD, d)
        w_in_l = w_in[l].astype(BF16)
        w_out_l = w_out[l].astype(BF16)
        w_up_l = w_up[l].astype(BF16)
        w_down_l = w_down[l].astype(BF16)
        filt = (hy_w1[l], hy_b1[l], hy_w2[l], hy_b2[l], hy_w3[l], hy_b3[l], hy_wout[l], hy_freq[l], hy_bias[l])

        h = _in_proj(x, g_norm[l, 0], m_lat, w_in_l, 1)
        xc_flat = xc.reshape(1, bsz * n_ctx, d)
        hc = _in_proj(xc_flat, g_norm[l, 0], m_ctx, w_in_l[:, KV_OFF:] if last else w_in_l, 0)
        hc = hc.reshape(bsz, n_ctx, -1)
        c_blk = 0 if last else KV_BLK

        o_d = _diff_attention(lam_pad[l], h, h, KV_BLK, hc, c_blk, tab_d, diff_subln[l], lam_init, tq_lat)
        o_g = _gqa_attention(h, h, KV_BLK, hc, c_blk, tab_g, gqa_q_norm[l], gqa_k_norm[l], gqa_out_norm[l],
                             tq_lat, 2)
        hp, hq = _hy_filters(hyc_lat, *filt)
        o_h = _hy_mixer(h, hyc_lat, hy_conv_w[l], hy_conv_b[l], hp, hq, hy_out_norm[l])
        x_new = _out_proj(o_d, o_g, o_h, w_out_l, x, g_norm[l, 1], m_lat, 1)

        if not last:
            oc_d = _diff_attention(lam_pad[l], hc, hc, KV_BLK, None, 0, None, diff_subln[l], lam_init, tq_ctx)
            oc_g = _gqa_attention(hc, hc, KV_BLK, None, 0, None, gqa_q_norm[l], gqa_k_norm[l],
                                  gqa_out_norm[l], tq_ctx, 1)
            hpc, hqc = _hy_filters(hyc_ctx, *filt)
            oc_h = _hy_mixer(hc, hyc_ctx, hy_conv_w[l], hy_conv_b[l], hpc, hqc, hy_out_norm[l])
            flat = lambda a: a.reshape(1, bsz * n_ctx, a.shape[-1])
            xc_flat = _out_proj(flat(oc_d), flat(oc_g), flat(oc_h), w_out_l, xc_flat, g_norm[l, 1], m_ctx, 0)
        x = x_new

        x = _mlp(x, g_norm[l, 2], g_norm[l, 3], m_lat, w_up_l, w_down_l, 1)
        if not last:
            xc_flat = _mlp(xc_flat, g_norm[l, 2], g_norm[l, 3], m_ctx, w_up_l, w_down_l, 0)
            xc = xc_flat.reshape(bsz, n_ctx, d)
    return x
```

```python
import functools
import math

import jax
import jax.numpy as jnp
from jax import lax
from jax.experimental import pallas as pl
from jax.experimental.pallas import tpu as pltpu

F32 = jnp.float32
BF16 = jnp.bfloat16

D_MODEL = 2048
GRID_W = 64
DIFF_W = 512
GQA_W = 1024
HY_CH = 512
DIFF_V = 128
DIFF_QK = 64
DIFF_HEADS = 4
GQA_HD = 128
GQA_KV_HEADS = 2
GQA_GROUP = 4
HY_EMB = 33
HY_BANDS = 16
HY_FFN = 64
HY_FAST_DECAY = 0.3
HY_SLOW_DECAY = 1.5
HY_TARGET = 1e-2
HY_SHIFT = 0.0
D_FF = 4 * D_MODEL
N_MOD = 6
ROPE_THETA = 10000.0
EPS = 1e-6

LANE = 128
DQ_BLK = 0
GQ_BLK = 4
HY_OFF = 1536
KV_OFF = 3072
KV_BLK = KV_OFF // LANE
DK_REL, DV_REL, GK_REL, GV_REL = 0, 4, 8, 10
N_IN = 4608
KV_W = N_IN - KV_OFF

VMEM_CAP = 64 * 1024 * 1024
MOD_ROWS = 24
PREP_ROWS = 256
LOG2E = 1.4426950408889634
ROW_CHUNK = 256
ONES_ROWS = 16


def _vmem_limit(nbytes):
    return int(min(max(nbytes * 5 // 4 + (4 << 20), 32 << 20), VMEM_CAP - (6 << 20)))


def _rms(x):
    return x * lax.rsqrt(jnp.mean(x * x, axis=-1, keepdims=True) + EPS)


def _dot(a, b):
    return jnp.dot(a, b, preferred_element_type=F32)


def _dot_nt(a, b):
    return lax.dot_general(a, b, (((1,), (1,)), ((), ())), preferred_element_type=F32)


def _split_bf16(a):
    hi = a.astype(BF16)
    lo = (a - hi.astype(F32)).astype(BF16)
    return hi, lo


def _dot3(a, b):
    ah, al = _split_bf16(a)
    bh, bl = _split_bf16(b)
    return _dot(ah, bh) + (_dot(al, bh) + _dot(ah, bl))


def _rope(x, cos, sin_signed):
    lane = lax.broadcasted_iota(jnp.int32, x.shape, 1)
    nxt = pltpu.roll(x, LANE - 1, axis=1)
    prv = pltpu.roll(x, 1, axis=1)
    swapped = jnp.where(lane % 2 == 0, nxt, prv)
    return x * cos + swapped * sin_signed


def _mod_kernel(c_ref, w_ref, b_ref, o_ref):
    c = c_ref[...]
    s = c * (1.0 / (1.0 + jnp.exp(-c)))
    o_ref[0] = _dot(s.astype(BF16), w_ref[0].astype(BF16)) + b_ref[0]


def _modulations(cpad, w_mod, b_mod):
    nl, d, n = w_mod.shape
    tn = 1024
    return pl.pallas_call(
        _mod_kernel,
        grid=(nl, n // tn),
        in_specs=[
            pl.BlockSpec((MOD_ROWS, d), lambda l, j: (0, 0)),
            pl.BlockSpec((1, d, tn), lambda l, j: (l, 0, j)),
            pl.BlockSpec((1, 1, tn), lambda l, j: (l, 0, j)),
        ],
        out_specs=pl.BlockSpec((1, MOD_ROWS, tn), lambda l, j: (l, 0, j)),
        out_shape=jax.ShapeDtypeStruct((nl, MOD_ROWS, n), F32),
        compiler_params=pltpu.CompilerParams(
            dimension_semantics=("arbitrary", "arbitrary"),
            vmem_limit_bytes=_vmem_limit(2 * d * tn * 4 + d * tn * 2)),
        name="modulations",
    )(cpad, w_mod, b_mod.reshape(nl, 1, n))


def _in_proj_kernel(x_ref, g_ref, m_ref, w_ref, o_ref, xn_ref):
    j = pl.program_id(2)

    @pl.when(j == 0)
    def _():
        for r in range(0, o_ref.shape[0], ROW_CHUNK):
            rows = slice(r, r + ROW_CHUNK)
            y = _rms(x_ref[rows, :]) * g_ref[...]
            xn = (y * (1.0 + m_ref[1:2, :]) + m_ref[0:1, :]).astype(BF16)
            xn_ref[rows, :] = xn
            o_ref[rows, :] = _dot(xn, w_ref[...]).astype(o_ref.dtype)

    @pl.when(j > 0)
    def _():
        o_ref[...] = _dot(xn_ref[...], w_ref[...]).astype(o_ref.dtype)


def _in_proj(x, g, mods, w, per_batch):
    b, s, d = x.shape
    n = w.shape[1]
    tm = min(s, 1024)
    tn = 512
    est = 2 * tm * d * 4 + tm * d * 2 + 2 * d * tn * 2 + 2 * tm * tn * 2 + tm * tn * 4
    return pl.pallas_call(
        _in_proj_kernel,
        grid=(b, s // tm, n // tn),
        in_specs=[
            pl.BlockSpec((None, tm, d), lambda bi, i, j: (bi, i, 0)),
            pl.BlockSpec((1, d), lambda bi, i, j: (0, 0)),
            pl.BlockSpec((None, N_MOD, d), lambda bi, i, j: (bi * per_batch, 0, 0)),
            pl.BlockSpec((d, tn), lambda bi, i, j: (0, j)),
        ],
        out_specs=pl.BlockSpec((None, tm, tn), lambda bi, i, j: (bi, i, j)),
        out_shape=jax.ShapeDtypeStruct((b, s, n), BF16),
        scratch_shapes=[pltpu.VMEM((tm, d), BF16)],
        compiler_params=pltpu.CompilerParams(
            dimension_semantics=("parallel", "parallel", "arbitrary"),
            vmem_limit_bytes=_vmem_limit(est)),
        name="in_proj",
    )(x, g.reshape(1, d), mods, w)


def _gqa_kernel(*refs, rope, prefix, heads_per_step):
    it = iter(refs)
    q_ref, k_ref, v_ref = next(it), next(it), next(it)
    kc_ref = vc_ref = None
    if prefix:
        kc_ref, vc_ref = next(it), next(it)
    cq_ref = sq_ref = ck_ref = sk_ref = None
    if rope:
        cq_ref, sq_ref, ck_ref, sk_ref = next(it), next(it), next(it), next(it)
    gq_ref, gk_ref, go_ref = next(it), next(it), next(it)
    o_ref = next(it)
    kall_ref, vt_ref = next(it), next(it)

    c = kc_ref.shape[0] if prefix else 0
    s = k_ref.shape[0]

    @pl.when((pl.program_id(2) == 0) & (pl.program_id(3) == 0))
    def _():
        if prefix:
            kall_ref[0:c, :] = (_rms(kc_ref[...].astype(F32)) * gk_ref[...]).astype(BF16)
            vt_ref[:, 0:c] = vc_ref[...].astype(F32).T.astype(BF16)
        pr = min(s, PREP_ROWS)
        for r0 in range(0, s, pr):
            rows = slice(r0, r0 + pr)
            k = _rms(k_ref[rows, :].astype(F32)) * gk_ref[...]
            if rope:
                k = _rope(k, ck_ref[rows, :], sk_ref[rows, :])
            kall_ref[c + r0:c + r0 + pr, :] = k.astype(BF16)
            vt_ref[:, c + r0:c + r0 + pr] = v_ref[rows, :].astype(F32).T.astype(BF16)

    for hh in range(heads_per_step):
        cols = slice(hh * LANE, (hh + 1) * LANE)
        q = _rms(q_ref[:, cols].astype(F32)) * gq_ref[...]
        if rope:
            q = _rope(q, cq_ref[...], sq_ref[...])
        qt = (q * (GQA_HD ** -0.5 * LOG2E)).T.astype(BF16)
        st = _dot(kall_ref[...], qt)
        m = jnp.max(st, axis=0, keepdims=True)
        p = jnp.exp2(st - m)
        denom = jnp.sum(p, axis=0, keepdims=True)
        ot = _dot(vt_ref[...], p.astype(BF16)) / denom
        ot = ot * lax.rsqrt(jnp.mean(ot * ot, axis=0, keepdims=True) + EPS) * go_ref[...]
        o_ref[:, cols] = ot.T.astype(o_ref.dtype)


def _gqa_attention(hq, hk, kv_blk, hc, c_blk, tables, gq, gk, go, tq, heads_per_step):
    b, s, _ = hq.shape
    prefix = hc is not None
    rope = tables is not None
    hps = heads_per_step
    grid = (b, GQA_KV_HEADS, GQA_GROUP // hps, s // tq)
    in_specs = [
        pl.BlockSpec((None, tq, hps * LANE), lambda bi, g, r, i: (bi, i, (GQ_BLK + g * GQA_GROUP) // hps + r)),
        pl.BlockSpec((None, s, LANE), lambda bi, g, r, i: (bi, 0, kv_blk + GK_REL + g)),
        pl.BlockSpec((None, s, LANE), lambda bi, g, r, i: (bi, 0, kv_blk + GV_REL + g)),
    ]
    args = [hq, hk, hk]
    c = 0
    if prefix:
        c = hc.shape[1]
        in_specs += [
            pl.BlockSpec((None, c, LANE), lambda bi, g, r, i: (bi, 0, c_blk + GK_REL + g)),
            pl.BlockSpec((None, c, LANE), lambda bi, g, r, i: (bi, 0, c_blk + GV_REL + g)),
        ]
        args += [hc, hc]
    if rope:
        cos, sin = tables
        in_specs += [
            pl.BlockSpec((tq, LANE), lambda bi, g, r, i: (i, 0)),
            pl.BlockSpec((tq, LANE), lambda bi, g, r, i: (i, 0)),
            pl.BlockSpec((s, LANE), lambda bi, g, r, i: (0, 0)),
            pl.BlockSpec((s, LANE), lambda bi, g, r, i: (0, 0)),
        ]
        args += [cos, sin, cos, sin]
    vec = pl.BlockSpec((1, LANE), lambda bi, g, r, i: (0, 0))
    in_specs += [vec, vec, pl.BlockSpec((LANE, 1), lambda bi, g, r, i: (0, 0))]
    args += [gq.reshape(1, LANE), gk.reshape(1, LANE), go.reshape(LANE, 1)]
    lk = s + c
    est = 4 * s * LANE * 2 + 4 * s * LANE * 4 + 2 * lk * LANE * 2 + hps * 3 * tq * lk * 4
    return pl.pallas_call(
        functools.partial(_gqa_kernel, rope=rope, prefix=prefix, heads_per_step=hps),
        grid=grid,
        in_specs=in_specs,
        out_specs=pl.BlockSpec((None, tq, hps * LANE), lambda bi, g, r, i: (bi, i, g * GQA_GROUP // hps + r)),
        out_shape=jax.ShapeDtypeStruct((b, s, GQA_W), BF16),
        scratch_shapes=[pltpu.VMEM((lk, LANE), BF16), pltpu.VMEM((LANE, lk), BF16)],
        compiler_params=pltpu.CompilerParams(
            dimension_semantics=("arbitrary",) * 4,
            vmem_limit_bytes=_vmem_limit(est)),
        name="gqa_attention",
    )(*args)


def _diff_kernel(*refs, rope, prefix, lam_init):
    it = iter(refs)
    lp_ref, q_ref, k_ref, v_ref = next(it), next(it), next(it), next(it)
    kc_ref = vc_ref = None
    if prefix:
        kc_ref, vc_ref = next(it), next(it)
    cq_ref = sq_ref = ck_ref = sk_ref = None
    if rope:
        cq_ref, sq_ref, ck_ref, sk_ref = next(it), next(it), next(it), next(it)
    gs_ref = next(it)
    o_ref = next(it)
    kall_ref, vt_ref, lam_ref = next(it), next(it), next(it)
    c = kc_ref.shape[0] if prefix else 0
    s = k_ref.shape[0]

    first = (pl.program_id(0) == 0) & (pl.program_id(1) == 0) & (pl.program_id(2) == 0)

    @pl.when(first)
    def _():
        p = lp_ref[...]
        t1 = jnp.sum(p[0:1] * p[1:2], axis=-1, keepdims=True)
        t2 = jnp.sum(p[2:3] * p[3:4], axis=-1, keepdims=True)
        lam_ref[...] = jnp.broadcast_to(jnp.exp(t1) - jnp.exp(t2) + lam_init, lam_ref.shape)

    @pl.when(pl.program_id(2) == 0)
    def _():
        vt_ref[LANE:, :] = jnp.ones((ONES_ROWS, c + s), BF16)
        if prefix:
            kall_ref[0:c, :] = kc_ref[...]
            vt_ref[0:LANE, 0:c] = vc_ref[...].astype(F32).T.astype(BF16)
        pr = min(s, PREP_ROWS)
        for r0 in range(0, s, pr):
            rows = slice(r0, r0 + pr)
            if rope:
                k = _rope(k_ref[rows, :].astype(F32), ck_ref[rows, :], sk_ref[rows, :]).astype(BF16)
            else:
                k = k_ref[rows, :]
            kall_ref[c + r0:c + r0 + pr, :] = k
            vt_ref[0:LANE, c + r0:c + r0 + pr] = v_ref[rows, :].astype(F32).T.astype(BF16)

    lam = lam_ref[0:1, 0:1]
    q = q_ref[...].astype(F32)
    if rope:
        q = _rope(q, cq_ref[...], sq_ref[...])
    qt = (q * (DIFF_QK ** -0.5 * LOG2E)).T
    row = lax.broadcasted_iota(jnp.int32, qt.shape, 0)
    q1 = jnp.where(row < DIFF_QK, qt, 0.0).astype(BF16)
    q2 = jnp.where(row >= DIFF_QK, qt, 0.0).astype(BF16)

    kall = kall_ref[...]
    s1 = _dot(kall, q1)
    s2 = _dot(kall, q2)
    e1 = jnp.exp2(s1 - jnp.max(s1, axis=0, keepdims=True)).astype(BF16)
    e2 = jnp.exp2(s2 - jnp.max(s2, axis=0, keepdims=True)).astype(BF16)
    vt = vt_ref[...]
    o1 = _dot(vt, e1)
    o2 = _dot(vt, e2)
    ot = o1[0:LANE] / o1[LANE:LANE + 1] - o2[0:LANE] * (lam / o2[LANE:LANE + 1])
    ot = ot * lax.rsqrt(jnp.mean(ot * ot, axis=0, keepdims=True) + EPS) * (gs_ref[...] * (1.0 - lam_init))
    o_ref[...] = ot.T.astype(o_ref.dtype)


def _diff_attention(lam_params, hq, hk, kv_blk, hc, c_blk, tables, gsub, lam_init, tq):
    b, s, _ = hq.shape
    prefix = hc is not None
    rope = tables is not None
    grid = (b, DIFF_HEADS, s // tq)
    in_specs = [
        pl.BlockSpec((8, LANE), lambda bi, h, i: (0, 0)),
        pl.BlockSpec((None, tq, LANE), lambda bi, h, i: (bi, i, DQ_BLK + h)),
        pl.BlockSpec((None, s, LANE), lambda bi, h, i: (bi, 0, kv_blk + DK_REL + h)),
        pl.BlockSpec((None, s, LANE), lambda bi, h, i: (bi, 0, kv_blk + DV_REL + h)),
    ]
    args = [lam_params, hq, hk, hk]
    c = 0
    if prefix:
        c = hc.shape[1]
        in_specs += [
            pl.BlockSpec((None, c, LANE), lambda bi, h, i: (bi, 0, c_blk + DK_REL + h)),
            pl.BlockSpec((None, c, LANE), lambda bi, h, i: (bi, 0, c_blk + DV_REL + h)),
        ]
        args += [hc, hc]
    if rope:
        cos, sin = tables
        in_specs += [
            pl.BlockSpec((tq, LANE), lambda bi, h, i: (i, 0)),
            pl.BlockSpec((tq, LANE), lambda bi, h, i: (i, 0)),
            pl.BlockSpec((s, LANE), lambda bi, h, i: (0, 0)),
            pl.BlockSpec((s, LANE), lambda bi, h, i: (0, 0)),
        ]
        args += [cos, sin, cos, sin]
    in_specs.append(pl.BlockSpec((LANE, 1), lambda bi, h, i: (0, 0)))
    args.append(gsub.reshape(LANE, 1))
    lk = s + c
    est = 4 * s * LANE * 2 + 4 * s * LANE * 4 + 2 * lk * LANE * 2 + 5 * tq * lk * 4
    return pl.pallas_call(
        functools.partial(_diff_kernel, rope=rope, prefix=prefix, lam_init=lam_init),
        grid=grid,
        in_specs=in_specs,
        out_specs=pl.BlockSpec((None, tq, LANE), lambda bi, h, i: (bi, i, h)),
        out_shape=jax.ShapeDtypeStruct((b, s, DIFF_W), BF16),
        scratch_shapes=[pltpu.VMEM((lk, LANE), BF16), pltpu.VMEM((LANE + ONES_ROWS, lk), BF16), pltpu.VMEM((8, LANE), F32)],
        compiler_params=pltpu.CompilerParams(
            dimension_semantics=("arbitrary",) * 3,
            vmem_limit_bytes=_vmem_limit(est)),
        name="diff_attention",
    )(*args)


def _hy_filter_kernel(z_ref, w1_ref, b1_ref, w2_ref, b2_ref, w3_ref, b3_ref, wo_ref, fr_ref,
                      bias_ref, dl_ref, fp_hi_ref, fp_lo_ref, fq_hi_ref, fq_lo_ref,
                      hp_ref, hq_ref, hs_hi, hs_lo, hd_hi, hd_lo, alt_ref):
    j = pl.program_id(0)

    @pl.when(j == 0)
    def _():
        z = z_ref[...]
        h = jnp.sin(fr_ref[0:1, :] * (_dot3(z, w1_ref[...]) + b1_ref[...]))
        h = jnp.sin(fr_ref[1:2, :] * (_dot3(h, w2_ref[...]) + b2_ref[...]))
        h = jnp.sin(fr_ref[2:3, :] * (_dot3(h, w3_ref[...]) + b3_ref[...]))
        h = _dot3(h, wo_ref[...])
        decay = jnp.exp(-z[:, 0:1] * jnp.abs(dl_ref[...])) + HY_SHIFT
        hf = h[:, :HY_CH] * decay
        hb = h[:, HY_CH:] * decay
        row = lax.broadcasted_iota(jnp.int32, hf.shape, 0)
        hf = hf + jnp.where(row == 0, bias_ref[...], 0.0)
        hb = jnp.where(row == 0, 0.0, hb)
        hs = hf + hb
        hd = hf - hb
        sign = jnp.where(row % 2 == 0, 1.0, -1.0)
        alt_ref[...] = jnp.broadcast_to(jnp.sum(sign * hs, axis=0, keepdims=True), alt_ref.shape)
        a, bb = _split_bf16(hs)
        hs_hi[...] = a
        hs_lo[...] = bb
        a, bb = _split_bf16(hd)
        hd_hi[...] = a
        hd_lo[...] = bb

    xp = _dot(fp_hi_ref[...], hs_hi[...]) + (_dot(fp_lo_ref[...], hs_hi[...]) + _dot(fp_hi_ref[...], hs_lo[...]))
    xq = _dot(fq_hi_ref[...], hd_hi[...]) + (_dot(fq_lo_ref[...], hd_hi[...]) + _dot(fq_hi_ref[...], hd_lo[...]))
    row = lax.broadcasted_iota(jnp.int32, xq.shape, 0)
    is0 = (row == 0) & (j == 0)
    hp_ref[...] = xp
    hq_ref[...] = jnp.where(is0, alt_ref[0:1, :], xq)


def _hy_filters(consts, w1, b1, w2, b2, w3, b3, wout, freq, bias):
    n = consts["n"]
    fc = consts["fc"]
    nf = n // fc
    pad = LANE - HY_FFN
    w1p = jnp.pad(w1, ((0, LANE - HY_EMB), (0, pad)))
    w2p = jnp.pad(w2, ((0, pad), (0, pad)))
    w3p = jnp.pad(w3, ((0, pad), (0, pad)))
    wop = jnp.pad(wout, ((0, pad), (0, 0)))
    b1p = jnp.pad(b1, (0, pad)).reshape(1, LANE)
    b2p = jnp.pad(b2, (0, pad)).reshape(1, LANE)
    b3p = jnp.pad(b3, (0, pad)).reshape(1, LANE)
    frp = jnp.pad(freq, ((0, 0), (0, pad)))
    full = lambda shape: pl.BlockSpec(shape, lambda j: (0,) * len(shape))
    fspec_p = pl.BlockSpec((fc, n), lambda j: (j, 0))
    fspec_q = pl.BlockSpec((fc, n), lambda j: (nf + j, 0))
    est = n * LANE * 4 * 2 + 8 * fc * n * 2 + 4 * n * HY_CH * 2 + 4 * fc * HY_CH * 4 + 8 * n * HY_CH * 4
    return pl.pallas_call(
        _hy_filter_kernel,
        grid=(nf,),
        in_specs=[
            full((n, LANE)), full((LANE, LANE)), full((1, LANE)), full((LANE, LANE)), full((1, LANE)),
            full((LANE, LANE)), full((1, LANE)), full((LANE, 2 * HY_CH)), full((3, LANE)),
            full((1, HY_CH)), full((1, HY_CH)),
            fspec_p, fspec_p, fspec_q, fspec_q,
        ],
        out_specs=[pl.BlockSpec((fc, HY_CH), lambda j: (j, 0)), pl.BlockSpec((fc, HY_CH), lambda j: (j, 0))],
        out_shape=[jax.ShapeDtypeStruct((n, HY_CH), F32), jax.ShapeDtypeStruct((n, HY_CH), F32)],
        scratch_shapes=[pltpu.VMEM((n, HY_CH), BF16)] * 4 + [pltpu.VMEM((8, HY_CH), F32)],
        compiler_params=pltpu.CompilerParams(
            dimension_semantics=("arbitrary",),
            vmem_limit_bytes=_vmem_limit(est)),
        name="hyena_filters",
    )(consts["zfeat"], w1p, b1p, w2p, b2p, w3p, b3p, wop, frp, bias.reshape(1, HY_CH), consts["deltas"],
      consts["f_hi"], consts["f_lo"], consts["f_hi"], consts["f_lo"])


def _short_conv(hy_ref, cw_ref, cb_ref, c0, n):
    x = hy_ref[:, c0:c0 + LANE].astype(F32)
    row = lax.broadcasted_iota(jnp.int32, x.shape, 0)
    prv = jnp.where(row == 0, 0.0, pltpu.roll(x, 1, axis=0))
    nxt = jnp.where(row == n - 1, 0.0, pltpu.roll(x, n - 1, axis=0))
    w = cw_ref[:, c0:c0 + LANE]
    return prv * w[0:1] + x * w[1:2] + nxt * w[2:3] + cb_ref[:, c0:c0 + LANE]


def _hy_mixer_kernel(hy_ref, cw_ref, cb_ref, fp_ref, fq_ref, hp_ref, hq_ref, gp_ref, gq_ref, gn_ref,
                     o_ref, z_ref, y_ref, *, n, nf):
    j = pl.program_id(1)
    nchunk = HY_CH // LANE

    @pl.when(j == 0)
    def _():
        for c in range(nchunk):
            x1 = _short_conv(hy_ref, cw_ref, cb_ref, HY_CH + c * LANE, n)
            v = _short_conv(hy_ref, cw_ref, cb_ref, 2 * HY_CH + c * LANE, n)
            z_ref[:, c * LANE:(c + 1) * LANE] = (v * x1).astype(BF16)
        y_ref[...] = jnp.zeros_like(y_ref)

    z = z_ref[...]
    xp = _dot(fp_ref[...], z)
    xq = _dot(fq_ref[...], z)
    hp = hp_ref[...]
    hq = hq_ref[...]
    row = lax.broadcasted_iota(jnp.int32, xp.shape, 0)
    is0 = (row == 0) & (j == 0)
    qq = xq * hq
    yp = xp * hp - jnp.where(is0, 0.0, qq)
    yq = jnp.where(is0, qq, xp * hq + xq * hp)
    y_ref[...] += _dot(gp_ref[...], yp.astype(BF16)) + _dot(gq_ref[...], yq.astype(BF16))

    @pl.when(j == nf - 1)
    def _():
        ss = jnp.zeros((n, 1), F32)
        for c in range(nchunk):
            x0 = _short_conv(hy_ref, cw_ref, cb_ref, c * LANE, n)
            prod = x0 * y_ref[:, c * LANE:(c + 1) * LANE]
            y_ref[:, c * LANE:(c + 1) * LANE] = prod
            ss = ss + jnp.sum(prod * prod, axis=-1, keepdims=True)
        inv = lax.rsqrt(ss * (1.0 / HY_CH) + EPS)
        o_ref[...] = (y_ref[...] * inv * gn_ref[...]).astype(o_ref.dtype)


def _hy_mixer(h, consts, conv_w, conv_b, hp, hq, gnorm):
    b, n, _ = h.shape
    fc = consts["fc"]
    nf = n // fc
    est = (n * 3 * HY_CH * 2 + 8 * fc * n * 2 + 4 * fc * HY_CH * 4 + n * HY_CH * (2 + 4 + 4)
           + 6 * fc * HY_CH * 4 + 6 * n * LANE * 4)
    return pl.pallas_call(
        functools.partial(_hy_mixer_kernel, n=n, nf=nf),
        grid=(b, nf),
        in_specs=[
            pl.BlockSpec((None, n, 3 * HY_CH), lambda bi, j: (bi, 0, HY_OFF // (3 * HY_CH)),
                         pipeline_mode=pl.Buffered(1)),
            pl.BlockSpec((3, 3 * HY_CH), lambda bi, j: (0, 0)),
            pl.BlockSpec((1, 3 * HY_CH), lambda bi, j: (0, 0)),
            pl.BlockSpec((fc, n), lambda bi, j: (j, 0)),
            pl.BlockSpec((fc, n), lambda bi, j: (nf + j, 0)),
            pl.BlockSpec((fc, HY_CH), lambda bi, j: (j, 0)),
            pl.BlockSpec((fc, HY_CH), lambda bi, j: (j, 0)),
            pl.BlockSpec((n, fc), lambda bi, j: (0, j)),
            pl.BlockSpec((n, fc), lambda bi, j: (0, nf + j)),
            pl.BlockSpec((1, HY_CH), lambda bi, j: (0, 0)),
        ],
        out_specs=pl.BlockSpec((None, n, HY_CH), lambda bi, j: (bi, 0, 0)),
        out_shape=jax.ShapeDtypeStruct((b, n, HY_CH), BF16),
        scratch_shapes=[pltpu.VMEM((n, HY_CH), BF16), pltpu.VMEM((n, HY_CH), F32)],
        compiler_params=pltpu.CompilerParams(
            dimension_semantics=("arbitrary", "arbitrary"),
            vmem_limit_bytes=_vmem_limit(est)),
        name="hyena_mixer",
    )(h, conv_w, conv_b.reshape(1, 3 * HY_CH), consts["f_hi"], consts["f_hi"], hp, hq,
      consts["g"], consts["g"], gnorm.reshape(1, HY_CH))


def _hy_constants(n):
    big = 2 * n
    t = jnp.linspace(0.0, 1.0, n, dtype=F32)[:, None]
    w = 2.0 * math.pi * jnp.arange(n, dtype=F32)[:, None] / n
    f = jnp.linspace(1e-4, HY_BANDS - 1, HY_BANDS, dtype=F32)[None, :]
    zfeat = jnp.concatenate([t, jnp.cos(f * w), -jnp.sin(f * w)], axis=-1)
    zfeat = jnp.pad(zfeat, ((0, 0), (0, LANE - HY_EMB)))
    min_decay = math.log(HY_TARGET) / HY_SLOW_DECAY
    max_decay = math.log(HY_TARGET) / HY_FAST_DECAY
    deltas = jnp.linspace(min_decay, max_decay, HY_CH, dtype=F32)[None, :]

    k = jnp.arange(n, dtype=jnp.int32)
    theta = ((k[:, None] * k[None, :]) % big).astype(F32) * (2.0 * math.pi / big)
    cosm = jnp.cos(theta)
    sinm = jnp.sin(theta)
    alt = jnp.where(k % 2 == 0, 1.0, -1.0).astype(F32)
    fq = jnp.where((k == 0)[:, None], alt[None, :], -sinm)
    fmat = jnp.concatenate([cosm, fq], axis=0)
    f_hi = fmat.astype(BF16)
    f_lo = (fmat - f_hi.astype(F32)).astype(BF16)
    wk = jnp.where(k == 0, 1.0, 2.0).astype(F32)[None, :] / big
    gp = cosm * wk
    gq = jnp.where((k == 0)[None, :], alt[:, None] / big, -sinm * wk)
    g = jnp.concatenate([gp, gq], axis=1).astype(BF16)
    return dict(n=n, fc=min(n, 512), zfeat=zfeat, deltas=deltas, f_hi=f_hi, f_lo=f_lo, g=g)


def _out_proj_kernel(od_ref, og_ref, oh_ref, w_ref, x_ref, g_ref, m_ref, o_ref):
    for r in range(0, o_ref.shape[0], ROW_CHUNK):
        rows = slice(r, r + ROW_CHUNK)
        mix = _dot(od_ref[rows, :], w_ref[0:DIFF_W, :])
        mix = mix + _dot(og_ref[rows, :], w_ref[DIFF_W:DIFF_W + GQA_W, :])
        mix = mix + _dot(oh_ref[rows, :], w_ref[DIFF_W + GQA_W:, :])
        o_ref[rows, :] = x_ref[rows, :] + m_ref[2:3, :] * (_rms(mix) * g_ref[...])


def _out_proj(o_d, o_g, o_h, w, x, g, mods, per_batch):
    b, s, d = x.shape
    tm = min(s, 512)
    est = d * d * 2 + 2 * tm * d * 2 + 4 * tm * d * 4 + 2 * tm * d * 4
    return pl.pallas_call(
        _out_proj_kernel,
        grid=(b, s // tm),
        in_specs=[
            pl.BlockSpec((None, tm, DIFF_W), lambda bi, i: (bi, i, 0)),
            pl.BlockSpec((None, tm, GQA_W), lambda bi, i: (bi, i, 0)),
            pl.BlockSpec((None, tm, HY_CH), lambda bi, i: (bi, i, 0)),
            pl.BlockSpec((d, d), lambda bi, i: (0, 0), pipeline_mode=pl.Buffered(1)),
            pl.BlockSpec((None, tm, d), lambda bi, i: (bi, i, 0)),
            pl.BlockSpec((1, d), lambda bi, i: (0, 0)),
            pl.BlockSpec((None, N_MOD, d), lambda bi, i: (bi * per_batch, 0, 0)),
        ],
        out_specs=pl.BlockSpec((None, tm, d), lambda bi, i: (bi, i, 0)),
        out_shape=jax.ShapeDtypeStruct((b, s, d), F32),
        compiler_params=pltpu.CompilerParams(
            dimension_semantics=("parallel", "parallel"),
            vmem_limit_bytes=_vmem_limit(est)),
        name="out_proj",
    )(o_d, o_g, o_h, w, x, g.reshape(1, d), mods)


def _mlp_kernel(x_ref, g2_ref, g3_ref, m_ref, wu_ref, wd_ref, o_ref, xn_ref, *, nf, ncol):
    f = pl.program_id(2)
    tm, d = o_ref.shape
    row_chunks = [slice(r, r + ROW_CHUNK) for r in range(0, tm, ROW_CHUNK)]

    def hidden(xn):
        return jnp.square(jnp.maximum(_dot(xn, wu_ref[...]), 0.0)).astype(BF16)

    @pl.when(f == 0)
    def _():
        for rows in row_chunks:
            y = _rms(x_ref[rows, :]) * g2_ref[...]
            xn = (y * (1.0 + m_ref[4:5, :]) + m_ref[3:4, :]).astype(BF16)
            xn_ref[rows, :] = xn
            o_ref[rows, :] = _dot(hidden(xn), wd_ref[...])

    @pl.when((f > 0) & (f < nf - 1))
    def _():
        hmid = hidden(xn_ref[...])
        for c in range(d // ncol):
            o_ref[:, c * ncol:(c + 1) * ncol] += _dot(hmid, wd_ref[:, c * ncol:(c + 1) * ncol])

    @pl.when(f == nf - 1)
    def _():
        for rows in row_chunks:
            acc = o_ref[rows, :] + _dot(hidden(xn_ref[rows, :]), wd_ref[...])
            o_ref[rows, :] = x_ref[rows, :] + m_ref[5:6, :] * (_rms(acc) * g3_ref[...])


def _mlp(x, g2, g3, mods, w_up, w_down, per_batch):
    b, s, d = x.shape
    dff = w_up.shape[1]
    tm = min(s, 1024)
    tf = 512
    nf = dff // tf
    est = tm * d * 4 + 2 * tm * d * 4 + tm * d * 2 + 4 * d * tf * 2 + tm * tf * 6 + tm * 512 * 4
    return pl.pallas_call(
        functools.partial(_mlp_kernel, nf=nf, ncol=512),
        grid=(b, s // tm, nf),
        in_specs=[
            pl.BlockSpec((None, tm, d), lambda bi, i, f: (bi, i, 0), pipeline_mode=pl.Buffered(1)),
            pl.BlockSpec((1, d), lambda bi, i, f: (0, 0)),
            pl.BlockSpec((1, d), lambda bi, i, f: (0, 0)),
            pl.BlockSpec((None, N_MOD, d), lambda bi, i, f: (bi * per_batch, 0, 0)),
            pl.BlockSpec((d, tf), lambda bi, i, f: (0, f)),
            pl.BlockSpec((tf, d), lambda bi, i, f: (f, 0)),
        ],
        out_specs=pl.BlockSpec((None, tm, d), lambda bi, i, f: (bi, i, 0)),
        out_shape=jax.ShapeDtypeStruct((b, s, d), F32),
        scratch_shapes=[pltpu.VMEM((tm, d), BF16)],
        compiler_params=pltpu.CompilerParams(
            dimension_semantics=("parallel", "parallel", "arbitrary"),
            vmem_limit_bytes=_vmem_limit(est)),
        name="mlp",
    )(x, g2.reshape(1, d), g3.reshape(1, d), mods, w_up, w_down)


def _rope_tables(n_rows, head_dim):
    t_row = jnp.repeat(jnp.arange(n_rows, dtype=F32), GRID_W)
    t_col = jnp.tile(jnp.arange(GRID_W, dtype=F32), n_rows)
    d_axis = head_dim // 2
    inv = ROPE_THETA ** (-jnp.arange(0, d_axis, 2, dtype=F32) / d_axis)
    ang = jnp.concatenate([t_row[:, None] * inv, t_col[:, None] * inv], axis=-1)
    cos = jnp.repeat(jnp.cos(ang), 2, axis=-1)
    sin = jnp.repeat(jnp.sin(ang), 2, axis=-1)
    sign = jnp.where(jnp.arange(head_dim) % 2 == 0, -1.0, 1.0).astype(F32)
    sin = sin * sign
    reps = LANE // head_dim
    return jnp.tile(cos, (1, reps)), jnp.tile(sin, (1, reps))


def kernel(x, c, ctx, c_ctx, w_mod, b_mod, g_norm, w_in, w_out, diff_lam, diff_subln, gqa_q_norm,
           gqa_k_norm, gqa_out_norm, hy_conv_w, hy_conv_b, hy_w1, hy_b1, hy_w2, hy_b2, hy_w3, hy_b3,
           hy_wout, hy_freq, hy_bias, hy_out_norm, w_up, w_down):
    bsz, n_lat, d = x.shape
    n_ctx = ctx.shape[1]
    depth = w_mod.shape[0]
    assert d == D_MODEL and bsz + 1 <= MOD_ROWS
    assert n_lat % GRID_W == 0 and n_lat % 256 == 0 and n_ctx % 128 == 0

    tab_d = _rope_tables(n_lat // GRID_W, DIFF_QK)
    tab_g = _rope_tables(n_lat // GRID_W, GQA_HD)
    hyc_lat = _hy_constants(n_lat)
    hyc_ctx = _hy_constants(n_ctx)

    cpad = jnp.zeros((MOD_ROWS, d), F32).at[:bsz].set(c).at[bsz].set(c_ctx)
    mods = _modulations(cpad, w_mod, b_mod)
    lam_pad = jnp.pad(diff_lam, ((0, 0), (0, 4), (0, LANE - DIFF_QK)))

    tq_lat = min(n_lat, 512)
    tq_ctx = min(n_ctx, 256)
    xc = ctx
    for l in range(depth):
        last = l == depth - 1
        lam_init = 0.8 - 0.6 * math.exp(-0.3 * l)
        m_lat = mods[l, :bsz].reshape(bsz, N_MOD, d)
        m_ctx = mods[l, bsz:bsz + 1].reshape(1, N_MOD, d)
        w_in_l = w_in[l].astype(BF16)
        w_out_l = w_out[l].astype(BF16)
        w_up_l = w_up[l].astype(BF16)
        w_down_l = w_down[l].astype(BF16)
        filt = (hy_w1[l], hy_b1[l], hy_w2[l], hy_b2[l], hy_w3[l], hy_b3[l], hy_wout[l], hy_freq[l], hy_bias[l])

        h = _in_proj(x, g_norm[l, 0], m_lat, w_in_l, 1)
        xc_flat = xc.reshape(1, bsz * n_ctx, d)
        hc = _in_proj(xc_flat, g_norm[l, 0], m_ctx, w_in_l[:, KV_OFF:] if last else w_in_l, 0)
        hc = hc.reshape(bsz, n_ctx, -1)
        c_blk = 0 if last else KV_BLK

        o_d = _diff_attention(lam_pad[l], h, h, KV_BLK, hc, c_blk, tab_d, diff_subln[l], lam_init, tq_lat)
        o_g = _gqa_attention(h, h, KV_BLK, hc, c_blk, tab_g, gqa_q_norm[l], gqa_k_norm[l], gqa_out_norm[l],
                             tq_lat, 4)
        hp, hq = _hy_filters(hyc_lat, *filt)
        o_h = _hy_mixer(h, hyc_lat, hy_conv_w[l], hy_conv_b[l], hp, hq, hy_out_norm[l])
        x_new = _out_proj(o_d, o_g, o_h, w_out_l, x, g_norm[l, 1], m_lat, 1)

        if not last:
            oc_d = _diff_attention(lam_pad[l], hc, hc, KV_BLK, None, 0, None, diff_subln[l], lam_init, tq_ctx)
            oc_g = _gqa_attention(hc, hc, KV_BLK, None, 0, None, gqa_q_norm[l], gqa_k_norm[l],
                                  gqa_out_norm[l], tq_ctx, 1)
            hpc, hqc = _hy_filters(hyc_ctx, *filt)
            oc_h = _hy_mixer(hc, hyc_ctx, hy_conv_w[l], hy_conv_b[l], hpc, hqc, hy_out_norm[l])
            flat = lambda a: a.reshape(1, bsz * n_ctx, a.shape[-1])
            xc_flat = _out_proj(flat(oc_d), flat(oc_g), flat(oc_h), w_out_l, xc_flat, g_norm[l, 1], m_ctx, 0)
        x = x_new

        x = _mlp(x, g_norm[l, 2], g_norm[l, 3], m_lat, w_up_l, w_down_l, 1)
        if not last:
            xc_flat = _mlp(xc_flat, g_norm[l, 2], g_norm[l, 3], m_ctx, w_up_l, w_down_l, 0)
            xc = xc_flat.reshape(bsz, n_ctx, d)
    return x
```

```python
import functools
import math

import jax
import jax.numpy as jnp
from jax import lax
from jax.experimental import pallas as pl
from jax.experimental.pallas import tpu as pltpu

F32 = jnp.float32
BF16 = jnp.bfloat16

D_MODEL = 2048
GRID_W = 64
DIFF_W = 512
GQA_W = 1024
HY_CH = 512
DIFF_V = 128
DIFF_QK = 64
DIFF_HEADS = 4
GQA_HD = 128
GQA_KV_HEADS = 2
GQA_GROUP = 4
HY_EMB = 33
HY_BANDS = 16
HY_FFN = 64
HY_FAST_DECAY = 0.3
HY_SLOW_DECAY = 1.5
HY_TARGET = 1e-2
HY_SHIFT = 0.0
D_FF = 4 * D_MODEL
N_MOD = 6
ROPE_THETA = 10000.0
EPS = 1e-6

LANE = 128
DQ_BLK = 0
GQ_BLK = 4
HY_OFF = 1536
KV_OFF = 3072
KV_BLK = KV_OFF // LANE
DK_REL, DV_REL, GK_REL, GV_REL = 0, 4, 8, 10
N_IN = 4608
KV_W = N_IN - KV_OFF

VMEM_CAP = 64 * 1024 * 1024
MOD_ROWS = 24
PREP_ROWS = 256
LOG2E = 1.4426950408889634
KEY_CHUNK = 256
ROW_CHUNK = 256
DIFF_HEADS_PER_STEP = 2


def _vmem_limit(nbytes):
    return int(min(max(nbytes * 5 // 4 + (4 << 20), 32 << 20), VMEM_CAP - (6 << 20)))


def _rms(x):
    return x * lax.rsqrt(jnp.mean(x * x, axis=-1, keepdims=True) + EPS)


def _dot(a, b):
    return jnp.dot(a, b, preferred_element_type=F32)


def _dot_nt(a, b):
    return lax.dot_general(a, b, (((1,), (1,)), ((), ())), preferred_element_type=F32)


def _split_bf16(a):
    hi = a.astype(BF16)
    lo = (a - hi.astype(F32)).astype(BF16)
    return hi, lo


def _dot3(a, b):
    ah, al = _split_bf16(a)
    bh, bl = _split_bf16(b)
    return _dot(ah, bh) + (_dot(al, bh) + _dot(ah, bl))


def _rope(x, cos, sin_signed):
    lane = lax.broadcasted_iota(jnp.int32, x.shape, 1)
    nxt = pltpu.roll(x, LANE - 1, axis=1)
    prv = pltpu.roll(x, 1, axis=1)
    swapped = jnp.where(lane % 2 == 0, nxt, prv)
    return x * cos + swapped * sin_signed


def _mod_kernel(c_ref, w_ref, b_ref, o_ref):
    c = c_ref[...]
    s = c * (1.0 / (1.0 + jnp.exp(-c)))
    o_ref[0] = _dot(s.astype(BF16), w_ref[0].astype(BF16)) + b_ref[0]


def _modulations(cpad, w_mod, b_mod):
    nl, d, n = w_mod.shape
    tn = 1024
    return pl.pallas_call(
        _mod_kernel,
        grid=(nl, n // tn),
        in_specs=[
            pl.BlockSpec((MOD_ROWS, d), lambda l, j: (0, 0)),
            pl.BlockSpec((1, d, tn), lambda l, j: (l, 0, j)),
            pl.BlockSpec((1, 1, tn), lambda l, j: (l, 0, j)),
        ],
        out_specs=pl.BlockSpec((1, MOD_ROWS, tn), lambda l, j: (l, 0, j)),
        out_shape=jax.ShapeDtypeStruct((nl, MOD_ROWS, n), F32),
        compiler_params=pltpu.CompilerParams(
            dimension_semantics=("arbitrary", "arbitrary"),
            vmem_limit_bytes=_vmem_limit(2 * d * tn * 4 + d * tn * 2)),
        name="modulations",
    )(cpad, w_mod, b_mod.reshape(nl, 1, n))


def _in_proj_kernel(x_ref, g_ref, m_ref, w_ref, o_ref, xn_ref):
    j = pl.program_id(2)

    @pl.when(j == 0)
    def _():
        for r in range(0, o_ref.shape[0], ROW_CHUNK):
            rows = slice(r, r + ROW_CHUNK)
            y = _rms(x_ref[rows, :]) * g_ref[...]
            xn = (y * (1.0 + m_ref[1:2, :]) + m_ref[0:1, :]).astype(BF16)
            xn_ref[rows, :] = xn
            o_ref[rows, :] = _dot(xn, w_ref[...]).astype(o_ref.dtype)

    @pl.when(j > 0)
    def _():
        o_ref[...] = _dot(xn_ref[...], w_ref[...]).astype(o_ref.dtype)


def _in_proj(x, g, mods, w, per_batch):
    b, s, d = x.shape
    n = w.shape[1]
    tm = min(s, 1024)
    tn = 512
    est = 2 * tm * d * 4 + tm * d * 2 + 2 * d * tn * 2 + 2 * tm * tn * 2 + tm * tn * 4
    return pl.pallas_call(
        _in_proj_kernel,
        grid=(b, s // tm, n // tn),
        in_specs=[
            pl.BlockSpec((None, tm, d), lambda bi, i, j: (bi, i, 0)),
            pl.BlockSpec((1, d), lambda bi, i, j: (0, 0)),
            pl.BlockSpec((None, N_MOD, d), lambda bi, i, j: (bi * per_batch, 0, 0)),
            pl.BlockSpec((d, tn), lambda bi, i, j: (0, j)),
        ],
        out_specs=pl.BlockSpec((None, tm, tn), lambda bi, i, j: (bi, i, j)),
        out_shape=jax.ShapeDtypeStruct((b, s, n), BF16),
        scratch_shapes=[pltpu.VMEM((tm, d), BF16)],
        compiler_params=pltpu.CompilerParams(
            dimension_semantics=("parallel", "parallel", "arbitrary"),
            vmem_limit_bytes=_vmem_limit(est)),
        name="in_proj",
    )(x, g.reshape(1, d), mods, w)


def _attend_streams(streams, s_ref, finish):
    lk = streams[0][1].shape[0]
    ck = KEY_CHUNK if lk % KEY_CHUNK == 0 else LANE
    chunks = [slice(r, r + ck) for r in range(0, lk, ck)]
    n = len(streams)
    qt = mx = None
    prev = None
    for i in range(n + 1):
        if i < n:
            qt = streams[i][0]()
            mx = None
        lsum = acc = None
        for rows in chunks:
            if i < n:
                sj = _dot(streams[i][1][rows, :], qt)
                s_ref[i % 2, rows, :] = sj
                mj = jnp.max(sj, axis=0, keepdims=True)
                mx = mj if mx is None else jnp.maximum(mx, mj)
            if prev is not None:
                p = jnp.exp2(s_ref[prev[0] % 2, rows, :] - prev[1])
                lj = jnp.sum(p, axis=0, keepdims=True)
                oj = _dot(streams[prev[0]][2][:, rows], p.astype(BF16))
                lsum = lj if lsum is None else lsum + lj
                acc = oj if acc is None else acc + oj
        if prev is not None:
            finish(prev[0], acc, lsum)
        prev = (i, mx) if i < n else None


def _gqa_kernel(*refs, rope, prefix, heads_per_step):
    it = iter(refs)
    q_ref, k_ref, v_ref = next(it), next(it), next(it)
    kc_ref = vc_ref = None
    if prefix:
        kc_ref, vc_ref = next(it), next(it)
    cq_ref = sq_ref = ck_ref = sk_ref = None
    if rope:
        cq_ref, sq_ref, ck_ref, sk_ref = next(it), next(it), next(it), next(it)
    gq_ref, gk_ref, go_ref = next(it), next(it), next(it)
    o_ref = next(it)
    kall_ref, vt_ref, s_ref = next(it), next(it), next(it)

    c = kc_ref.shape[0] if prefix else 0
    s = k_ref.shape[0]

    @pl.when((pl.program_id(2) == 0) & (pl.program_id(3) == 0))
    def _():
        if prefix:
            kall_ref[0:c, :] = (_rms(kc_ref[...].astype(F32)) * gk_ref[...]).astype(BF16)
            vt_ref[:, 0:c] = vc_ref[...].astype(F32).T.astype(BF16)
        pr = min(s, PREP_ROWS)
        for r0 in range(0, s, pr):
            rows = slice(r0, r0 + pr)
            k = _rms(k_ref[rows, :].astype(F32)) * gk_ref[...]
            if rope:
                k = _rope(k, ck_ref[rows, :], sk_ref[rows, :])
            kall_ref[c + r0:c + r0 + pr, :] = k.astype(BF16)
            vt_ref[:, c + r0:c + r0 + pr] = v_ref[rows, :].astype(F32).T.astype(BF16)

    def query_t(hh):
        q = _rms(q_ref[:, hh * LANE:(hh + 1) * LANE].astype(F32)) * gq_ref[...]
        if rope:
            q = _rope(q, cq_ref[...], sq_ref[...])
        return (q * (GQA_HD ** -0.5 * LOG2E)).T.astype(BF16)

    def finish(hh, acc, denom):
        ot = acc / denom
        ot = ot * lax.rsqrt(jnp.mean(ot * ot, axis=0, keepdims=True) + EPS) * go_ref[...]
        o_ref[:, hh * LANE:(hh + 1) * LANE] = ot.T.astype(o_ref.dtype)

    streams = [(functools.partial(query_t, hh), kall_ref, vt_ref) for hh in range(heads_per_step)]
    _attend_streams(streams, s_ref, finish)


def _gqa_attention(hq, hk, kv_blk, hc, c_blk, tables, gq, gk, go, tq, heads_per_step):
    b, s, _ = hq.shape
    prefix = hc is not None
    rope = tables is not None
    hps = heads_per_step
    grid = (b, GQA_KV_HEADS, GQA_GROUP // hps, s // tq)
    in_specs = [
        pl.BlockSpec((None, tq, hps * LANE), lambda bi, g, r, i: (bi, i, (GQ_BLK + g * GQA_GROUP) // hps + r)),
        pl.BlockSpec((None, s, LANE), lambda bi, g, r, i: (bi, 0, kv_blk + GK_REL + g)),
        pl.BlockSpec((None, s, LANE), lambda bi, g, r, i: (bi, 0, kv_blk + GV_REL + g)),
    ]
    args = [hq, hk, hk]
    c = 0
    if prefix:
        c = hc.shape[1]
        in_specs += [
            pl.BlockSpec((None, c, LANE), lambda bi, g, r, i: (bi, 0, c_blk + GK_REL + g)),
            pl.BlockSpec((None, c, LANE), lambda bi, g, r, i: (bi, 0, c_blk + GV_REL + g)),
        ]
        args += [hc, hc]
    if rope:
        cos, sin = tables
        in_specs += [
            pl.BlockSpec((tq, LANE), lambda bi, g, r, i: (i, 0)),
            pl.BlockSpec((tq, LANE), lambda bi, g, r, i: (i, 0)),
            pl.BlockSpec((s, LANE), lambda bi, g, r, i: (0, 0)),
            pl.BlockSpec((s, LANE), lambda bi, g, r, i: (0, 0)),
        ]
        args += [cos, sin, cos, sin]
    vec = pl.BlockSpec((1, LANE), lambda bi, g, r, i: (0, 0))
    in_specs += [vec, vec, pl.BlockSpec((LANE, 1), lambda bi, g, r, i: (0, 0))]
    args += [gq.reshape(1, LANE), gk.reshape(1, LANE), go.reshape(LANE, 1)]
    lk = s + c
    est = 4 * s * LANE * 2 + 4 * s * LANE * 4 + 2 * lk * LANE * 2 + hps * 3 * tq * lk * 4
    return pl.pallas_call(
        functools.partial(_gqa_kernel, rope=rope, prefix=prefix, heads_per_step=hps),
        grid=grid,
        in_specs=in_specs,
        out_specs=pl.BlockSpec((None, tq, hps * LANE), lambda bi, g, r, i: (bi, i, g * GQA_GROUP // hps + r)),
        out_shape=jax.ShapeDtypeStruct((b, s, GQA_W), BF16),
        scratch_shapes=[pltpu.VMEM((lk, LANE), BF16), pltpu.VMEM((LANE, lk), BF16),
                        pltpu.VMEM((2, lk, tq), F32)],
        compiler_params=pltpu.CompilerParams(
            dimension_semantics=("arbitrary",) * 4,
            vmem_limit_bytes=_vmem_limit(est)),
        name="gqa_attention",
    )(*args)


def _diff_kernel(*refs, rope, prefix, lam_init):
    it = iter(refs)
    lp_ref, q_ref, k_ref, v_ref = next(it), next(it), next(it), next(it)
    kc_ref = vc_ref = None
    if prefix:
        kc_ref, vc_ref = next(it), next(it)
    cq_ref = sq_ref = ck_ref = sk_ref = None
    if rope:
        cq_ref, sq_ref, ck_ref, sk_ref = next(it), next(it), next(it), next(it)
    gs_ref = next(it)
    o_ref = next(it)
    kall_ref, vt_ref, s_ref, lam_ref = next(it), next(it), next(it), next(it)
    c = kc_ref.shape[0] if prefix else 0
    s = k_ref.shape[0]

    first = (pl.program_id(0) == 0) & (pl.program_id(1) == 0) & (pl.program_id(2) == 0)

    @pl.when(first)
    def _():
        p = lp_ref[...]
        t1 = jnp.sum(p[0:1] * p[1:2], axis=-1, keepdims=True)
        t2 = jnp.sum(p[2:3] * p[3:4], axis=-1, keepdims=True)
        lam_ref[...] = jnp.broadcast_to(jnp.exp(t1) - jnp.exp(t2) + lam_init, lam_ref.shape)

    @pl.when(pl.program_id(2) == 0)
    def _():
        for hh in range(DIFF_HEADS_PER_STEP):
            cols = slice(hh * LANE, (hh + 1) * LANE)
            if prefix:
                kall_ref[hh, 0:c, :] = kc_ref[:, cols]
                vt_ref[hh, :, 0:c] = vc_ref[:, cols].astype(F32).T.astype(BF16)
            pr = min(s, PREP_ROWS)
            for r0 in range(0, s, pr):
                rows = slice(r0, r0 + pr)
                if rope:
                    k = _rope(k_ref[rows, cols].astype(F32), ck_ref[rows, :], sk_ref[rows, :]).astype(BF16)
                else:
                    k = k_ref[rows, cols]
                kall_ref[hh, c + r0:c + r0 + pr, :] = k
                vt_ref[hh, :, c + r0:c + r0 + pr] = v_ref[rows, cols].astype(F32).T.astype(BF16)

    lam = lam_ref[0:1, 0:1]

    head_qt = {}

    def query_t(hh, mp):
        if hh not in head_qt:
            q = q_ref[:, hh * LANE:(hh + 1) * LANE].astype(F32)
            if rope:
                q = _rope(q, cq_ref[...], sq_ref[...])
            head_qt[hh] = (q * (DIFF_QK ** -0.5 * LOG2E)).T
        qt = head_qt[hh]
        row = lax.broadcasted_iota(jnp.int32, qt.shape, 0)
        keep = (row < DIFF_QK) if mp == 0 else (row >= DIFF_QK)
        return jnp.where(keep, qt, 0.0).astype(BF16)

    first_map = {}

    def finish(i, acc, denom):
        hh, mp = divmod(i, 2)
        if mp == 0:
            first_map[hh] = acc / denom
            return
        ot = first_map.pop(hh) - acc * (lam / denom)
        ot = ot * lax.rsqrt(jnp.mean(ot * ot, axis=0, keepdims=True) + EPS) * (gs_ref[...] * (1.0 - lam_init))
        o_ref[:, hh * LANE:(hh + 1) * LANE] = ot.T.astype(o_ref.dtype)

    streams = [(functools.partial(query_t, hh, mp), kall_ref.at[hh], vt_ref.at[hh])
               for hh in range(DIFF_HEADS_PER_STEP) for mp in range(2)]
    _attend_streams(streams, s_ref, finish)


def _diff_attention(lam_params, hq, hk, kv_blk, hc, c_blk, tables, gsub, lam_init, tq):
    b, s, _ = hq.shape
    prefix = hc is not None
    rope = tables is not None
    hps = DIFF_HEADS_PER_STEP
    w = hps * LANE
    grid = (b, DIFF_HEADS // hps, s // tq)
    in_specs = [
        pl.BlockSpec((8, LANE), lambda bi, h, i: (0, 0)),
        pl.BlockSpec((None, tq, w), lambda bi, h, i: (bi, i, DQ_BLK // hps + h)),
        pl.BlockSpec((None, s, w), lambda bi, h, i: (bi, 0, (kv_blk + DK_REL) // hps + h)),
        pl.BlockSpec((None, s, w), lambda bi, h, i: (bi, 0, (kv_blk + DV_REL) // hps + h)),
    ]
    args = [lam_params, hq, hk, hk]
    c = 0
    if prefix:
        c = hc.shape[1]
        in_specs += [
            pl.BlockSpec((None, c, w), lambda bi, h, i: (bi, 0, (c_blk + DK_REL) // hps + h)),
            pl.BlockSpec((None, c, w), lambda bi, h, i: (bi, 0, (c_blk + DV_REL) // hps + h)),
        ]
        args += [hc, hc]
    if rope:
        cos, sin = tables
        in_specs += [
            pl.BlockSpec((tq, LANE), lambda bi, h, i: (i, 0)),
            pl.BlockSpec((tq, LANE), lambda bi, h, i: (i, 0)),
            pl.BlockSpec((s, LANE), lambda bi, h, i: (0, 0)),
            pl.BlockSpec((s, LANE), lambda bi, h, i: (0, 0)),
        ]
        args += [cos, sin, cos, sin]
    in_specs.append(pl.BlockSpec((LANE, 1), lambda bi, h, i: (0, 0)))
    args.append(gsub.reshape(LANE, 1))
    lk = s + c
    est = 8 * s * w * 2 + 4 * s * LANE * 4 + 4 * lk * w * 2 + 5 * tq * lk * 4
    return pl.pallas_call(
        functools.partial(_diff_kernel, rope=rope, prefix=prefix, lam_init=lam_init),
        grid=grid,
        in_specs=in_specs,
        out_specs=pl.BlockSpec((None, tq, w), lambda bi, h, i: (bi, i, h)),
        out_shape=jax.ShapeDtypeStruct((b, s, DIFF_W), BF16),
        scratch_shapes=[pltpu.VMEM((hps, lk, LANE), BF16), pltpu.VMEM((hps, LANE, lk), BF16),
                        pltpu.VMEM((2, lk, tq), F32), pltpu.VMEM((8, LANE), F32)],
        compiler_params=pltpu.CompilerParams(
            dimension_semantics=("arbitrary",) * 3,
            vmem_limit_bytes=_vmem_limit(est)),
        name="diff_attention",
    )(*args)


def _hy_filter_kernel(z_ref, w1_ref, b1_ref, w2_ref, b2_ref, w3_ref, b3_ref, wo_ref, fr_ref,
                      bias_ref, dl_ref, fp_hi_ref, fp_lo_ref, fq_hi_ref, fq_lo_ref,
                      hp_ref, hq_ref, hs_hi, hs_lo, hd_hi, hd_lo, alt_ref):
    j = pl.program_id(0)

    @pl.when(j == 0)
    def _():
        z = z_ref[...]
        h = jnp.sin(fr_ref[0:1, :] * (_dot3(z, w1_ref[...]) + b1_ref[...]))
        h = jnp.sin(fr_ref[1:2, :] * (_dot3(h, w2_ref[...]) + b2_ref[...]))
        h = jnp.sin(fr_ref[2:3, :] * (_dot3(h, w3_ref[...]) + b3_ref[...]))
        h = _dot3(h, wo_ref[...])
        decay = jnp.exp(-z[:, 0:1] * jnp.abs(dl_ref[...])) + HY_SHIFT
        hf = h[:, :HY_CH] * decay
        hb = h[:, HY_CH:] * decay
        row = lax.broadcasted_iota(jnp.int32, hf.shape, 0)
        hf = hf + jnp.where(row == 0, bias_ref[...], 0.0)
        hb = jnp.where(row == 0, 0.0, hb)
        hs = hf + hb
        hd = hf - hb
        sign = jnp.where(row % 2 == 0, 1.0, -1.0)
        alt_ref[...] = jnp.broadcast_to(jnp.sum(sign * hs, axis=0, keepdims=True), alt_ref.shape)
        a, bb = _split_bf16(hs)
        hs_hi[...] = a
        hs_lo[...] = bb
        a, bb = _split_bf16(hd)
        hd_hi[...] = a
        hd_lo[...] = bb

    xp = _dot(fp_hi_ref[...], hs_hi[...]) + (_dot(fp_lo_ref[...], hs_hi[...]) + _dot(fp_hi_ref[...], hs_lo[...]))
    xq = _dot(fq_hi_ref[...], hd_hi[...]) + (_dot(fq_lo_ref[...], hd_hi[...]) + _dot(fq_hi_ref[...], hd_lo[...]))
    row = lax.broadcasted_iota(jnp.int32, xq.shape, 0)
    is0 = (row == 0) & (j == 0)
    hp_ref[...] = xp
    hq_ref[...] = jnp.where(is0, alt_ref[0:1, :], xq)


def _hy_filters(consts, w1, b1, w2, b2, w3, b3, wout, freq, bias):
    n = consts["n"]
    fc = consts["fc"]
    nf = n // fc
    pad = LANE - HY_FFN
    w1p = jnp.pad(w1, ((0, LANE - HY_EMB), (0, pad)))
    w2p = jnp.pad(w2, ((0, pad), (0, pad)))
    w3p = jnp.pad(w3, ((0, pad), (0, pad)))
    wop = jnp.pad(wout, ((0, pad), (0, 0)))
    b1p = jnp.pad(b1, (0, pad)).reshape(1, LANE)
    b2p = jnp.pad(b2, (0, pad)).reshape(1, LANE)
    b3p = jnp.pad(b3, (0, pad)).reshape(1, LANE)
    frp = jnp.pad(freq, ((0, 0), (0, pad)))
    full = lambda shape: pl.BlockSpec(shape, lambda j: (0,) * len(shape))
    fspec_p = pl.BlockSpec((fc, n), lambda j: (j, 0))
    fspec_q = pl.BlockSpec((fc, n), lambda j: (nf + j, 0))
    est = n * LANE * 4 * 2 + 8 * fc * n * 2 + 4 * n * HY_CH * 2 + 4 * fc * HY_CH * 4 + 8 * n * HY_CH * 4
    return pl.pallas_call(
        _hy_filter_kernel,
        grid=(nf,),
        in_specs=[
            full((n, LANE)), full((LANE, LANE)), full((1, LANE)), full((LANE, LANE)), full((1, LANE)),
            full((LANE, LANE)), full((1, LANE)), full((LANE, 2 * HY_CH)), full((3, LANE)),
            full((1, HY_CH)), full((1, HY_CH)),
            fspec_p, fspec_p, fspec_q, fspec_q,
        ],
        out_specs=[pl.BlockSpec((fc, HY_CH), lambda j: (j, 0)), pl.BlockSpec((fc, HY_CH), lambda j: (j, 0))],
        out_shape=[jax.ShapeDtypeStruct((n, HY_CH), F32), jax.ShapeDtypeStruct((n, HY_CH), F32)],
        scratch_shapes=[pltpu.VMEM((n, HY_CH), BF16)] * 4 + [pltpu.VMEM((8, HY_CH), F32)],
        compiler_params=pltpu.CompilerParams(
            dimension_semantics=("arbitrary",),
            vmem_limit_bytes=_vmem_limit(est)),
        name="hyena_filters",
    )(consts["zfeat"], w1p, b1p, w2p, b2p, w3p, b3p, wop, frp, bias.reshape(1, HY_CH), consts["deltas"],
      consts["f_hi"], consts["f_lo"], consts["f_hi"], consts["f_lo"])


def _short_conv(hy_ref, cw_ref, cb_ref, c0, n):
    x = hy_ref[:, c0:c0 + LANE].astype(F32)
    row = lax.broadcasted_iota(jnp.int32, x.shape, 0)
    prv = jnp.where(row == 0, 0.0, pltpu.roll(x, 1, axis=0))
    nxt = jnp.where(row == n - 1, 0.0, pltpu.roll(x, n - 1, axis=0))
    w = cw_ref[:, c0:c0 + LANE]
    return prv * w[0:1] + x * w[1:2] + nxt * w[2:3] + cb_ref[:, c0:c0 + LANE]


def _hy_mixer_kernel(hy_ref, cw_ref, cb_ref, fp_ref, fq_ref, hp_ref, hq_ref, gp_ref, gq_ref, gn_ref,
                     o_ref, z_ref, y_ref, *, n, nf):
    j = pl.program_id(1)
    nchunk = HY_CH // LANE

    @pl.when(j == 0)
    def _():
        for c in range(nchunk):
            x1 = _short_conv(hy_ref, cw_ref, cb_ref, HY_CH + c * LANE, n)
            v = _short_conv(hy_ref, cw_ref, cb_ref, 2 * HY_CH + c * LANE, n)
            z_ref[:, c * LANE:(c + 1) * LANE] = (v * x1).astype(BF16)
        y_ref[...] = jnp.zeros_like(y_ref)

    z = z_ref[...]
    xp = _dot(fp_ref[...], z)
    xq = _dot(fq_ref[...], z)
    hp = hp_ref[...]
    hq = hq_ref[...]
    row = lax.broadcasted_iota(jnp.int32, xp.shape, 0)
    is0 = (row == 0) & (j == 0)
    qq = xq * hq
    yp = xp * hp - jnp.where(is0, 0.0, qq)
    yq = jnp.where(is0, qq, xp * hq + xq * hp)
    y_ref[...] += _dot(gp_ref[...], yp.astype(BF16)) + _dot(gq_ref[...], yq.astype(BF16))

    @pl.when(j == nf - 1)
    def _():
        ss = jnp.zeros((n, 1), F32)
        for c in range(nchunk):
            x0 = _short_conv(hy_ref, cw_ref, cb_ref, c * LANE, n)
            prod = x0 * y_ref[:, c * LANE:(c + 1) * LANE]
            y_ref[:, c * LANE:(c + 1) * LANE] = prod
            ss = ss + jnp.sum(prod * prod, axis=-1, keepdims=True)
        inv = lax.rsqrt(ss * (1.0 / HY_CH) + EPS)
        o_ref[...] = (y_ref[...] * inv * gn_ref[...]).astype(o_ref.dtype)


def _hy_mixer(h, consts, conv_w, conv_b, hp, hq, gnorm):
    b, n, _ = h.shape
    fc = consts["fc"]
    nf = n // fc
    est = (n * 3 * HY_CH * 2 + 8 * fc * n * 2 + 4 * fc * HY_CH * 4 + n * HY_CH * (2 + 4 + 4)
           + 6 * fc * HY_CH * 4 + 6 * n * LANE * 4)
    return pl.pallas_call(
        functools.partial(_hy_mixer_kernel, n=n, nf=nf),
        grid=(b, nf),
        in_specs=[
            pl.BlockSpec((None, n, 3 * HY_CH), lambda bi, j: (bi, 0, HY_OFF // (3 * HY_CH)),
                         pipeline_mode=pl.Buffered(1)),
            pl.BlockSpec((3, 3 * HY_CH), lambda bi, j: (0, 0)),
            pl.BlockSpec((1, 3 * HY_CH), lambda bi, j: (0, 0)),
            pl.BlockSpec((fc, n), lambda bi, j: (j, 0)),
            pl.BlockSpec((fc, n), lambda bi, j: (nf + j, 0)),
            pl.BlockSpec((fc, HY_CH), lambda bi, j: (j, 0)),
            pl.BlockSpec((fc, HY_CH), lambda bi, j: (j, 0)),
            pl.BlockSpec((n, fc), lambda bi, j: (0, j)),
            pl.BlockSpec((n, fc), lambda bi, j: (0, nf + j)),
            pl.BlockSpec((1, HY_CH), lambda bi, j: (0, 0)),
        ],
        out_specs=pl.BlockSpec((None, n, HY_CH), lambda bi, j: (bi, 0, 0)),
        out_shape=jax.ShapeDtypeStruct((b, n, HY_CH), BF16),
        scratch_shapes=[pltpu.VMEM((n, HY_CH), BF16), pltpu.VMEM((n, HY_CH), F32)],
        compiler_params=pltpu.CompilerParams(
            dimension_semantics=("arbitrary", "arbitrary"),
            vmem_limit_bytes=_vmem_limit(est)),
        name="hyena_mixer",
    )(h, conv_w, conv_b.reshape(1, 3 * HY_CH), consts["f_hi"], consts["f_hi"], hp, hq,
      consts["g"], consts["g"], gnorm.reshape(1, HY_CH))


def _hy_constants(n):
    big = 2 * n
    t = jnp.linspace(0.0, 1.0, n, dtype=F32)[:, None]
    w = 2.0 * math.pi * jnp.arange(n, dtype=F32)[:, None] / n
    f = jnp.linspace(1e-4, HY_BANDS - 1, HY_BANDS, dtype=F32)[None, :]
    zfeat = jnp.concatenate([t, jnp.cos(f * w), -jnp.sin(f * w)], axis=-1)
    zfeat = jnp.pad(zfeat, ((0, 0), (0, LANE - HY_EMB)))
    min_decay = math.log(HY_TARGET) / HY_SLOW_DECAY
    max_decay = math.log(HY_TARGET) / HY_FAST_DECAY
    deltas = jnp.linspace(min_decay, max_decay, HY_CH, dtype=F32)[None, :]

    k = jnp.arange(n, dtype=jnp.int32)
    theta = ((k[:, None] * k[None, :]) % big).astype(F32) * (2.0 * math.pi / big)
    cosm = jnp.cos(theta)
    sinm = jnp.sin(theta)
    alt = jnp.where(k % 2 == 0, 1.0, -1.0).astype(F32)
    fq = jnp.where((k == 0)[:, None], alt[None, :], -sinm)
    fmat = jnp.concatenate([cosm, fq], axis=0)
    f_hi = fmat.astype(BF16)
    f_lo = (fmat - f_hi.astype(F32)).astype(BF16)
    wk = jnp.where(k == 0, 1.0, 2.0).astype(F32)[None, :] / big
    gp = cosm * wk
    gq = jnp.where((k == 0)[None, :], alt[:, None] / big, -sinm * wk)
    g = jnp.concatenate([gp, gq], axis=1).astype(BF16)
    return dict(n=n, fc=min(n, 512), zfeat=zfeat, deltas=deltas, f_hi=f_hi, f_lo=f_lo, g=g)


def _out_proj_kernel(od_ref, og_ref, oh_ref, w_ref, x_ref, g_ref, m_ref, o_ref):
    for r in range(0, o_ref.shape[0], ROW_CHUNK):
        rows = slice(r, r + ROW_CHUNK)
        mix = _dot(od_ref[rows, :], w_ref[0:DIFF_W, :])
        mix = mix + _dot(og_ref[rows, :], w_ref[DIFF_W:DIFF_W + GQA_W, :])
        mix = mix + _dot(oh_ref[rows, :], w_ref[DIFF_W + GQA_W:, :])
        o_ref[rows, :] = x_ref[rows, :] + m_ref[2:3, :] * (_rms(mix) * g_ref[...])


def _out_proj(o_d, o_g, o_h, w, x, g, mods, per_batch):
    b, s, d = x.shape
    tm = min(s, 512)
    est = d * d * 2 + 2 * tm * d * 2 + 4 * tm * d * 4 + 2 * tm * d * 4
    return pl.pallas_call(
        _out_proj_kernel,
        grid=(b, s // tm),
        in_specs=[
            pl.BlockSpec((None, tm, DIFF_W), lambda bi, i: (bi, i, 0)),
            pl.BlockSpec((None, tm, GQA_W), lambda bi, i: (bi, i, 0)),
            pl.BlockSpec((None, tm, HY_CH), lambda bi, i: (bi, i, 0)),
            pl.BlockSpec((d, d), lambda bi, i: (0, 0), pipeline_mode=pl.Buffered(1)),
            pl.BlockSpec((None, tm, d), lambda bi, i: (bi, i, 0)),
            pl.BlockSpec((1, d), lambda bi, i: (0, 0)),
            pl.BlockSpec((None, N_MOD, d), lambda bi, i: (bi * per_batch, 0, 0)),
        ],
        out_specs=pl.BlockSpec((None, tm, d), lambda bi, i: (bi, i, 0)),
        out_shape=jax.ShapeDtypeStruct((b, s, d), F32),
        compiler_params=pltpu.CompilerParams(
            dimension_semantics=("parallel", "parallel"),
            vmem_limit_bytes=_vmem_limit(est)),
        name="out_proj",
    )(o_d, o_g, o_h, w, x, g.reshape(1, d), mods)


def _mlp_kernel(x_ref, g2_ref, g3_ref, m_ref, wu_ref, wd_ref, o_ref, xn_ref, *, nf, ncol):
    f = pl.program_id(2)
    tm, d = o_ref.shape
    row_chunks = [slice(r, r + ROW_CHUNK) for r in range(0, tm, ROW_CHUNK)]

    def hidden(xn):
        return jnp.square(jnp.maximum(_dot(xn, wu_ref[...]), 0.0)).astype(BF16)

    @pl.when(f == 0)
    def _():
        for rows in row_chunks:
            y = _rms(x_ref[rows, :]) * g2_ref[...]
            xn = (y * (1.0 + m_ref[4:5, :]) + m_ref[3:4, :]).astype(BF16)
            xn_ref[rows, :] = xn
            o_ref[rows, :] = _dot(hidden(xn), wd_ref[...])

    @pl.when((f > 0) & (f < nf - 1))
    def _():
        hmid = hidden(xn_ref[...])
        for c in range(d // ncol):
            o_ref[:, c * ncol:(c + 1) * ncol] += _dot(hmid, wd_ref[:, c * ncol:(c + 1) * ncol])

    @pl.when(f == nf - 1)
    def _():
        for rows in row_chunks:
            acc = o_ref[rows, :] + _dot(hidden(xn_ref[rows, :]), wd_ref[...])
            o_ref[rows, :] = x_ref[rows, :] + m_ref[5:6, :] * (_rms(acc) * g3_ref[...])


def _mlp(x, g2, g3, mods, w_up, w_down, per_batch):
    b, s, d = x.shape
    dff = w_up.shape[1]
    tm = min(s, 1024)
    tf = 512
    nf = dff // tf
    est = tm * d * 4 + 2 * tm * d * 4 + tm * d * 2 + 4 * d * tf * 2 + tm * tf * 6 + tm * 512 * 4
    return pl.pallas_call(
        functools.partial(_mlp_kernel, nf=nf, ncol=512),
        grid=(b, s // tm, nf),
        in_specs=[
            pl.BlockSpec((None, tm, d), lambda bi, i, f: (bi, i, 0), pipeline_mode=pl.Buffered(1)),
            pl.BlockSpec((1, d), lambda bi, i, f: (0, 0)),
            pl.BlockSpec((1, d), lambda bi, i, f: (0, 0)),
            pl.BlockSpec((None, N_MOD, d), lambda bi, i, f: (bi * per_batch, 0, 0)),
            pl.BlockSpec((d, tf), lambda bi, i, f: (0, f)),
            pl.BlockSpec((tf, d), lambda bi, i, f: (f, 0)),
        ],
        out_specs=pl.BlockSpec((None, tm, d), lambda bi, i, f: (bi, i, 0)),
        out_shape=jax.ShapeDtypeStruct((b, s, d), F32),
        scratch_shapes=[pltpu.VMEM((tm, d), BF16)],
        compiler_params=pltpu.CompilerParams(
            dimension_semantics=("parallel", "parallel", "arbitrary"),
            vmem_limit_bytes=_vmem_limit(est)),
        name="mlp",
    )(x, g2.reshape(1, d), g3.reshape(1, d), mods, w_up, w_down)


def _rope_tables(n_rows, head_dim):
    t_row = jnp.repeat(jnp.arange(n_rows, dtype=F32), GRID_W)
    t_col = jnp.tile(jnp.arange(GRID_W, dtype=F32), n_rows)
    d_axis = head_dim // 2
    inv = ROPE_THETA ** (-jnp.arange(0, d_axis, 2, dtype=F32) / d_axis)
    ang = jnp.concatenate([t_row[:, None] * inv, t_col[:, None] * inv], axis=-1)
    cos = jnp.repeat(jnp.cos(ang), 2, axis=-1)
    sin = jnp.repeat(jnp.sin(ang), 2, axis=-1)
    sign = jnp.where(jnp.arange(head_dim) % 2 == 0, -1.0, 1.0).astype(F32)
    sin = sin * sign
    reps = LANE // head_dim
    return jnp.tile(cos, (1, reps)), jnp.tile(sin, (1, reps))


def kernel(x, c, ctx, c_ctx, w_mod, b_mod, g_norm, w_in, w_out, diff_lam, diff_subln, gqa_q_norm,
           gqa_k_norm, gqa_out_norm, hy_conv_w, hy_conv_b, hy_w1, hy_b1, hy_w2, hy_b2, hy_w3, hy_b3,
           hy_wout, hy_freq, hy_bias, hy_out_norm, w_up, w_down):
    bsz, n_lat, d = x.shape
    n_ctx = ctx.shape[1]
    depth = w_mod.shape[0]
    assert d == D_MODEL and bsz + 1 <= MOD_ROWS
    assert n_lat % GRID_W == 0 and n_lat % 256 == 0 and n_ctx % 128 == 0

    tab_d = _rope_tables(n_lat // GRID_W, DIFF_QK)
    tab_g = _rope_tables(n_lat // GRID_W, GQA_HD)
    hyc_lat = _hy_constants(n_lat)
    hyc_ctx = _hy_constants(n_ctx)

    cpad = jnp.zeros((MOD_ROWS, d), F32).at[:bsz].set(c).at[bsz].set(c_ctx)
    mods = _modulations(cpad, w_mod, b_mod)
    lam_pad = jnp.pad(diff_lam, ((0, 0), (0, 4), (0, LANE - DIFF_QK)))

    tq_lat = min(n_lat, 512)
    tq_ctx = min(n_ctx, 256)
    xc = ctx
    for l in range(depth):
        last = l == depth - 1
        lam_init = 0.8 - 0.6 * math.exp(-0.3 * l)
        m_lat = mods[l, :bsz].reshape(bsz, N_MOD, d)
        m_ctx = mods[l, bsz:bsz + 1].reshape(1, N_MOD, d)
        w_in_l = w_in[l].astype(BF16)
        w_out_l = w_out[l].astype(BF16)
        w_up_l = w_up[l].astype(BF16)
        w_down_l = w_down[l].astype(BF16)
        filt = (hy_w1[l], hy_b1[l], hy_w2[l], hy_b2[l], hy_w3[l], hy_b3[l], hy_wout[l], hy_freq[l], hy_bias[l])

        h = _in_proj(x, g_norm[l, 0], m_lat, w_in_l, 1)
        xc_flat = xc.reshape(1, bsz * n_ctx, d)
        hc = _in_proj(xc_flat, g_norm[l, 0], m_ctx, w_in_l[:, KV_OFF:] if last else w_in_l, 0)
        hc = hc.reshape(bsz, n_ctx, -1)
        c_blk = 0 if last else KV_BLK

        o_d = _diff_attention(lam_pad[l], h, h, KV_BLK, hc, c_blk, tab_d, diff_subln[l], lam_init, tq_lat)
        o_g = _gqa_attention(h, h, KV_BLK, hc, c_blk, tab_g, gqa_q_norm[l], gqa_k_norm[l], gqa_out_norm[l],
                             tq_lat, 4)
        hp, hq = _hy_filters(hyc_lat, *filt)
        o_h = _hy_mixer(h, hyc_lat, hy_conv_w[l], hy_conv_b[l], hp, hq, hy_out_norm[l])
        x_new = _out_proj(o_d, o_g, o_h, w_out_l, x, g_norm[l, 1], m_lat, 1)

        if not last:
            oc_d = _diff_attention(lam_pad[l], hc, hc, KV_BLK, None, 0, None, diff_subln[l], lam_init, tq_ctx)
            oc_g = _gqa_attention(hc, hc, KV_BLK, None, 0, None, gqa_q_norm[l], gqa_k_norm[l],
                                  gqa_out_norm[l], tq_ctx, 1)
            hpc, hqc = _hy_filters(hyc_ctx, *filt)
            oc_h = _hy_mixer(hc, hyc_ctx, hy_conv_w[l], hy_conv_b[l], hpc, hqc, hy_out_norm[l])
            flat = lambda a: a.reshape(1, bsz * n_ctx, a.shape[-1])
            xc_flat = _out_proj(flat(oc_d), flat(oc_g), flat(oc_h), w_out_l, xc_flat, g_norm[l, 1], m_ctx, 0)
        x = x_new

        x = _mlp(x, g_norm[l, 2], g_norm[l, 3], m_lat, w_up_l, w_down_l, 1)
        if not last:
            xc_flat = _mlp(xc_flat, g_norm[l, 2], g_norm[l, 3], m_ctx, w_up_l, w_down_l, 0)
            xc = xc_flat.reshape(bsz, n_ctx, d)
    return x
```

```python
import functools
import math

import jax
import jax.numpy as jnp
from jax import lax
from jax.experimental import pallas as pl
from jax.experimental.pallas import tpu as pltpu

F32 = jnp.float32
BF16 = jnp.bfloat16

D_MODEL = 2048
GRID_W = 64
DIFF_W = 512
GQA_W = 1024
HY_CH = 512
DIFF_V = 128
DIFF_QK = 64
DIFF_HEADS = 4
GQA_HD = 128
GQA_KV_HEADS = 2
GQA_GROUP = 4
HY_EMB = 33
HY_BANDS = 16
HY_FFN = 64
HY_FAST_DECAY = 0.3
HY_SLOW_DECAY = 1.5
HY_TARGET = 1e-2
HY_SHIFT = 0.0
D_FF = 4 * D_MODEL
N_MOD = 6
ROPE_THETA = 10000.0
EPS = 1e-6

LANE = 128
DQ_BLK = 0
GQ_BLK = 4
HY_OFF = 1536
KV_OFF = 3072
KV_BLK = KV_OFF // LANE
DK_REL, DV_REL, GK_REL, GV_REL = 0, 4, 8, 10
N_IN = 4608
KV_W = N_IN - KV_OFF

VMEM_CAP = 64 * 1024 * 1024
MOD_ROWS = 24
PREP_ROWS = 256
LOG2E = 1.4426950408889634
IN_PROJ_COLS = 1536
MLP_HIDDEN_COLS = 1024
KEY_CHUNK = 256
ROW_CHUNK = 256
DIFF_HEADS_PER_STEP = 2


def _vmem_limit(nbytes):
    return int(min(max(nbytes * 5 // 4 + (4 << 20), 32 << 20), VMEM_CAP - (6 << 20)))


def _rms(x):
    return x * lax.rsqrt(jnp.mean(x * x, axis=-1, keepdims=True) + EPS)


def _dot(a, b):
    return jnp.dot(a, b, preferred_element_type=F32)


def _dot_nt(a, b):
    return lax.dot_general(a, b, (((1,), (1,)), ((), ())), preferred_element_type=F32)


def _split_bf16(a):
    hi = a.astype(BF16)
    lo = (a - hi.astype(F32)).astype(BF16)
    return hi, lo


def _dot3(a, b):
    ah, al = _split_bf16(a)
    bh, bl = _split_bf16(b)
    return _dot(ah, bh) + (_dot(al, bh) + _dot(ah, bl))


def _rope(x, cos, sin_signed):
    lane = lax.broadcasted_iota(jnp.int32, x.shape, 1)
    nxt = pltpu.roll(x, LANE - 1, axis=1)
    prv = pltpu.roll(x, 1, axis=1)
    swapped = jnp.where(lane % 2 == 0, nxt, prv)
    return x * cos + swapped * sin_signed


def _mod_kernel(c_ref, w_ref, b_ref, o_ref):
    c = c_ref[...]
    s = c * (1.0 / (1.0 + jnp.exp(-c)))
    o_ref[0] = _dot(s.astype(BF16), w_ref[0].astype(BF16)) + b_ref[0]


def _modulations(cpad, w_mod, b_mod):
    nl, d, n = w_mod.shape
    tn = 1024
    return pl.pallas_call(
        _mod_kernel,
        grid=(nl, n // tn),
        in_specs=[
            pl.BlockSpec((MOD_ROWS, d), lambda l, j: (0, 0)),
            pl.BlockSpec((1, d, tn), lambda l, j: (l, 0, j)),
            pl.BlockSpec((1, 1, tn), lambda l, j: (l, 0, j)),
        ],
        out_specs=pl.BlockSpec((1, MOD_ROWS, tn), lambda l, j: (l, 0, j)),
        out_shape=jax.ShapeDtypeStruct((nl, MOD_ROWS, n), F32),
        compiler_params=pltpu.CompilerParams(
            dimension_semantics=("arbitrary", "arbitrary"),
            vmem_limit_bytes=_vmem_limit(2 * d * tn * 4 + d * tn * 2)),
        name="modulations",
    )(cpad, w_mod, b_mod.reshape(nl, 1, n))


def _in_proj_kernel(x_ref, g_ref, m_ref, w_ref, o_ref, xn_ref):
    j = pl.program_id(2)

    @pl.when(j == 0)
    def _():
        for r in range(0, o_ref.shape[0], ROW_CHUNK):
            rows = slice(r, r + ROW_CHUNK)
            y = _rms(x_ref[rows, :]) * g_ref[...]
            xn = (y * (1.0 + m_ref[1:2, :]) + m_ref[0:1, :]).astype(BF16)
            xn_ref[rows, :] = xn
            o_ref[rows, :] = _dot(xn, w_ref[...]).astype(o_ref.dtype)

    @pl.when(j > 0)
    def _():
        o_ref[...] = _dot(xn_ref[...], w_ref[...]).astype(o_ref.dtype)


def _in_proj(x, g, mods, w, layer, col0, n, per_batch):
    b, s, d = x.shape
    tm = min(s, 1024)
    tn = IN_PROJ_COLS
    assert n % tn == 0 and col0 % tn == 0
    jb = col0 // tn
    est = 2 * tm * d * 4 + tm * d * 2 + 2 * d * tn * 2 + 2 * tm * tn * 2 + tm * tn * 4
    return pl.pallas_call(
        _in_proj_kernel,
        grid=(b, s // tm, n // tn),
        in_specs=[
            pl.BlockSpec((None, tm, d), lambda bi, i, j: (bi, i, 0)),
            pl.BlockSpec((1, d), lambda bi, i, j: (0, 0)),
            pl.BlockSpec((None, N_MOD, d), lambda bi, i, j: (bi * per_batch, 0, 0)),
            pl.BlockSpec((None, d, tn), lambda bi, i, j: (layer, 0, jb + j)),
        ],
        out_specs=pl.BlockSpec((None, tm, tn), lambda bi, i, j: (bi, i, j)),
        out_shape=jax.ShapeDtypeStruct((b, s, n), BF16),
        scratch_shapes=[pltpu.VMEM((tm, d), BF16)],
        compiler_params=pltpu.CompilerParams(
            dimension_semantics=("parallel", "parallel", "arbitrary"),
            vmem_limit_bytes=_vmem_limit(est)),
        name="in_proj",
    )(x, g.reshape(1, d), mods, w)


def _attend_streams(streams, s_ref, finish):
    lk = streams[0][1].shape[0]
    ck = KEY_CHUNK if lk % KEY_CHUNK == 0 else LANE
    chunks = [slice(r, r + ck) for r in range(0, lk, ck)]
    n = len(streams)
    qt = mx = None
    prev = None
    for i in range(n + 1):
        if i < n:
            qt = streams[i][0]()
            mx = None
        lsum = acc = None
        for rows in chunks:
            if i < n:
                sj = _dot(streams[i][1][rows, :], qt)
                s_ref[i % 2, rows, :] = sj
                mj = jnp.max(sj, axis=0, keepdims=True)
                mx = mj if mx is None else jnp.maximum(mx, mj)
            if prev is not None:
                p = jnp.exp2(s_ref[prev[0] % 2, rows, :] - prev[1])
                lj = jnp.sum(p, axis=0, keepdims=True)
                oj = _dot(streams[prev[0]][2][:, rows], p.astype(BF16))
                lsum = lj if lsum is None else lsum + lj
                acc = oj if acc is None else acc + oj
        if prev is not None:
            finish(prev[0], acc, lsum)
        prev = (i, mx) if i < n else None


def _gqa_kernel(*refs, rope, prefix, heads_per_step):
    it = iter(refs)
    q_ref, k_ref, v_ref = next(it), next(it), next(it)
    kc_ref = vc_ref = None
    if prefix:
        kc_ref, vc_ref = next(it), next(it)
    cq_ref = sq_ref = ck_ref = sk_ref = None
    if rope:
        cq_ref, sq_ref, ck_ref, sk_ref = next(it), next(it), next(it), next(it)
    gq_ref, gk_ref, go_ref = next(it), next(it), next(it)
    o_ref = next(it)
    kall_ref, vt_ref, s_ref = next(it), next(it), next(it)

    c = kc_ref.shape[0] if prefix else 0
    s = k_ref.shape[0]

    @pl.when((pl.program_id(2) == 0) & (pl.program_id(3) == 0))
    def _():
        if prefix:
            kall_ref[0:c, :] = (_rms(kc_ref[...].astype(F32)) * gk_ref[...]).astype(BF16)
            vt_ref[:, 0:c] = vc_ref[...].astype(F32).T.astype(BF16)
        pr = min(s, PREP_ROWS)
        for r0 in range(0, s, pr):
            rows = slice(r0, r0 + pr)
            k = _rms(k_ref[rows, :].astype(F32)) * gk_ref[...]
            if rope:
                k = _rope(k, ck_ref[rows, :], sk_ref[rows, :])
            kall_ref[c + r0:c + r0 + pr, :] = k.astype(BF16)
            vt_ref[:, c + r0:c + r0 + pr] = v_ref[rows, :].astype(F32).T.astype(BF16)

    def query_t(hh):
        q = _rms(q_ref[:, hh * LANE:(hh + 1) * LANE].astype(F32)) * gq_ref[...]
        if rope:
            q = _rope(q, cq_ref[...], sq_ref[...])
        return (q * (GQA_HD ** -0.5 * LOG2E)).T.astype(BF16)

    def finish(hh, acc, denom):
        ot = acc / denom
        ot = ot * lax.rsqrt(jnp.mean(ot * ot, axis=0, keepdims=True) + EPS) * go_ref[...]
        o_ref[:, hh * LANE:(hh + 1) * LANE] = ot.T.astype(o_ref.dtype)

    streams = [(functools.partial(query_t, hh), kall_ref, vt_ref) for hh in range(heads_per_step)]
    _attend_streams(streams, s_ref, finish)


def _gqa_attention(hq, hk, kv_blk, hc, c_blk, tables, gq, gk, go, tq, heads_per_step):
    b, s, _ = hq.shape
    prefix = hc is not None
    rope = tables is not None
    hps = heads_per_step
    grid = (b, GQA_KV_HEADS, GQA_GROUP // hps, s // tq)
    in_specs = [
        pl.BlockSpec((None, tq, hps * LANE), lambda bi, g, r, i: (bi, i, (GQ_BLK + g * GQA_GROUP) // hps + r)),
        pl.BlockSpec((None, s, LANE), lambda bi, g, r, i: (bi, 0, kv_blk + GK_REL + g)),
        pl.BlockSpec((None, s, LANE), lambda bi, g, r, i: (bi, 0, kv_blk + GV_REL + g)),
    ]
    args = [hq, hk, hk]
    c = 0
    if prefix:
        c = hc.shape[1]
        in_specs += [
            pl.BlockSpec((None, c, LANE), lambda bi, g, r, i: (bi, 0, c_blk + GK_REL + g)),
            pl.BlockSpec((None, c, LANE), lambda bi, g, r, i: (bi, 0, c_blk + GV_REL + g)),
        ]
        args += [hc, hc]
    if rope:
        cos, sin = tables
        in_specs += [
            pl.BlockSpec((tq, LANE), lambda bi, g, r, i: (i, 0)),
            pl.BlockSpec((tq, LANE), lambda bi, g, r, i: (i, 0)),
            pl.BlockSpec((s, LANE), lambda bi, g, r, i: (0, 0)),
            pl.BlockSpec((s, LANE), lambda bi, g, r, i: (0, 0)),
        ]
        args += [cos, sin, cos, sin]
    vec = pl.BlockSpec((1, LANE), lambda bi, g, r, i: (0, 0))
    in_specs += [vec, vec, pl.BlockSpec((LANE, 1), lambda bi, g, r, i: (0, 0))]
    args += [gq.reshape(1, LANE), gk.reshape(1, LANE), go.reshape(LANE, 1)]
    lk = s + c
    est = 4 * s * LANE * 2 + 4 * s * LANE * 4 + 2 * lk * LANE * 2 + hps * 3 * tq * lk * 4
    return pl.pallas_call(
        functools.partial(_gqa_kernel, rope=rope, prefix=prefix, heads_per_step=hps),
        grid=grid,
        in_specs=in_specs,
        out_specs=pl.BlockSpec((None, tq, hps * LANE), lambda bi, g, r, i: (bi, i, g * GQA_GROUP // hps + r)),
        out_shape=jax.ShapeDtypeStruct((b, s, GQA_W), BF16),
        scratch_shapes=[pltpu.VMEM((lk, LANE), BF16), pltpu.VMEM((LANE, lk), BF16),
                        pltpu.VMEM((2, lk, tq), F32)],
        compiler_params=pltpu.CompilerParams(
            dimension_semantics=("arbitrary",) * 4,
            vmem_limit_bytes=_vmem_limit(est)),
        name="gqa_attention",
    )(*args)


def _diff_kernel(*refs, rope, prefix, lam_init):
    it = iter(refs)
    lp_ref, q_ref, k_ref, v_ref = next(it), next(it), next(it), next(it)
    kc_ref = vc_ref = None
    if prefix:
        kc_ref, vc_ref = next(it), next(it)
    cq_ref = sq_ref = ck_ref = sk_ref = None
    if rope:
        cq_ref, sq_ref, ck_ref, sk_ref = next(it), next(it), next(it), next(it)
    gs_ref = next(it)
    o_ref = next(it)
    kall_ref, vt_ref, s_ref, lam_ref = next(it), next(it), next(it), next(it)
    c = kc_ref.shape[0] if prefix else 0
    s = k_ref.shape[0]

    first = (pl.program_id(0) == 0) & (pl.program_id(1) == 0) & (pl.program_id(2) == 0)

    @pl.when(first)
    def _():
        p = lp_ref[...]
        t1 = jnp.sum(p[0:1] * p[1:2], axis=-1, keepdims=True)
        t2 = jnp.sum(p[2:3] * p[3:4], axis=-1, keepdims=True)
        lam_ref[...] = jnp.broadcast_to(jnp.exp(t1) - jnp.exp(t2) + lam_init, lam_ref.shape)

    @pl.when(pl.program_id(2) == 0)
    def _():
        for hh in range(DIFF_HEADS_PER_STEP):
            cols = slice(hh * LANE, (hh + 1) * LANE)
            if prefix:
                kall_ref[hh, 0:c, :] = kc_ref[:, cols]
                vt_ref[hh, :, 0:c] = vc_ref[:, cols].astype(F32).T.astype(BF16)
            pr = min(s, PREP_ROWS)
            for r0 in range(0, s, pr):
                rows = slice(r0, r0 + pr)
                if rope:
                    k = _rope(k_ref[rows, cols].astype(F32), ck_ref[rows, :], sk_ref[rows, :]).astype(BF16)
                else:
                    k = k_ref[rows, cols]
                kall_ref[hh, c + r0:c + r0 + pr, :] = k
                vt_ref[hh, :, c + r0:c + r0 + pr] = v_ref[rows, cols].astype(F32).T.astype(BF16)

    lam = lam_ref[0:1, 0:1]

    head_qt = {}

    def query_t(hh, mp):
        if hh not in head_qt:
            q = q_ref[:, hh * LANE:(hh + 1) * LANE].astype(F32)
            if rope:
                q = _rope(q, cq_ref[...], sq_ref[...])
            head_qt[hh] = (q * (DIFF_QK ** -0.5 * LOG2E)).T
        qt = head_qt[hh]
        row = lax.broadcasted_iota(jnp.int32, qt.shape, 0)
        keep = (row < DIFF_QK) if mp == 0 else (row >= DIFF_QK)
        return jnp.where(keep, qt, 0.0).astype(BF16)

    first_map = {}

    def finish(i, acc, denom):
        hh, mp = divmod(i, 2)
        if mp == 0:
            first_map[hh] = acc / denom
            return
        ot = first_map.pop(hh) - acc * (lam / denom)
        ot = ot * lax.rsqrt(jnp.mean(ot * ot, axis=0, keepdims=True) + EPS) * (gs_ref[...] * (1.0 - lam_init))
        o_ref[:, hh * LANE:(hh + 1) * LANE] = ot.T.astype(o_ref.dtype)

    streams = [(functools.partial(query_t, hh, mp), kall_ref.at[hh], vt_ref.at[hh])
               for hh in range(DIFF_HEADS_PER_STEP) for mp in range(2)]
    _attend_streams(streams, s_ref, finish)


def _diff_attention(lam_params, hq, hk, kv_blk, hc, c_blk, tables, gsub, lam_init, tq):
    b, s, _ = hq.shape
    prefix = hc is not None
    rope = tables is not None
    hps = DIFF_HEADS_PER_STEP
    w = hps * LANE
    grid = (b, DIFF_HEADS // hps, s // tq)
    in_specs = [
        pl.BlockSpec((8, LANE), lambda bi, h, i: (0, 0)),
        pl.BlockSpec((None, tq, w), lambda bi, h, i: (bi, i, DQ_BLK // hps + h)),
        pl.BlockSpec((None, s, w), lambda bi, h, i: (bi, 0, (kv_blk + DK_REL) // hps + h)),
        pl.BlockSpec((None, s, w), lambda bi, h, i: (bi, 0, (kv_blk + DV_REL) // hps + h)),
    ]
    args = [lam_params, hq, hk, hk]
    c = 0
    if prefix:
        c = hc.shape[1]
        in_specs += [
            pl.BlockSpec((None, c, w), lambda bi, h, i: (bi, 0, (c_blk + DK_REL) // hps + h)),
            pl.BlockSpec((None, c, w), lambda bi, h, i: (bi, 0, (c_blk + DV_REL) // hps + h)),
        ]
        args += [hc, hc]
    if rope:
        cos, sin = tables
        in_specs += [
            pl.BlockSpec((tq, LANE), lambda bi, h, i: (i, 0)),
            pl.BlockSpec((tq, LANE), lambda bi, h, i: (i, 0)),
            pl.BlockSpec((s, LANE), lambda bi, h, i: (0, 0)),
            pl.BlockSpec((s, LANE), lambda bi, h, i: (0, 0)),
        ]
        args += [cos, sin, cos, sin]
    in_specs.append(pl.BlockSpec((LANE, 1), lambda bi, h, i: (0, 0)))
    args.append(gsub.reshape(LANE, 1))
    lk = s + c
    est = 8 * s * w * 2 + 4 * s * LANE * 4 + 4 * lk * w * 2 + 5 * tq * lk * 4
    return pl.pallas_call(
        functools.partial(_diff_kernel, rope=rope, prefix=prefix, lam_init=lam_init),
        grid=grid,
        in_specs=in_specs,
        out_specs=pl.BlockSpec((None, tq, w), lambda bi, h, i: (bi, i, h)),
        out_shape=jax.ShapeDtypeStruct((b, s, DIFF_W), BF16),
        scratch_shapes=[pltpu.VMEM((hps, lk, LANE), BF16), pltpu.VMEM((hps, LANE, lk), BF16),
                        pltpu.VMEM((2, lk, tq), F32), pltpu.VMEM((8, LANE), F32)],
        compiler_params=pltpu.CompilerParams(
            dimension_semantics=("arbitrary",) * 3,
            vmem_limit_bytes=_vmem_limit(est)),
        name="diff_attention",
    )(*args)


def _hy_filter_kernel(z_ref, w1_ref, b1_ref, w2_ref, b2_ref, w3_ref, b3_ref, wo_ref, fr_ref,
                      bias_ref, dl_ref, fp_hi_ref, fp_lo_ref, fq_hi_ref, fq_lo_ref,
                      hp_ref, hq_ref, hs_hi, hs_lo, hd_hi, hd_lo, alt_ref):
    j = pl.program_id(0)

    @pl.when(j == 0)
    def _():
        z = z_ref[...]
        h = jnp.sin(fr_ref[0:1, :] * (_dot3(z, w1_ref[...]) + b1_ref[...]))
        h = jnp.sin(fr_ref[1:2, :] * (_dot3(h, w2_ref[...]) + b2_ref[...]))
        h = jnp.sin(fr_ref[2:3, :] * (_dot3(h, w3_ref[...]) + b3_ref[...]))
        h = _dot3(h, wo_ref[...])
        decay = jnp.exp(-z[:, 0:1] * jnp.abs(dl_ref[...])) + HY_SHIFT
        hf = h[:, :HY_CH] * decay
        hb = h[:, HY_CH:] * decay
        row = lax.broadcasted_iota(jnp.int32, hf.shape, 0)
        hf = hf + jnp.where(row == 0, bias_ref[...], 0.0)
        hb = jnp.where(row == 0, 0.0, hb)
        hs = hf + hb
        hd = hf - hb
        sign = jnp.where(row % 2 == 0, 1.0, -1.0)
        alt_ref[...] = jnp.broadcast_to(jnp.sum(sign * hs, axis=0, keepdims=True), alt_ref.shape)
        a, bb = _split_bf16(hs)
        hs_hi[...] = a
        hs_lo[...] = bb
        a, bb = _split_bf16(hd)
        hd_hi[...] = a
        hd_lo[...] = bb

    xp = _dot(fp_hi_ref[...], hs_hi[...]) + (_dot(fp_lo_ref[...], hs_hi[...]) + _dot(fp_hi_ref[...], hs_lo[...]))
    xq = _dot(fq_hi_ref[...], hd_hi[...]) + (_dot(fq_lo_ref[...], hd_hi[...]) + _dot(fq_hi_ref[...], hd_lo[...]))
    row = lax.broadcasted_iota(jnp.int32, xq.shape, 0)
    is0 = (row == 0) & (j == 0)
    hp_ref[...] = xp
    hq_ref[...] = jnp.where(is0, alt_ref[0:1, :], xq)


def _hy_filters(consts, w1, b1, w2, b2, w3, b3, wout, freq, bias):
    n = consts["n"]
    fc = consts["fc"]
    nf = n // fc
    pad = LANE - HY_FFN
    w1p = jnp.pad(w1, ((0, LANE - HY_EMB), (0, pad)))
    w2p = jnp.pad(w2, ((0, pad), (0, pad)))
    w3p = jnp.pad(w3, ((0, pad), (0, pad)))
    wop = jnp.pad(wout, ((0, pad), (0, 0)))
    b1p = jnp.pad(b1, (0, pad)).reshape(1, LANE)
    b2p = jnp.pad(b2, (0, pad)).reshape(1, LANE)
    b3p = jnp.pad(b3, (0, pad)).reshape(1, LANE)
    frp = jnp.pad(freq, ((0, 0), (0, pad)))
    full = lambda shape: pl.BlockSpec(shape, lambda j: (0,) * len(shape))
    fspec_p = pl.BlockSpec((fc, n), lambda j: (j, 0))
    fspec_q = pl.BlockSpec((fc, n), lambda j: (nf + j, 0))
    est = n * LANE * 4 * 2 + 8 * fc * n * 2 + 4 * n * HY_CH * 2 + 4 * fc * HY_CH * 4 + 8 * n * HY_CH * 4
    return pl.pallas_call(
        _hy_filter_kernel,
        grid=(nf,),
        in_specs=[
            full((n, LANE)), full((LANE, LANE)), full((1, LANE)), full((LANE, LANE)), full((1, LANE)),
            full((LANE, LANE)), full((1, LANE)), full((LANE, 2 * HY_CH)), full((3, LANE)),
            full((1, HY_CH)), full((1, HY_CH)),
            fspec_p, fspec_p, fspec_q, fspec_q,
        ],
        out_specs=[pl.BlockSpec((fc, HY_CH), lambda j: (j, 0)), pl.BlockSpec((fc, HY_CH), lambda j: (j, 0))],
        out_shape=[jax.ShapeDtypeStruct((n, HY_CH), F32), jax.ShapeDtypeStruct((n, HY_CH), F32)],
        scratch_shapes=[pltpu.VMEM((n, HY_CH), BF16)] * 4 + [pltpu.VMEM((8, HY_CH), F32)],
        compiler_params=pltpu.CompilerParams(
            dimension_semantics=("arbitrary",),
            vmem_limit_bytes=_vmem_limit(est)),
        name="hyena_filters",
    )(consts["zfeat"], w1p, b1p, w2p, b2p, w3p, b3p, wop, frp, bias.reshape(1, HY_CH), consts["deltas"],
      consts["f_hi"], consts["f_lo"], consts["f_hi"], consts["f_lo"])


def _short_conv(hy_ref, cw_ref, cb_ref, c0, n):
    x = hy_ref[:, c0:c0 + LANE].astype(F32)
    row = lax.broadcasted_iota(jnp.int32, x.shape, 0)
    prv = jnp.where(row == 0, 0.0, pltpu.roll(x, 1, axis=0))
    nxt = jnp.where(row == n - 1, 0.0, pltpu.roll(x, n - 1, axis=0))
    w = cw_ref[:, c0:c0 + LANE]
    return prv * w[0:1] + x * w[1:2] + nxt * w[2:3] + cb_ref[:, c0:c0 + LANE]


def _hy_mixer_kernel(hy_ref, cw_ref, cb_ref, fp_ref, fq_ref, hp_ref, hq_ref, gp_ref, gq_ref, gn_ref,
                     o_ref, z_ref, y_ref, *, n, nf):
    j = pl.program_id(1)
    nchunk = HY_CH // LANE

    @pl.when(j == 0)
    def _():
        for c in range(nchunk):
            x1 = _short_conv(hy_ref, cw_ref, cb_ref, HY_CH + c * LANE, n)
            v = _short_conv(hy_ref, cw_ref, cb_ref, 2 * HY_CH + c * LANE, n)
            z_ref[:, c * LANE:(c + 1) * LANE] = (v * x1).astype(BF16)
        y_ref[...] = jnp.zeros_like(y_ref)

    z = z_ref[...]
    xp = _dot(fp_ref[...], z)
    xq = _dot(fq_ref[...], z)
    hp = hp_ref[...]
    hq = hq_ref[...]
    row = lax.broadcasted_iota(jnp.int32, xp.shape, 0)
    is0 = (row == 0) & (j == 0)
    qq = xq * hq
    yp = xp * hp - jnp.where(is0, 0.0, qq)
    yq = jnp.where(is0, qq, xp * hq + xq * hp)
    y_ref[...] += _dot(gp_ref[...], yp.astype(BF16)) + _dot(gq_ref[...], yq.astype(BF16))

    @pl.when(j == nf - 1)
    def _():
        ss = jnp.zeros((n, 1), F32)
        for c in range(nchunk):
            x0 = _short_conv(hy_ref, cw_ref, cb_ref, c * LANE, n)
            prod = x0 * y_ref[:, c * LANE:(c + 1) * LANE]
            y_ref[:, c * LANE:(c + 1) * LANE] = prod
            ss = ss + jnp.sum(prod * prod, axis=-1, keepdims=True)
        inv = lax.rsqrt(ss * (1.0 / HY_CH) + EPS)
        o_ref[...] = (y_ref[...] * inv * gn_ref[...]).astype(o_ref.dtype)


def _hy_mixer(h, consts, conv_w, conv_b, hp, hq, gnorm):
    b, n, _ = h.shape
    fc = consts["fc"]
    nf = n // fc
    est = (n * 3 * HY_CH * 2 + 8 * fc * n * 2 + 4 * fc * HY_CH * 4 + n * HY_CH * (2 + 4 + 4)
           + 6 * fc * HY_CH * 4 + 6 * n * LANE * 4)
    return pl.pallas_call(
        functools.partial(_hy_mixer_kernel, n=n, nf=nf),
        grid=(b, nf),
        in_specs=[
            pl.BlockSpec((None, n, 3 * HY_CH), lambda bi, j: (bi, 0, HY_OFF // (3 * HY_CH)),
                         pipeline_mode=pl.Buffered(1)),
            pl.BlockSpec((3, 3 * HY_CH), lambda bi, j: (0, 0)),
            pl.BlockSpec((1, 3 * HY_CH), lambda bi, j: (0, 0)),
            pl.BlockSpec((fc, n), lambda bi, j: (j, 0)),
            pl.BlockSpec((fc, n), lambda bi, j: (nf + j, 0)),
            pl.BlockSpec((fc, HY_CH), lambda bi, j: (j, 0)),
            pl.BlockSpec((fc, HY_CH), lambda bi, j: (j, 0)),
            pl.BlockSpec((n, fc), lambda bi, j: (0, j)),
            pl.BlockSpec((n, fc), lambda bi, j: (0, nf + j)),
            pl.BlockSpec((1, HY_CH), lambda bi, j: (0, 0)),
        ],
        out_specs=pl.BlockSpec((None, n, HY_CH), lambda bi, j: (bi, 0, 0)),
        out_shape=jax.ShapeDtypeStruct((b, n, HY_CH), BF16),
        scratch_shapes=[pltpu.VMEM((n, HY_CH), BF16), pltpu.VMEM((n, HY_CH), F32)],
        compiler_params=pltpu.CompilerParams(
            dimension_semantics=("arbitrary", "arbitrary"),
            vmem_limit_bytes=_vmem_limit(est)),
        name="hyena_mixer",
    )(h, conv_w, conv_b.reshape(1, 3 * HY_CH), consts["f_hi"], consts["f_hi"], hp, hq,
      consts["g"], consts["g"], gnorm.reshape(1, HY_CH))


def _hy_constants(n):
    big = 2 * n
    t = jnp.linspace(0.0, 1.0, n, dtype=F32)[:, None]
    w = 2.0 * math.pi * jnp.arange(n, dtype=F32)[:, None] / n
    f = jnp.linspace(1e-4, HY_BANDS - 1, HY_BANDS, dtype=F32)[None, :]
    zfeat = jnp.concatenate([t, jnp.cos(f * w), -jnp.sin(f * w)], axis=-1)
    zfeat = jnp.pad(zfeat, ((0, 0), (0, LANE - HY_EMB)))
    min_decay = math.log(HY_TARGET) / HY_SLOW_DECAY
    max_decay = math.log(HY_TARGET) / HY_FAST_DECAY
    deltas = jnp.linspace(min_decay, max_decay, HY_CH, dtype=F32)[None, :]

    k = jnp.arange(n, dtype=jnp.int32)
    theta = ((k[:, None] * k[None, :]) % big).astype(F32) * (2.0 * math.pi / big)
    cosm = jnp.cos(theta)
    sinm = jnp.sin(theta)
    alt = jnp.where(k % 2 == 0, 1.0, -1.0).astype(F32)
    fq = jnp.where((k == 0)[:, None], alt[None, :], -sinm)
    fmat = jnp.concatenate([cosm, fq], axis=0)
    f_hi = fmat.astype(BF16)
    f_lo = (fmat - f_hi.astype(F32)).astype(BF16)
    wk = jnp.where(k == 0, 1.0, 2.0).astype(F32)[None, :] / big
    gp = cosm * wk
    gq = jnp.where((k == 0)[None, :], alt[:, None] / big, -sinm * wk)
    g = jnp.concatenate([gp, gq], axis=1).astype(BF16)
    return dict(n=n, fc=min(n, 512), zfeat=zfeat, deltas=deltas, f_hi=f_hi, f_lo=f_lo, g=g)


def _out_proj_kernel(od_ref, og_ref, oh_ref, w_ref, x_ref, g_ref, m_ref, o_ref):
    for r in range(0, o_ref.shape[0], ROW_CHUNK):
        rows = slice(r, r + ROW_CHUNK)
        mix = _dot(od_ref[rows, :], w_ref[0:DIFF_W, :])
        mix = mix + _dot(og_ref[rows, :], w_ref[DIFF_W:DIFF_W + GQA_W, :])
        mix = mix + _dot(oh_ref[rows, :], w_ref[DIFF_W + GQA_W:, :])
        o_ref[rows, :] = x_ref[rows, :] + m_ref[2:3, :] * (_rms(mix) * g_ref[...])


def _out_proj(o_d, o_g, o_h, w, layer, x, g, mods, per_batch):
    b, s, d = x.shape
    tm = min(s, 512)
    est = d * d * 2 + 2 * tm * d * 2 + 4 * tm * d * 4 + 2 * tm * d * 4
    return pl.pallas_call(
        _out_proj_kernel,
        grid=(b, s // tm),
        in_specs=[
            pl.BlockSpec((None, tm, DIFF_W), lambda bi, i: (bi, i, 0)),
            pl.BlockSpec((None, tm, GQA_W), lambda bi, i: (bi, i, 0)),
            pl.BlockSpec((None, tm, HY_CH), lambda bi, i: (bi, i, 0)),
            pl.BlockSpec((None, d, d), lambda bi, i: (layer, 0, 0), pipeline_mode=pl.Buffered(1)),
            pl.BlockSpec((None, tm, d), lambda bi, i: (bi, i, 0)),
            pl.BlockSpec((1, d), lambda bi, i: (0, 0)),
            pl.BlockSpec((None, N_MOD, d), lambda bi, i: (bi * per_batch, 0, 0)),
        ],
        out_specs=pl.BlockSpec((None, tm, d), lambda bi, i: (bi, i, 0)),
        out_shape=jax.ShapeDtypeStruct((b, s, d), F32),
        compiler_params=pltpu.CompilerParams(
            dimension_semantics=("parallel", "parallel"),
            vmem_limit_bytes=_vmem_limit(est)),
        name="out_proj",
    )(o_d, o_g, o_h, w, x, g.reshape(1, d), mods)


def _mlp_kernel(x_ref, g2_ref, g3_ref, m_ref, wu_ref, wd_ref, o_ref, xn_ref, *, nf, ncol):
    f = pl.program_id(2)
    tm, d = o_ref.shape
    row_chunks = [slice(r, r + ROW_CHUNK) for r in range(0, tm, ROW_CHUNK)]

    def hidden(xn):
        return jnp.square(jnp.maximum(_dot(xn, wu_ref[...]), 0.0)).astype(BF16)

    @pl.when(f == 0)
    def _():
        for rows in row_chunks:
            y = _rms(x_ref[rows, :]) * g2_ref[...]
            xn = (y * (1.0 + m_ref[4:5, :]) + m_ref[3:4, :]).astype(BF16)
            xn_ref[rows, :] = xn
            o_ref[rows, :] = _dot(hidden(xn), wd_ref[...])

    @pl.when((f > 0) & (f < nf - 1))
    def _():
        hmid = hidden(xn_ref[...])
        for c in range(d // ncol):
            o_ref[:, c * ncol:(c + 1) * ncol] += _dot(hmid, wd_ref[:, c * ncol:(c + 1) * ncol])

    @pl.when(f == nf - 1)
    def _():
        for rows in row_chunks:
            acc = o_ref[rows, :] + _dot(hidden(xn_ref[rows, :]), wd_ref[...])
            o_ref[rows, :] = x_ref[rows, :] + m_ref[5:6, :] * (_rms(acc) * g3_ref[...])


def _mlp(x, g2, g3, mods, w_up, w_down, layer, per_batch):
    b, s, d = x.shape
    dff = w_up.shape[2]
    tm = min(s, 1024)
    tf = MLP_HIDDEN_COLS
    nf = dff // tf
    est = tm * d * 4 + 2 * tm * d * 4 + tm * d * 2 + 4 * d * tf * 2 + tm * tf * 6 + tm * 512 * 4
    return pl.pallas_call(
        functools.partial(_mlp_kernel, nf=nf, ncol=512),
        grid=(b, s // tm, nf),
        in_specs=[
            pl.BlockSpec((None, tm, d), lambda bi, i, f: (bi, i, 0), pipeline_mode=pl.Buffered(1)),
            pl.BlockSpec((1, d), lambda bi, i, f: (0, 0)),
            pl.BlockSpec((1, d), lambda bi, i, f: (0, 0)),
            pl.BlockSpec((None, N_MOD, d), lambda bi, i, f: (bi * per_batch, 0, 0)),
            pl.BlockSpec((None, d, tf), lambda bi, i, f: (layer, 0, f)),
            pl.BlockSpec((None, tf, d), lambda bi, i, f: (layer, f, 0)),
        ],
        out_specs=pl.BlockSpec((None, tm, d), lambda bi, i, f: (bi, i, 0)),
        out_shape=jax.ShapeDtypeStruct((b, s, d), F32),
        scratch_shapes=[pltpu.VMEM((tm, d), BF16)],
        compiler_params=pltpu.CompilerParams(
            dimension_semantics=("parallel", "parallel", "arbitrary"),
            vmem_limit_bytes=_vmem_limit(est)),
        name="mlp",
    )(x, g2.reshape(1, d), g3.reshape(1, d), mods, w_up, w_down)


def _rope_tables(n_rows, head_dim):
    t_row = jnp.repeat(jnp.arange(n_rows, dtype=F32), GRID_W)
    t_col = jnp.tile(jnp.arange(GRID_W, dtype=F32), n_rows)
    d_axis = head_dim // 2
    inv = ROPE_THETA ** (-jnp.arange(0, d_axis, 2, dtype=F32) / d_axis)
    ang = jnp.concatenate([t_row[:, None] * inv, t_col[:, None] * inv], axis=-1)
    cos = jnp.repeat(jnp.cos(ang), 2, axis=-1)
    sin = jnp.repeat(jnp.sin(ang), 2, axis=-1)
    sign = jnp.where(jnp.arange(head_dim) % 2 == 0, -1.0, 1.0).astype(F32)
    sin = sin * sign
    reps = LANE // head_dim
    return jnp.tile(cos, (1, reps)), jnp.tile(sin, (1, reps))


def kernel(x, c, ctx, c_ctx, w_mod, b_mod, g_norm, w_in, w_out, diff_lam, diff_subln, gqa_q_norm,
           gqa_k_norm, gqa_out_norm, hy_conv_w, hy_conv_b, hy_w1, hy_b1, hy_w2, hy_b2, hy_w3, hy_b3,
           hy_wout, hy_freq, hy_bias, hy_out_norm, w_up, w_down):
    bsz, n_lat, d = x.shape
    n_ctx = ctx.shape[1]
    depth = w_mod.shape[0]
    assert d == D_MODEL and bsz + 1 <= MOD_ROWS
    assert n_lat % GRID_W == 0 and n_lat % 256 == 0 and n_ctx % 128 == 0

    tab_d = _rope_tables(n_lat // GRID_W, DIFF_QK)
    tab_g = _rope_tables(n_lat // GRID_W, GQA_HD)
    hyc_lat = _hy_constants(n_lat)
    hyc_ctx = _hy_constants(n_ctx)

    cpad = jnp.zeros((MOD_ROWS, d), F32).at[:bsz].set(c).at[bsz].set(c_ctx)
    mods = _modulations(cpad, w_mod, b_mod)
    lam_pad = jnp.pad(diff_lam, ((0, 0), (0, 4), (0, LANE - DIFF_QK)))

    w_in_b, w_out_b, w_up_b, w_down_b = (w.astype(BF16) for w in (w_in, w_out, w_up, w_down))
    tq_lat = min(n_lat, 512)
    tq_ctx = min(n_ctx, 256)
    xc = ctx
    for l in range(depth):
        last = l == depth - 1
        lam_init = 0.8 - 0.6 * math.exp(-0.3 * l)
        m_lat = mods[l, :bsz].reshape(bsz, N_MOD, d)
        m_ctx = mods[l, bsz:bsz + 1].reshape(1, N_MOD, d)
        filt = (hy_w1[l], hy_b1[l], hy_w2[l], hy_b2[l], hy_w3[l], hy_b3[l], hy_wout[l], hy_freq[l], hy_bias[l])

        h = _in_proj(x, g_norm[l, 0], m_lat, w_in_b, l, 0, N_IN, 1)
        xc_flat = xc.reshape(1, bsz * n_ctx, d)
        if last:
            hc = _in_proj(xc_flat, g_norm[l, 0], m_ctx, w_in_b, l, KV_OFF, KV_W, 0)
        else:
            hc = _in_proj(xc_flat, g_norm[l, 0], m_ctx, w_in_b, l, 0, N_IN, 0)
        hc = hc.reshape(bsz, n_ctx, -1)
        c_blk = 0 if last else KV_BLK

        o_d = _diff_attention(lam_pad[l], h, h, KV_BLK, hc, c_blk, tab_d, diff_subln[l], lam_init, tq_lat)
        o_g = _gqa_attention(h, h, KV_BLK, hc, c_blk, tab_g, gqa_q_norm[l], gqa_k_norm[l], gqa_out_norm[l],
                             tq_lat, 4)
        hp, hq = _hy_filters(hyc_lat, *filt)
        o_h = _hy_mixer(h, hyc_lat, hy_conv_w[l], hy_conv_b[l], hp, hq, hy_out_norm[l])
        x_new = _out_proj(o_d, o_g, o_h, w_out_b, l, x, g_norm[l, 1], m_lat, 1)

        if not last:
            oc_d = _diff_attention(lam_pad[l], hc, hc, KV_BLK, None, 0, None, diff_subln[l], lam_init, tq_ctx)
            oc_g = _gqa_attention(hc, hc, KV_BLK, None, 0, None, gqa_q_norm[l], gqa_k_norm[l],
                                  gqa_out_norm[l], tq_ctx, 4)
            hpc, hqc = _hy_filters(hyc_ctx, *filt)
            oc_h = _hy_mixer(hc, hyc_ctx, hy_conv_w[l], hy_conv_b[l], hpc, hqc, hy_out_norm[l])
            flat = lambda a: a.reshape(1, bsz * n_ctx, a.shape[-1])
            xc_flat = _out_proj(flat(oc_d), flat(oc_g), flat(oc_h), w_out_b, l, xc_flat, g_norm[l, 1],
                                m_ctx, 0)
        x = x_new

        x = _mlp(x, g_norm[l, 2], g_norm[l, 3], m_lat, w_up_b, w_down_b, l, 1)
        if not last:
            xc_flat = _mlp(xc_flat, g_norm[l, 2], g_norm[l, 3], m_ctx, w_up_b, w_down_b, l, 0)
            xc = xc_flat.reshape(bsz, n_ctx, d)
    return x
```

```python
import functools
import math

import jax
import jax.numpy as jnp
from jax import lax
from jax.experimental import pallas as pl
from jax.experimental.pallas import tpu as pltpu

F32 = jnp.float32
BF16 = jnp.bfloat16

D_MODEL = 2048
GRID_W = 64
DIFF_W = 512
GQA_W = 1024
HY_CH = 512
DIFF_V = 128
DIFF_QK = 64
DIFF_HEADS = 4
GQA_HD = 128
GQA_KV_HEADS = 2
GQA_GROUP = 4
HY_EMB = 33
HY_BANDS = 16
HY_FFN = 64
HY_FAST_DECAY = 0.3
HY_SLOW_DECAY = 1.5
HY_TARGET = 1e-2
HY_SHIFT = 0.0
D_FF = 4 * D_MODEL
N_MOD = 6
ROPE_THETA = 10000.0
EPS = 1e-6

LANE = 128
DQ_BLK = 0
GQ_BLK = 4
HY_OFF = 1536
KV_OFF = 3072
KV_BLK = KV_OFF // LANE
DK_REL, DV_REL, GK_REL, GV_REL = 0, 4, 8, 10
N_IN = 4608
KV_W = N_IN - KV_OFF

VMEM_CAP = 64 * 1024 * 1024
MOD_ROWS = 24
PREP_ROWS = 256
LOG2E = 1.4426950408889634
IN_PROJ_COLS = 1536
MLP_HIDDEN_COLS = 1024
DFT_SPLIT = 64
HY_HALF = 256
KEY_CHUNK = 256
ROW_CHUNK = 256
DIFF_HEADS_PER_STEP = 2


def _vmem_limit(nbytes):
    return int(min(max(nbytes * 5 // 4 + (4 << 20), 32 << 20), VMEM_CAP - (6 << 20)))


def _rms(x):
    return x * lax.rsqrt(jnp.mean(x * x, axis=-1, keepdims=True) + EPS)


def _dot(a, b):
    return jnp.dot(a, b, preferred_element_type=F32)


def _dot_nt(a, b):
    return lax.dot_general(a, b, (((1,), (1,)), ((), ())), preferred_element_type=F32)


def _split_bf16(a):
    hi = a.astype(BF16)
    lo = (a - hi.astype(F32)).astype(BF16)
    return hi, lo


def _dot3(a, b):
    ah, al = _split_bf16(a)
    bh, bl = _split_bf16(b)
    return _dot(ah, bh) + (_dot(al, bh) + _dot(ah, bl))


def _rope(x, cos, sin_signed):
    lane = lax.broadcasted_iota(jnp.int32, x.shape, 1)
    nxt = pltpu.roll(x, LANE - 1, axis=1)
    prv = pltpu.roll(x, 1, axis=1)
    swapped = jnp.where(lane % 2 == 0, nxt, prv)
    return x * cos + swapped * sin_signed


def _mod_kernel(c_ref, w_ref, b_ref, o_ref):
    c = c_ref[...]
    s = c * (1.0 / (1.0 + jnp.exp(-c)))
    o_ref[0] = _dot(s.astype(BF16), w_ref[0].astype(BF16)) + b_ref[0]


def _modulations(cpad, w_mod, b_mod):
    nl, d, n = w_mod.shape
    tn = 1024
    return pl.pallas_call(
        _mod_kernel,
        grid=(nl, n // tn),
        in_specs=[
            pl.BlockSpec((MOD_ROWS, d), lambda l, j: (0, 0)),
            pl.BlockSpec((1, d, tn), lambda l, j: (l, 0, j)),
            pl.BlockSpec((1, 1, tn), lambda l, j: (l, 0, j)),
        ],
        out_specs=pl.BlockSpec((1, MOD_ROWS, tn), lambda l, j: (l, 0, j)),
        out_shape=jax.ShapeDtypeStruct((nl, MOD_ROWS, n), F32),
        compiler_params=pltpu.CompilerParams(
            dimension_semantics=("arbitrary", "arbitrary"),
            vmem_limit_bytes=_vmem_limit(2 * d * tn * 4 + d * tn * 2)),
        name="modulations",
    )(cpad, w_mod, b_mod.reshape(nl, 1, n))


def _in_proj_kernel(x_ref, g_ref, m_ref, w_ref, o_ref, xn_ref):
    j = pl.program_id(2)

    @pl.when(j == 0)
    def _():
        for r in range(0, o_ref.shape[0], ROW_CHUNK):
            rows = slice(r, r + ROW_CHUNK)
            y = _rms(x_ref[rows, :]) * g_ref[...]
            xn = (y * (1.0 + m_ref[1:2, :]) + m_ref[0:1, :]).astype(BF16)
            xn_ref[rows, :] = xn
            o_ref[rows, :] = _dot(xn, w_ref[...]).astype(o_ref.dtype)

    @pl.when(j > 0)
    def _():
        o_ref[...] = _dot(xn_ref[...], w_ref[...]).astype(o_ref.dtype)


def _in_proj(x, g, mods, w, layer, col0, n, per_batch):
    b, s, d = x.shape
    tm = min(s, 1024)
    assert s % tm == 0 and tm % ROW_CHUNK == 0
    tn = IN_PROJ_COLS
    assert n % tn == 0 and col0 % tn == 0
    jb = col0 // tn
    est = 2 * tm * d * 4 + tm * d * 2 + 2 * d * tn * 2 + 2 * tm * tn * 2 + tm * tn * 4
    return pl.pallas_call(
        _in_proj_kernel,
        grid=(b, s // tm, n // tn),
        in_specs=[
            pl.BlockSpec((None, tm, d), lambda bi, i, j: (bi, i, 0)),
            pl.BlockSpec((1, d), lambda bi, i, j: (0, 0)),
            pl.BlockSpec((None, N_MOD, d), lambda bi, i, j: (bi * per_batch, 0, 0)),
            pl.BlockSpec((None, d, tn), lambda bi, i, j: (layer, 0, jb + j)),
        ],
        out_specs=pl.BlockSpec((None, tm, tn), lambda bi, i, j: (bi, i, j)),
        out_shape=jax.ShapeDtypeStruct((b, s, n), BF16),
        scratch_shapes=[pltpu.VMEM((tm, d), BF16)],
        compiler_params=pltpu.CompilerParams(
            dimension_semantics=("parallel", "parallel", "arbitrary"),
            vmem_limit_bytes=_vmem_limit(est)),
        name="in_proj",
    )(x, g.reshape(1, d), mods, w)


def _attend_streams(streams, s_ref, p_ref, finish):
    lk = streams[0][1].shape[0]
    ck = KEY_CHUNK if lk % KEY_CHUNK == 0 else LANE
    chunks = [slice(r, r + ck) for r in range(0, lk, ck)]
    n = len(streams)
    qts = [make_qt() for make_qt, _, _ in streams]
    mx = None
    prev = None
    done = None
    for i in range(n + 1):
        lsum = acc = None
        for ci, rows in enumerate(chunks):
            if i < n:
                sj = _dot(streams[i][1][rows, :], qts[i])
                s_ref[i % 2, rows, :] = sj
                mj = jnp.max(sj, axis=0, keepdims=True)
                mx = mj if ci == 0 else jnp.maximum(mx, mj)
            if ci == 0 and done is not None:
                finish(*done)
                done = None
            if prev is not None:
                p = jnp.exp2(s_ref[prev[0] % 2, rows, :] - prev[1])
                lj = jnp.sum(p, axis=0, keepdims=True)
                p_ref[rows, :] = p.astype(BF16)
                oj = _dot(streams[prev[0]][2][:, rows], p_ref[rows, :])
                lsum = lj if lsum is None else lsum + lj
                acc = oj if acc is None else acc + oj
        if prev is not None:
            done = (prev[0], acc, lsum)
        prev = (i, mx) if i < n else None
    finish(*done)


def _gqa_kernel(*refs, rope, prefix, heads_per_step):
    it = iter(refs)
    q_ref, k_ref, v_ref = next(it), next(it), next(it)
    kc_ref = vc_ref = None
    if prefix:
        kc_ref, vc_ref = next(it), next(it)
    cq_ref = sq_ref = ck_ref = sk_ref = None
    if rope:
        cq_ref, sq_ref, ck_ref, sk_ref = next(it), next(it), next(it), next(it)
    gq_ref, gk_ref, go_ref = next(it), next(it), next(it)
    o_ref = next(it)
    kall_ref, vt_ref, s_ref, p_ref = next(it), next(it), next(it), next(it)

    c = kc_ref.shape[0] if prefix else 0
    s = k_ref.shape[0]

    @pl.when((pl.program_id(2) == 0) & (pl.program_id(3) == 0))
    def _():
        if prefix:
            kall_ref[0:c, :] = (_rms(kc_ref[...].astype(F32)) * gk_ref[...]).astype(BF16)
            vt_ref[:, 0:c] = vc_ref[...].astype(F32).T.astype(BF16)
        pr = min(s, PREP_ROWS)
        for r0 in range(0, s, pr):
            rows = slice(r0, r0 + pr)
            k = _rms(k_ref[rows, :].astype(F32)) * gk_ref[...]
            if rope:
                k = _rope(k, ck_ref[rows, :], sk_ref[rows, :])
            kall_ref[c + r0:c + r0 + pr, :] = k.astype(BF16)
            vt_ref[:, c + r0:c + r0 + pr] = v_ref[rows, :].astype(F32).T.astype(BF16)

    def query_t(hh):
        q = _rms(q_ref[:, hh * LANE:(hh + 1) * LANE].astype(F32)) * gq_ref[...]
        if rope:
            q = _rope(q, cq_ref[...], sq_ref[...])
        return (q * (GQA_HD ** -0.5 * LOG2E)).T.astype(BF16)

    def finish(hh, acc, denom):
        ot = acc / denom
        ot = ot * lax.rsqrt(jnp.mean(ot * ot, axis=0, keepdims=True) + EPS) * go_ref[...]
        o_ref[:, hh * LANE:(hh + 1) * LANE] = ot.T.astype(o_ref.dtype)

    streams = [(functools.partial(query_t, hh), kall_ref, vt_ref) for hh in range(heads_per_step)]
    _attend_streams(streams, s_ref, p_ref, finish)


def _gqa_attention(hq, hk, kv_blk, hc, c_blk, tables, gq, gk, go, tq, heads_per_step):
    b, s, _ = hq.shape
    prefix = hc is not None
    rope = tables is not None
    hps = heads_per_step
    grid = (b, GQA_KV_HEADS, GQA_GROUP // hps, s // tq)
    in_specs = [
        pl.BlockSpec((None, tq, hps * LANE), lambda bi, g, r, i: (bi, i, (GQ_BLK + g * GQA_GROUP) // hps + r)),
        pl.BlockSpec((None, s, LANE), lambda bi, g, r, i: (bi, 0, kv_blk + GK_REL + g)),
        pl.BlockSpec((None, s, LANE), lambda bi, g, r, i: (bi, 0, kv_blk + GV_REL + g)),
    ]
    args = [hq, hk, hk]
    c = 0
    if prefix:
        c = hc.shape[1]
        in_specs += [
            pl.BlockSpec((None, c, LANE), lambda bi, g, r, i: (bi, 0, c_blk + GK_REL + g)),
            pl.BlockSpec((None, c, LANE), lambda bi, g, r, i: (bi, 0, c_blk + GV_REL + g)),
        ]
        args += [hc, hc]
    if rope:
        cos, sin = tables
        in_specs += [
            pl.BlockSpec((tq, LANE), lambda bi, g, r, i: (i, 0)),
            pl.BlockSpec((tq, LANE), lambda bi, g, r, i: (i, 0)),
            pl.BlockSpec((s, LANE), lambda bi, g, r, i: (0, 0)),
            pl.BlockSpec((s, LANE), lambda bi, g, r, i: (0, 0)),
        ]
        args += [cos, sin, cos, sin]
    vec = pl.BlockSpec((1, LANE), lambda bi, g, r, i: (0, 0))
    in_specs += [vec, vec, pl.BlockSpec((LANE, 1), lambda bi, g, r, i: (0, 0))]
    args += [gq.reshape(1, LANE), gk.reshape(1, LANE), go.reshape(LANE, 1)]
    lk = s + c
    est = 4 * s * LANE * 2 + 4 * s * LANE * 4 + 2 * lk * LANE * 2 + hps * 3 * tq * lk * 4
    return pl.pallas_call(
        functools.partial(_gqa_kernel, rope=rope, prefix=prefix, heads_per_step=hps),
        grid=grid,
        in_specs=in_specs,
        out_specs=pl.BlockSpec((None, tq, hps * LANE), lambda bi, g, r, i: (bi, i, g * GQA_GROUP // hps + r)),
        out_shape=jax.ShapeDtypeStruct((b, s, GQA_W), BF16),
        scratch_shapes=[pltpu.VMEM((lk, LANE), BF16), pltpu.VMEM((LANE, lk), BF16),
                        pltpu.VMEM((2, lk, tq), F32), pltpu.VMEM((lk, tq), BF16)],
        compiler_params=pltpu.CompilerParams(
            dimension_semantics=("arbitrary",) * 4,
            vmem_limit_bytes=_vmem_limit(est)),
        name="gqa_attention",
    )(*args)


def _diff_kernel(*refs, rope, prefix, lam_init):
    it = iter(refs)
    lp_ref, q_ref, k_ref, v_ref = next(it), next(it), next(it), next(it)
    kc_ref = vc_ref = None
    if prefix:
        kc_ref, vc_ref = next(it), next(it)
    cq_ref = sq_ref = ck_ref = sk_ref = None
    if rope:
        cq_ref, sq_ref, ck_ref, sk_ref = next(it), next(it), next(it), next(it)
    gs_ref = next(it)
    o_ref = next(it)
    kall_ref, vt_ref, s_ref, p_ref, lam_ref = next(it), next(it), next(it), next(it), next(it)
    c = kc_ref.shape[0] if prefix else 0
    s = k_ref.shape[0]

    first = (pl.program_id(0) == 0) & (pl.program_id(1) == 0) & (pl.program_id(2) == 0)

    @pl.when(first)
    def _():
        p = lp_ref[...]
        t1 = jnp.sum(p[0:1] * p[1:2], axis=-1, keepdims=True)
        t2 = jnp.sum(p[2:3] * p[3:4], axis=-1, keepdims=True)
        lam_ref[...] = jnp.broadcast_to(jnp.exp(t1) - jnp.exp(t2) + lam_init, lam_ref.shape)

    @pl.when(pl.program_id(2) == 0)
    def _():
        for hh in range(DIFF_HEADS_PER_STEP):
            cols = slice(hh * LANE, (hh + 1) * LANE)
            if prefix:
                kall_ref[hh, 0:c, :] = kc_ref[:, cols]
                vt_ref[hh, :, 0:c] = vc_ref[:, cols].astype(F32).T.astype(BF16)
            pr = min(s, PREP_ROWS)
            for r0 in range(0, s, pr):
                rows = slice(r0, r0 + pr)
                if rope:
                    k = _rope(k_ref[rows, cols].astype(F32), ck_ref[rows, :], sk_ref[rows, :]).astype(BF16)
                else:
                    k = k_ref[rows, cols]
                kall_ref[hh, c + r0:c + r0 + pr, :] = k
                vt_ref[hh, :, c + r0:c + r0 + pr] = v_ref[rows, cols].astype(F32).T.astype(BF16)

    lam = lam_ref[0:1, 0:1]

    head_qt = {}

    def query_t(hh, mp):
        if hh not in head_qt:
            q = q_ref[:, hh * LANE:(hh + 1) * LANE].astype(F32)
            if rope:
                q = _rope(q, cq_ref[...], sq_ref[...])
            head_qt[hh] = (q * (DIFF_QK ** -0.5 * LOG2E)).T
        qt = head_qt[hh]
        row = lax.broadcasted_iota(jnp.int32, qt.shape, 0)
        keep = (row < DIFF_QK) if mp == 0 else (row >= DIFF_QK)
        return jnp.where(keep, qt, 0.0).astype(BF16)

    first_map = {}

    def finish(i, acc, denom):
        hh, mp = divmod(i, 2)
        if mp == 0:
            first_map[hh] = acc / denom
            return
        ot = first_map.pop(hh) - acc * (lam / denom)
        ot = ot * lax.rsqrt(jnp.mean(ot * ot, axis=0, keepdims=True) + EPS) * (gs_ref[...] * (1.0 - lam_init))
        o_ref[:, hh * LANE:(hh + 1) * LANE] = ot.T.astype(o_ref.dtype)

    streams = [(functools.partial(query_t, hh, mp), kall_ref.at[hh], vt_ref.at[hh])
               for hh in range(DIFF_HEADS_PER_STEP) for mp in range(2)]
    _attend_streams(streams, s_ref, p_ref, finish)


def _diff_attention(lam_params, hq, hk, kv_blk, hc, c_blk, tables, gsub, lam_init, tq):
    b, s, _ = hq.shape
    prefix = hc is not None
    rope = tables is not None
    hps = DIFF_HEADS_PER_STEP
    w = hps * LANE
    grid = (b, DIFF_HEADS // hps, s // tq)
    in_specs = [
        pl.BlockSpec((8, LANE), lambda bi, h, i: (0, 0)),
        pl.BlockSpec((None, tq, w), lambda bi, h, i: (bi, i, DQ_BLK // hps + h)),
        pl.BlockSpec((None, s, w), lambda bi, h, i: (bi, 0, (kv_blk + DK_REL) // hps + h)),
        pl.BlockSpec((None, s, w), lambda bi, h, i: (bi, 0, (kv_blk + DV_REL) // hps + h)),
    ]
    args = [lam_params, hq, hk, hk]
    c = 0
    if prefix:
        c = hc.shape[1]
        in_specs += [
            pl.BlockSpec((None, c, w), lambda bi, h, i: (bi, 0, (c_blk + DK_REL) // hps + h)),
            pl.BlockSpec((None, c, w), lambda bi, h, i: (bi, 0, (c_blk + DV_REL) // hps + h)),
        ]
        args += [hc, hc]
    if rope:
        cos, sin = tables
        in_specs += [
            pl.BlockSpec((tq, LANE), lambda bi, h, i: (i, 0)),
            pl.BlockSpec((tq, LANE), lambda bi, h, i: (i, 0)),
            pl.BlockSpec((s, LANE), lambda bi, h, i: (0, 0)),
            pl.BlockSpec((s, LANE), lambda bi, h, i: (0, 0)),
        ]
        args += [cos, sin, cos, sin]
    in_specs.append(pl.BlockSpec((LANE, 1), lambda bi, h, i: (0, 0)))
    args.append(gsub.reshape(LANE, 1))
    lk = s + c
    est = 8 * s * w * 2 + 4 * s * LANE * 4 + 4 * lk * w * 2 + 5 * tq * lk * 4
    return pl.pallas_call(
        functools.partial(_diff_kernel, rope=rope, prefix=prefix, lam_init=lam_init),
        grid=grid,
        in_specs=in_specs,
        out_specs=pl.BlockSpec((None, tq, w), lambda bi, h, i: (bi, i, h)),
        out_shape=jax.ShapeDtypeStruct((b, s, DIFF_W), BF16),
        scratch_shapes=[pltpu.VMEM((hps, lk, LANE), BF16), pltpu.VMEM((hps, LANE, lk), BF16),
                        pltpu.VMEM((2, lk, tq), F32), pltpu.VMEM((lk, tq), BF16), pltpu.VMEM((8, LANE), F32)],
        compiler_params=pltpu.CompilerParams(
            dimension_semantics=("arbitrary",) * 3,
            vmem_limit_bytes=_vmem_limit(est)),
        name="diff_attention",
    )(*args)


def _hy_filter_kernel(z_ref, w1_ref, b1_ref, w2_ref, b2_ref, w3_ref, b3_ref, wo_ref, fr_ref,
                      bias_ref, dl_ref, fp_hi_ref, fp_lo_ref, fq_hi_ref, fq_lo_ref,
                      hp_ref, hq_ref, hs_hi, hs_lo, hd_hi, hd_lo, alt_ref):
    j = pl.program_id(0)

    @pl.when(j == 0)
    def _():
        z = z_ref[...]
        h = jnp.sin(fr_ref[0:1, :] * (_dot3(z, w1_ref[...]) + b1_ref[...]))
        h = jnp.sin(fr_ref[1:2, :] * (_dot3(h, w2_ref[...]) + b2_ref[...]))
        h = jnp.sin(fr_ref[2:3, :] * (_dot3(h, w3_ref[...]) + b3_ref[...]))
        h = _dot3(h, wo_ref[...])
        decay = jnp.exp(-z[:, 0:1] * jnp.abs(dl_ref[...])) + HY_SHIFT
        hf = h[:, :HY_CH] * decay
        hb = h[:, HY_CH:] * decay
        row = lax.broadcasted_iota(jnp.int32, hf.shape, 0)
        hf = hf + jnp.where(row == 0, bias_ref[...], 0.0)
        hb = jnp.where(row == 0, 0.0, hb)
        hs = hf + hb
        hd = hf - hb
        sign = jnp.where(row % 2 == 0, 1.0, -1.0)
        alt_ref[...] = jnp.broadcast_to(jnp.sum(sign * hs, axis=0, keepdims=True), alt_ref.shape)
        a, bb = _split_bf16(hs)
        hs_hi[...] = a
        hs_lo[...] = bb
        a, bb = _split_bf16(hd)
        hd_hi[...] = a
        hd_lo[...] = bb

    xp = _dot(fp_hi_ref[...], hs_hi[...]) + (_dot(fp_lo_ref[...], hs_hi[...]) + _dot(fp_hi_ref[...], hs_lo[...]))
    xq = _dot(fq_hi_ref[...], hd_hi[...]) + (_dot(fq_lo_ref[...], hd_hi[...]) + _dot(fq_hi_ref[...], hd_lo[...]))
    row = lax.broadcasted_iota(jnp.int32, xq.shape, 0)
    is0 = (row == 0) & (j == 0)
    hp_ref[...] = xp
    hq_ref[...] = jnp.where(is0, alt_ref[0:1, :], xq)


def _hy_filters(consts, w1, b1, w2, b2, w3, b3, wout, freq, bias):
    n = consts["n"]
    fc = consts["fc"]
    nf = n // fc
    pad = LANE - HY_FFN
    w1p = jnp.pad(w1, ((0, LANE - HY_EMB), (0, pad)))
    w2p = jnp.pad(w2, ((0, pad), (0, pad)))
    w3p = jnp.pad(w3, ((0, pad), (0, pad)))
    wop = jnp.pad(wout, ((0, pad), (0, 0)))
    b1p = jnp.pad(b1, (0, pad)).reshape(1, LANE)
    b2p = jnp.pad(b2, (0, pad)).reshape(1, LANE)
    b3p = jnp.pad(b3, (0, pad)).reshape(1, LANE)
    frp = jnp.pad(freq, ((0, 0), (0, pad)))
    full = lambda shape: pl.BlockSpec(shape, lambda j: (0,) * len(shape))
    fspec_p = pl.BlockSpec((fc, n), lambda j: (j, 0))
    fspec_q = pl.BlockSpec((fc, n), lambda j: (nf + j, 0))
    est = n * LANE * 4 * 2 + 8 * fc * n * 2 + 4 * n * HY_CH * 2 + 4 * fc * HY_CH * 4 + 8 * n * HY_CH * 4
    return pl.pallas_call(
        _hy_filter_kernel,
        grid=(nf,),
        in_specs=[
            full((n, LANE)), full((LANE, LANE)), full((1, LANE)), full((LANE, LANE)), full((1, LANE)),
            full((LANE, LANE)), full((1, LANE)), full((LANE, 2 * HY_CH)), full((3, LANE)),
            full((1, HY_CH)), full((1, HY_CH)),
            fspec_p, fspec_p, fspec_q, fspec_q,
        ],
        out_specs=[pl.BlockSpec((fc, HY_CH), lambda j: (j, 0)), pl.BlockSpec((fc, HY_CH), lambda j: (j, 0))],
        out_shape=[jax.ShapeDtypeStruct((n, HY_CH), F32), jax.ShapeDtypeStruct((n, HY_CH), F32)],
        scratch_shapes=[pltpu.VMEM((n, HY_CH), BF16)] * 4 + [pltpu.VMEM((8, HY_CH), F32)],
        compiler_params=pltpu.CompilerParams(
            dimension_semantics=("arbitrary",),
            vmem_limit_bytes=_vmem_limit(est)),
        name="hyena_filters",
    )(consts["zfeat"], w1p, b1p, w2p, b2p, w3p, b3p, wop, frp, bias.reshape(1, HY_CH), consts["deltas"],
      consts["f_hi"], consts["f_lo"], consts["f_hi"], consts["f_lo"])


def _short_conv(hy_ref, cw_ref, cb_ref, c0, n):
    x = hy_ref[:, c0:c0 + LANE].astype(F32)
    row = lax.broadcasted_iota(jnp.int32, x.shape, 0)
    prv = jnp.where(row == 0, 0.0, pltpu.roll(x, 1, axis=0))
    nxt = jnp.where(row == n - 1, 0.0, pltpu.roll(x, n - 1, axis=0))
    w = cw_ref[:, c0:c0 + LANE]
    return prv * w[0:1] + x * w[1:2] + nxt * w[2:3] + cb_ref[:, c0:c0 + LANE]


def _hy_mixer_kernel(hy_ref, cw_ref, cb_ref, fp_ref, fq_ref, hp_ref, hq_ref, gp_ref, gq_ref, gn_ref,
                     o_ref, z_ref, y_ref, x0_ref, *, n, nf):
    j = pl.program_id(1)

    fc = fp_ref.shape[0]
    freq_halves = [slice(0, fc // 2), slice(fc // 2, fc)]
    all_cols = slice(0, HY_CH)
    row_chunks = [slice(r, r + ROW_CHUNK) for r in range(0, n, ROW_CHUNK)] if n > ROW_CHUNK else [slice(0, n)]

    def spectrum(cols, z, freqs, nyquist_row):
        xp = _dot(fp_ref[freqs, :], z)
        xq = _dot(fq_ref[freqs, :], z)
        hp = hp_ref[freqs, cols]
        hq = hq_ref[freqs, cols]
        qq = xq * hq
        if nyquist_row:
            is0 = lax.broadcasted_iota(jnp.int32, xp.shape, 0) == 0
            yp = xp * hp - jnp.where(is0, 0.0, qq)
            yq = jnp.where(is0, qq, xp * hq + xq * hp)
        else:
            yp = xp * hp - qq
            yq = xp * hq + xq * hp
        return yp.astype(BF16), yq.astype(BF16)

    def spectrum_with(cols, z, first_step, fillers):
        parts = []
        for i, freqs in enumerate(freq_halves):
            parts.append(spectrum(cols, z, freqs, first_step and i == 0))
            if fillers:
                fillers.pop(0)()
        return (jnp.concatenate([p[0] for p in parts], axis=0), jnp.concatenate([p[1] for p in parts], axis=0))

    def conv_z(c0):
        x1 = _short_conv(hy_ref, cw_ref, cb_ref, HY_CH + c0, n)
        v = _short_conv(hy_ref, cw_ref, cb_ref, 2 * HY_CH + c0, n)
        zc = (v * x1).astype(BF16)
        z_ref[:, c0:c0 + LANE] = zc
        return zc

    def conv_x0(c0):
        x0_ref[:, c0:c0 + LANE] = _short_conv(hy_ref, cw_ref, cb_ref, c0, n)

    def normed_rows(rows, y):
        prod = x0_ref[rows, :] * y
        inv = lax.rsqrt(jnp.mean(prod * prod, axis=-1, keepdims=True) + EPS)
        o_ref[rows, :] = (prod * inv * gn_ref[...]).astype(o_ref.dtype)

    @pl.when(j == 0)
    def _():
        lanes = list(range(0, HY_CH, LANE))
        zc = {c0: conv_z(c0) for c0 in lanes[:HY_HALF // LANE]}
        pending = lanes[HY_HALF // LANE:]
        fillers = [functools.partial(lambda c0: zc.__setitem__(c0, conv_z(c0)), c0) for c0 in pending]
        if nf <= 2:
            fillers += [functools.partial(conv_x0, c0) for c0 in lanes]
        for h0 in range(0, HY_CH, HY_HALF):
            cols = slice(h0, h0 + HY_HALF)
            z = jnp.concatenate([zc[c0] for c0 in range(h0, h0 + HY_HALF, LANE)], axis=1)
            yp, yq = spectrum_with(cols, z, True, fillers)
            for rows in row_chunks:
                y_ref[rows, cols] = _dot(gp_ref[rows, :], yp) + _dot(gq_ref[rows, :], yq)
                if h0 > 0 and fillers:
                    fillers.pop(0)()
        while fillers:
            fillers.pop(0)()
        if nf == 1:
            for rows in row_chunks:
                normed_rows(rows, y_ref[rows, :])

    def middle(with_x0):
        fillers = [functools.partial(conv_x0, c0) for c0 in range(0, HY_CH, LANE)] if with_x0 else []
        yp, yq = spectrum_with(all_cols, z_ref[...], False, fillers)
        half = max(len(row_chunks) // 2, 1)
        for i in range(0, len(row_chunks), half):
            rows = slice(row_chunks[i].start, row_chunks[min(i + half, len(row_chunks)) - 1].stop)
            y_ref[rows, :] += _dot(gp_ref[rows, :], yp) + _dot(gq_ref[rows, :], yq)
            if fillers:
                fillers.pop(0)()
        while fillers:
            fillers.pop(0)()

    if nf > 2:
        pl.when(j == 1)(functools.partial(middle, True))
    if nf > 3:
        pl.when((j > 1) & (j < nf - 1))(functools.partial(middle, False))

    if nf > 1:
        @pl.when(j == nf - 1)
        def _():
            yp, yq = spectrum_with(all_cols, z_ref[...], False, [])
            for rows in row_chunks:
                normed_rows(rows, y_ref[rows, :] + _dot(gp_ref[rows, :], yp) + _dot(gq_ref[rows, :], yq))


def _hy_mixer(h, consts, conv_w, conv_b, hp, hq, gnorm):
    b, n, _ = h.shape
    fc = consts["fc"]
    nf = n // fc
    est = (n * 3 * HY_CH * 2 + 8 * fc * n * 2 + 4 * fc * HY_CH * 4 + n * HY_CH * (2 + 4 + 4)
           + 6 * fc * HY_CH * 4 + 6 * n * LANE * 4)
    return pl.pallas_call(
        functools.partial(_hy_mixer_kernel, n=n, nf=nf),
        grid=(b, nf),
        in_specs=[
            pl.BlockSpec((None, n, 3 * HY_CH), lambda bi, j: (bi, 0, HY_OFF // (3 * HY_CH)),
                         pipeline_mode=pl.Buffered(1)),
            pl.BlockSpec((3, 3 * HY_CH), lambda bi, j: (0, 0)),
            pl.BlockSpec((1, 3 * HY_CH), lambda bi, j: (0, 0)),
            pl.BlockSpec((fc, n), lambda bi, j: (j, 0)),
            pl.BlockSpec((fc, n), lambda bi, j: (nf + j, 0)),
            pl.BlockSpec((fc, HY_CH), lambda bi, j: (j, 0)),
            pl.BlockSpec((fc, HY_CH), lambda bi, j: (j, 0)),
            pl.BlockSpec((n, fc), lambda bi, j: (0, j)),
            pl.BlockSpec((n, fc), lambda bi, j: (0, nf + j)),
            pl.BlockSpec((1, HY_CH), lambda bi, j: (0, 0)),
        ],
        out_specs=pl.BlockSpec((None, n, HY_CH), lambda bi, j: (bi, 0, 0)),
        out_shape=jax.ShapeDtypeStruct((b, n, HY_CH), BF16),
        scratch_shapes=[pltpu.VMEM((n, HY_CH), BF16), pltpu.VMEM((n, HY_CH), F32), pltpu.VMEM((n, HY_CH), F32)],
        compiler_params=pltpu.CompilerParams(
            dimension_semantics=("arbitrary", "arbitrary"),
            vmem_limit_bytes=_vmem_limit(est + n * HY_CH * 4)),
        name="hyena_mixer",
    )(h, conv_w, conv_b.reshape(1, 3 * HY_CH), consts["f_hi"], consts["f_hi"], hp, hq,
      consts["g"], consts["g"], gnorm.reshape(1, HY_CH))


def _hy_constants(n):
    big = 2 * n
    t = jnp.linspace(0.0, 1.0, n, dtype=F32)[:, None]
    w = 2.0 * math.pi * jnp.arange(n, dtype=F32)[:, None] / n
    f = jnp.linspace(1e-4, HY_BANDS - 1, HY_BANDS, dtype=F32)[None, :]
    zfeat = jnp.concatenate([t, jnp.cos(f * w), -jnp.sin(f * w)], axis=-1)
    zfeat = jnp.pad(zfeat, ((0, 0), (0, LANE - HY_EMB)))
    min_decay = math.log(HY_TARGET) / HY_SLOW_DECAY
    max_decay = math.log(HY_TARGET) / HY_FAST_DECAY
    deltas = jnp.linspace(min_decay, max_decay, HY_CH, dtype=F32)[None, :]

    k = jnp.arange(n, dtype=jnp.int32)

    def cos_sin(mult):
        theta = ((k[:, None] * mult[None, :]) % big).astype(F32) * (2.0 * math.pi / big)
        return jnp.cos(theta), jnp.sin(theta)

    ca, sa = cos_sin(jnp.arange(n // DFT_SPLIT, dtype=jnp.int32) * DFT_SPLIT)
    cb, sb = cos_sin(jnp.arange(DFT_SPLIT, dtype=jnp.int32))
    cosm = (ca[:, :, None] * cb[:, None, :] - sa[:, :, None] * sb[:, None, :]).reshape(n, n)
    sinm = (sa[:, :, None] * cb[:, None, :] + ca[:, :, None] * sb[:, None, :]).reshape(n, n)
    alt = jnp.where(k % 2 == 0, 1.0, -1.0).astype(F32)
    fq = jnp.where((k == 0)[:, None], alt[None, :], -sinm)
    fmat = jnp.concatenate([cosm, fq], axis=0)
    f_hi = fmat.astype(BF16)
    f_lo = (fmat - f_hi.astype(F32)).astype(BF16)
    wk = jnp.where(k == 0, 1.0, 2.0).astype(F32)[None, :] / big
    gp = cosm * wk
    gq = jnp.where((k == 0)[None, :], alt[:, None] / big, -sinm * wk)
    g = jnp.concatenate([gp, gq], axis=1).astype(BF16)
    return dict(n=n, fc=min(n, 512), zfeat=zfeat, deltas=deltas, f_hi=f_hi, f_lo=f_lo, g=g)


def _out_proj_kernel(od_ref, og_ref, oh_ref, w_ref, x_ref, g_ref, m_ref, o_ref):
    for r in range(0, o_ref.shape[0], ROW_CHUNK):
        rows = slice(r, r + ROW_CHUNK)
        mix = _dot(od_ref[rows, :], w_ref[0:DIFF_W, :])
        mix = mix + _dot(og_ref[rows, :], w_ref[DIFF_W:DIFF_W + GQA_W, :])
        mix = mix + _dot(oh_ref[rows, :], w_ref[DIFF_W + GQA_W:, :])
        o_ref[rows, :] = x_ref[rows, :] + m_ref[2:3, :] * (_rms(mix) * g_ref[...])


def _out_proj(o_d, o_g, o_h, w, layer, x, g, mods, per_batch):
    b, s, d = x.shape
    tm = min(s, 512)
    assert s % tm == 0 and tm % ROW_CHUNK == 0
    est = d * d * 2 + 2 * tm * d * 2 + 4 * tm * d * 4 + 2 * tm * d * 4
    return pl.pallas_call(
        _out_proj_kernel,
        grid=(b, s // tm),
        in_specs=[
            pl.BlockSpec((None, tm, DIFF_W), lambda bi, i: (bi, i, 0)),
            pl.BlockSpec((None, tm, GQA_W), lambda bi, i: (bi, i, 0)),
            pl.BlockSpec((None, tm, HY_CH), lambda bi, i: (bi, i, 0)),
            pl.BlockSpec((None, d, d), lambda bi, i: (layer, 0, 0), pipeline_mode=pl.Buffered(1)),
            pl.BlockSpec((None, tm, d), lambda bi, i: (bi, i, 0)),
            pl.BlockSpec((1, d), lambda bi, i: (0, 0)),
            pl.BlockSpec((None, N_MOD, d), lambda bi, i: (bi * per_batch, 0, 0)),
        ],
        out_specs=pl.BlockSpec((None, tm, d), lambda bi, i: (bi, i, 0)),
        out_shape=jax.ShapeDtypeStruct((b, s, d), F32),
        compiler_params=pltpu.CompilerParams(
            dimension_semantics=("parallel", "parallel"),
            vmem_limit_bytes=_vmem_limit(est)),
        name="out_proj",
    )(o_d, o_g, o_h, w, x, g.reshape(1, d), mods)


def _mlp_kernel(x_ref, g2_ref, g3_ref, m_ref, wu_ref, wd_ref, o_ref, xn_ref, *, nf, ncol):
    f = pl.program_id(2)
    tm, d = o_ref.shape
    row_chunks = [slice(r, r + ROW_CHUNK) for r in range(0, tm, ROW_CHUNK)]

    def hidden(xn):
        return jnp.square(jnp.maximum(_dot(xn, wu_ref[...]), 0.0)).astype(BF16)

    @pl.when(f == 0)
    def _():
        for rows in row_chunks:
            y = _rms(x_ref[rows, :]) * g2_ref[...]
            xn = (y * (1.0 + m_ref[4:5, :]) + m_ref[3:4, :]).astype(BF16)
            xn_ref[rows, :] = xn
            o_ref[rows, :] = _dot(hidden(xn), wd_ref[...])

    @pl.when((f > 0) & (f < nf - 1))
    def _():
        hmid = hidden(xn_ref[...])
        for c in range(d // ncol):
            o_ref[:, c * ncol:(c + 1) * ncol] += _dot(hmid, wd_ref[:, c * ncol:(c + 1) * ncol])

    @pl.when(f == nf - 1)
    def _():
        for rows in row_chunks:
            acc = o_ref[rows, :] + _dot(hidden(xn_ref[rows, :]), wd_ref[...])
            o_ref[rows, :] = x_ref[rows, :] + m_ref[5:6, :] * (_rms(acc) * g3_ref[...])


def _mlp(x, g2, g3, mods, w_up, w_down, layer, per_batch):
    b, s, d = x.shape
    dff = w_up.shape[2]
    tm = min(s, 1024)
    assert s % tm == 0 and tm % ROW_CHUNK == 0
    tf = MLP_HIDDEN_COLS
    nf = dff // tf
    est = 4 * tm * d * 4 + tm * d * 2 + 4 * d * tf * 2 + tm * tf * 6 + tm * 512 * 4
    return pl.pallas_call(
        functools.partial(_mlp_kernel, nf=nf, ncol=512),
        grid=(b, s // tm, nf),
        in_specs=[
            pl.BlockSpec((None, tm, d), lambda bi, i, f: (bi, i, 0)),
            pl.BlockSpec((1, d), lambda bi, i, f: (0, 0)),
            pl.BlockSpec((1, d), lambda bi, i, f: (0, 0)),
            pl.BlockSpec((None, N_MOD, d), lambda bi, i, f: (bi * per_batch, 0, 0)),
            pl.BlockSpec((None, d, tf), lambda bi, i, f: (layer, 0, f)),
            pl.BlockSpec((None, tf, d), lambda bi, i, f: (layer, f, 0)),
        ],
        out_specs=pl.BlockSpec((None, tm, d), lambda bi, i, f: (bi, i, 0)),
        out_shape=jax.ShapeDtypeStruct((b, s, d), F32),
        scratch_shapes=[pltpu.VMEM((tm, d), BF16)],
        compiler_params=pltpu.CompilerParams(
            dimension_semantics=("parallel", "parallel", "arbitrary"),
            vmem_limit_bytes=_vmem_limit(est)),
        name="mlp",
    )(x, g2.reshape(1, d), g3.reshape(1, d), mods, w_up, w_down)


def _rope_tables(n_rows, head_dim):
    t_row = jnp.repeat(jnp.arange(n_rows, dtype=F32), GRID_W)
    t_col = jnp.tile(jnp.arange(GRID_W, dtype=F32), n_rows)
    d_axis = head_dim // 2
    inv = ROPE_THETA ** (-jnp.arange(0, d_axis, 2, dtype=F32) / d_axis)
    ang = jnp.concatenate([t_row[:, None] * inv, t_col[:, None] * inv], axis=-1)
    cos = jnp.repeat(jnp.cos(ang), 2, axis=-1)
    sin = jnp.repeat(jnp.sin(ang), 2, axis=-1)
    sign = jnp.where(jnp.arange(head_dim) % 2 == 0, -1.0, 1.0).astype(F32)
    sin = sin * sign
    reps = LANE // head_dim
    return jnp.tile(cos, (1, reps)), jnp.tile(sin, (1, reps))


def kernel(x, c, ctx, c_ctx, w_mod, b_mod, g_norm, w_in, w_out, diff_lam, diff_subln, gqa_q_norm,
           gqa_k_norm, gqa_out_norm, hy_conv_w, hy_conv_b, hy_w1, hy_b1, hy_w2, hy_b2, hy_w3, hy_b3,
           hy_wout, hy_freq, hy_bias, hy_out_norm, w_up, w_down):
    bsz, n_lat, d = x.shape
    n_ctx = ctx.shape[1]
    depth = w_mod.shape[0]
    assert d == D_MODEL and bsz + 1 <= MOD_ROWS
    assert n_lat % GRID_W == 0 and n_lat % 256 == 0 and n_ctx % 128 == 0

    tab_d = _rope_tables(n_lat // GRID_W, DIFF_QK)
    tab_g = _rope_tables(n_lat // GRID_W, GQA_HD)
    hyc_lat = _hy_constants(n_lat)
    hyc_ctx = _hy_constants(n_ctx)

    cpad = jnp.zeros((MOD_ROWS, d), F32).at[:bsz].set(c).at[bsz].set(c_ctx)
    mods = _modulations(cpad, w_mod, b_mod)
    lam_pad = jnp.pad(diff_lam, ((0, 0), (0, 4), (0, LANE - DIFF_QK)))

    w_in_b, w_out_b, w_up_b, w_down_b = (w.astype(BF16) for w in (w_in, w_out, w_up, w_down))
    tq_lat = min(n_lat, 512)
    tq_ctx = min(n_ctx, 256)
    xc = ctx
    for l in range(depth):
        last = l == depth - 1
        lam_init = 0.8 - 0.6 * math.exp(-0.3 * l)
        m_lat = mods[l, :bsz].reshape(bsz, N_MOD, d)
        m_ctx = mods[l, bsz:bsz + 1].reshape(1, N_MOD, d)
        filt = (hy_w1[l], hy_b1[l], hy_w2[l], hy_b2[l], hy_w3[l], hy_b3[l], hy_wout[l], hy_freq[l], hy_bias[l])

        h = _in_proj(x, g_norm[l, 0], m_lat, w_in_b, l, 0, N_IN, 1)
        xc_flat = xc.reshape(1, bsz * n_ctx, d)
        if last:
            hc = _in_proj(xc_flat, g_norm[l, 0], m_ctx, w_in_b, l, KV_OFF, KV_W, 0)
        else:
            hc = _in_proj(xc_flat, g_norm[l, 0], m_ctx, w_in_b, l, 0, N_IN, 0)
        hc = hc.reshape(bsz, n_ctx, -1)
        c_blk = 0 if last else KV_BLK

        o_d = _diff_attention(lam_pad[l], h, h, KV_BLK, hc, c_blk, tab_d, diff_subln[l], lam_init, tq_lat)
        o_g = _gqa_attention(h, h, KV_BLK, hc, c_blk, tab_g, gqa_q_norm[l], gqa_k_norm[l], gqa_out_norm[l],
                             tq_lat, 4)
        hp, hq = _hy_filters(hyc_lat, *filt)
        o_h = _hy_mixer(h, hyc_lat, hy_conv_w[l], hy_conv_b[l], hp, hq, hy_out_norm[l])
        x_new = _out_proj(o_d, o_g, o_h, w_out_b, l, x, g_norm[l, 1], m_lat, 1)

        if not last:
            oc_d = _diff_attention(lam_pad[l], hc, hc, KV_BLK, None, 0, None, diff_subln[l], lam_init, tq_ctx)
            oc_g = _gqa_attention(hc, hc, KV_BLK, None, 0, None, gqa_q_norm[l], gqa_k_norm[l],
                                  gqa_out_norm[l], tq_ctx, 4)
            hpc, hqc = _hy_filters(hyc_ctx, *filt)
            oc_h = _hy_mixer(hc, hyc_ctx, hy_conv_w[l], hy_conv_b[l], hpc, hqc, hy_out_norm[l])
            flat = lambda a: a.reshape(1, bsz * n_ctx, a.shape[-1])
            xc_flat = _out_proj(flat(oc_d), flat(oc_g), flat(oc_h), w_out_b, l, xc_flat, g_norm[l, 1],
                                m_ctx, 0)
        x = x_new

        x = _mlp(x, g_norm[l, 2], g_norm[l, 3], m_lat, w_up_b, w_down_b, l, 1)
        if not last:
            xc_flat = _mlp(xc_flat, g_norm[l, 2], g_norm[l, 3], m_ctx, w_up_b, w_down_b, l, 0)
            xc = xc_flat.reshape(bsz, n_ctx, d)
    return x
```

```python
import functools
import math

import jax
import jax.numpy as jnp
from jax import lax
from jax.experimental import pallas as pl
from jax.experimental.pallas import tpu as pltpu

F32 = jnp.float32
BF16 = jnp.bfloat16

D_MODEL = 2048
GRID_W = 64
DIFF_W = 512
GQA_W = 1024
HY_CH = 512
DIFF_V = 128
DIFF_QK = 64
DIFF_HEADS = 4
GQA_HD = 128
GQA_KV_HEADS = 2
GQA_GROUP = 4
HY_EMB = 33
HY_BANDS = 16
HY_FFN = 64
HY_FAST_DECAY = 0.3
HY_SLOW_DECAY = 1.5
HY_TARGET = 1e-2
HY_SHIFT = 0.0
D_FF = 4 * D_MODEL
N_MOD = 6
ROPE_THETA = 10000.0
EPS = 1e-6

LANE = 128
DQ_BLK = 0
GQ_BLK = 4
HY_OFF = 1536
KV_OFF = 3072
KV_BLK = KV_OFF // LANE
DK_REL, DV_REL, GK_REL, GV_REL = 0, 4, 8, 10
N_IN = 4608
KV_W = N_IN - KV_OFF

VMEM_CAP = 64 * 1024 * 1024
MOD_ROWS = 24
PREP_ROWS = 256
LOG2E = 1.4426950408889634
IN_PROJ_COLS = 1536
MLP_HIDDEN_COLS = 1024
DFT_SPLIT = 64
HY_HALF = 256
KEY_CHUNKS = (256, 128)
ROW_CHUNK = 256
DIFF_HEADS_PER_STEP = 2


def _vmem_limit(nbytes):
    return int(min(max(nbytes * 5 // 4 + (4 << 20), 32 << 20), VMEM_CAP - (6 << 20)))


def _rms(x):
    return x * lax.rsqrt(jnp.mean(x * x, axis=-1, keepdims=True) + EPS)


def _dot(a, b):
    return jnp.dot(a, b, preferred_element_type=F32)


def _dot_nt(a, b):
    return lax.dot_general(a, b, (((1,), (1,)), ((), ())), preferred_element_type=F32)


def _split_bf16(a):
    hi = a.astype(BF16)
    lo = (a - hi.astype(F32)).astype(BF16)
    return hi, lo


def _dot3(a, b):
    ah, al = _split_bf16(a)
    bh, bl = _split_bf16(b)
    return _dot(ah, bh) + (_dot(al, bh) + _dot(ah, bl))


def _rope(x, cos, sin_signed):
    lane = lax.broadcasted_iota(jnp.int32, x.shape, 1)
    nxt = pltpu.roll(x, LANE - 1, axis=1)
    prv = pltpu.roll(x, 1, axis=1)
    swapped = jnp.where(lane % 2 == 0, nxt, prv)
    return x * cos + swapped * sin_signed


def _mod_kernel(c_ref, w_ref, b_ref, o_ref):
    c = c_ref[...]
    s = c * (1.0 / (1.0 + jnp.exp(-c)))
    o_ref[0] = _dot(s.astype(BF16), w_ref[0].astype(BF16)) + b_ref[0]


def _modulations(cpad, w_mod, b_mod):
    nl, d, n = w_mod.shape
    tn = 1024
    return pl.pallas_call(
        _mod_kernel,
        grid=(nl, n // tn),
        in_specs=[
            pl.BlockSpec((MOD_ROWS, d), lambda l, j: (0, 0)),
            pl.BlockSpec((1, d, tn), lambda l, j: (l, 0, j)),
            pl.BlockSpec((1, 1, tn), lambda l, j: (l, 0, j)),
        ],
        out_specs=pl.BlockSpec((1, MOD_ROWS, tn), lambda l, j: (l, 0, j)),
        out_shape=jax.ShapeDtypeStruct((nl, MOD_ROWS, n), F32),
        compiler_params=pltpu.CompilerParams(
            dimension_semantics=("arbitrary", "arbitrary"),
            vmem_limit_bytes=_vmem_limit(2 * d * tn * 4 + d * tn * 2)),
        name="modulations",
    )(cpad, w_mod, b_mod.reshape(nl, 1, n))


def _in_proj_kernel(x_ref, g_ref, m_ref, w_ref, o_ref, xn_ref):
    j = pl.program_id(2)

    @pl.when(j == 0)
    def _():
        for r in range(0, o_ref.shape[0], ROW_CHUNK):
            rows = slice(r, r + ROW_CHUNK)
            y = _rms(x_ref[rows, :]) * g_ref[...]
            xn = (y * (1.0 + m_ref[1:2, :]) + m_ref[0:1, :]).astype(BF16)
            xn_ref[rows, :] = xn
            o_ref[rows, :] = _dot(xn, w_ref[...]).astype(o_ref.dtype)

    @pl.when(j > 0)
    def _():
        o_ref[...] = _dot(xn_ref[...], w_ref[...]).astype(o_ref.dtype)


def _in_proj(x, g, mods, w, layer, col0, n, per_batch):
    b, s, d = x.shape
    tm = min(s, 1024)
    assert s % tm == 0 and tm % ROW_CHUNK == 0
    tn = IN_PROJ_COLS
    assert n % tn == 0 and col0 % tn == 0
    jb = col0 // tn
    est = 2 * tm * d * 4 + tm * d * 2 + 2 * d * tn * 2 + 2 * tm * tn * 2 + tm * tn * 4
    return pl.pallas_call(
        _in_proj_kernel,
        grid=(b, s // tm, n // tn),
        in_specs=[
            pl.BlockSpec((None, tm, d), lambda bi, i, j: (bi, i, 0)),
            pl.BlockSpec((1, d), lambda bi, i, j: (0, 0)),
            pl.BlockSpec((None, N_MOD, d), lambda bi, i, j: (bi * per_batch, 0, 0)),
            pl.BlockSpec((None, d, tn), lambda bi, i, j: (layer, 0, jb + j)),
        ],
        out_specs=pl.BlockSpec((None, tm, tn), lambda bi, i, j: (bi, i, j)),
        out_shape=jax.ShapeDtypeStruct((b, s, n), BF16),
        scratch_shapes=[pltpu.VMEM((tm, d), BF16)],
        compiler_params=pltpu.CompilerParams(
            dimension_semantics=("parallel", "parallel", "arbitrary"),
            vmem_limit_bytes=_vmem_limit(est)),
        name="in_proj",
    )(x, g.reshape(1, d), mods, w)


def _attend_streams(streams, s_ref, finish):
    lk = streams[0][1].shape[0]
    ck = next(c for c in KEY_CHUNKS if lk % c == 0)
    chunks = [slice(r, r + ck) for r in range(0, lk, ck)]
    n = len(streams)
    qts = [make_qt() for make_qt, _, _ in streams]
    mx = None
    prev = None
    done = None
    for i in range(n + 1):
        lsum = acc = None
        for ci, rows in enumerate(chunks):
            if i < n:
                sj = _dot(streams[i][1][rows, :], qts[i])
                s_ref[i % 2, rows, :] = sj
                mj = jnp.max(sj, axis=0, keepdims=True)
                mx = mj if ci == 0 else jnp.maximum(mx, mj)
            if ci == 0 and done is not None:
                finish(*done)
                done = None
            if prev is not None:
                p = jnp.exp2(s_ref[prev[0] % 2, rows, :] - prev[1])
                lj = jnp.sum(p, axis=0, keepdims=True)
                oj = _dot(streams[prev[0]][2][:, rows], p.astype(BF16))
                lsum = lj if lsum is None else lsum + lj
                acc = oj if acc is None else acc + oj
        if prev is not None:
            done = (prev[0], acc, lsum)
        prev = (i, mx) if i < n else None
    finish(*done)


def _gqa_kernel(*refs, rope, prefix, heads_per_step):
    it = iter(refs)
    q_ref, k_ref, v_ref = next(it), next(it), next(it)
    kc_ref = vc_ref = None
    if prefix:
        kc_ref, vc_ref = next(it), next(it)
    cq_ref = sq_ref = ck_ref = sk_ref = None
    if rope:
        cq_ref, sq_ref, ck_ref, sk_ref = next(it), next(it), next(it), next(it)
    gq_ref, gk_ref, go_ref = next(it), next(it), next(it)
    o_ref = next(it)
    kall_ref, vt_ref, s_ref = next(it), next(it), next(it)

    c = kc_ref.shape[0] if prefix else 0
    s = k_ref.shape[0]

    @pl.when((pl.program_id(2) == 0) & (pl.program_id(3) == 0))
    def _():
        if prefix:
            kall_ref[0:c, :] = (_rms(kc_ref[...].astype(F32)) * gk_ref[...]).astype(BF16)
            vt_ref[:, 0:c] = vc_ref[...].astype(F32).T.astype(BF16)
        pr = min(s, PREP_ROWS)
        for r0 in range(0, s, pr):
            rows = slice(r0, r0 + pr)
            k = _rms(k_ref[rows, :].astype(F32)) * gk_ref[...]
            if rope:
                k = _rope(k, ck_ref[rows, :], sk_ref[rows, :])
            kall_ref[c + r0:c + r0 + pr, :] = k.astype(BF16)
            vt_ref[:, c + r0:c + r0 + pr] = v_ref[rows, :].astype(F32).T.astype(BF16)

    def query_t(hh):
        q = _rms(q_ref[:, hh * LANE:(hh + 1) * LANE].astype(F32)) * gq_ref[...]
        if rope:
            q = _rope(q, cq_ref[...], sq_ref[...])
        return (q * (GQA_HD ** -0.5 * LOG2E)).T.astype(BF16)

    def finish(hh, acc, denom):
        ot = acc / denom
        ot = ot * lax.rsqrt(jnp.mean(ot * ot, axis=0, keepdims=True) + EPS) * go_ref[...]
        o_ref[:, hh * LANE:(hh + 1) * LANE] = ot.T.astype(o_ref.dtype)

    streams = [(functools.partial(query_t, hh), kall_ref, vt_ref) for hh in range(heads_per_step)]
    _attend_streams(streams, s_ref, finish)


def _gqa_attention(hq, hk, kv_blk, hc, c_blk, tables, gq, gk, go, tq, heads_per_step):
    b, s, _ = hq.shape
    prefix = hc is not None
    rope = tables is not None
    hps = heads_per_step
    grid = (b, GQA_KV_HEADS, GQA_GROUP // hps, s // tq)
    in_specs = [
        pl.BlockSpec((None, tq, hps * LANE), lambda bi, g, r, i: (bi, i, (GQ_BLK + g * GQA_GROUP) // hps + r)),
        pl.BlockSpec((None, s, LANE), lambda bi, g, r, i: (bi, 0, kv_blk + GK_REL + g)),
        pl.BlockSpec((None, s, LANE), lambda bi, g, r, i: (bi, 0, kv_blk + GV_REL + g)),
    ]
    args = [hq, hk, hk]
    c = 0
    if prefix:
        c = hc.shape[1]
        in_specs += [
            pl.BlockSpec((None, c, LANE), lambda bi, g, r, i: (bi, 0, c_blk + GK_REL + g)),
            pl.BlockSpec((None, c, LANE), lambda bi, g, r, i: (bi, 0, c_blk + GV_REL + g)),
        ]
        args += [hc, hc]
    if rope:
        cos, sin = tables
        in_specs += [
            pl.BlockSpec((tq, LANE), lambda bi, g, r, i: (i, 0)),
            pl.BlockSpec((tq, LANE), lambda bi, g, r, i: (i, 0)),
            pl.BlockSpec((s, LANE), lambda bi, g, r, i: (0, 0)),
            pl.BlockSpec((s, LANE), lambda bi, g, r, i: (0, 0)),
        ]
        args += [cos, sin, cos, sin]
    vec = pl.BlockSpec((1, LANE), lambda bi, g, r, i: (0, 0))
    in_specs += [vec, vec, pl.BlockSpec((LANE, 1), lambda bi, g, r, i: (0, 0))]
    args += [gq.reshape(1, LANE), gk.reshape(1, LANE), go.reshape(LANE, 1)]
    lk = s + c
    est = 4 * s * LANE * 2 + 4 * s * LANE * 4 + 2 * lk * LANE * 2 + hps * 3 * tq * lk * 4
    return pl.pallas_call(
        functools.partial(_gqa_kernel, rope=rope, prefix=prefix, heads_per_step=hps),
        grid=grid,
        in_specs=in_specs,
        out_specs=pl.BlockSpec((None, tq, hps * LANE), lambda bi, g, r, i: (bi, i, g * GQA_GROUP // hps + r)),
        out_shape=jax.ShapeDtypeStruct((b, s, GQA_W), BF16),
        scratch_shapes=[pltpu.VMEM((lk, LANE), BF16), pltpu.VMEM((LANE, lk), BF16),
                        pltpu.VMEM((2, lk, tq), F32)],
        compiler_params=pltpu.CompilerParams(
            dimension_semantics=("arbitrary",) * 4,
            vmem_limit_bytes=_vmem_limit(est)),
        name="gqa_attention",
    )(*args)


def _diff_kernel(*refs, rope, prefix, lam_init):
    it = iter(refs)
    lp_ref, q_ref, k_ref, v_ref = next(it), next(it), next(it), next(it)
    kc_ref = vc_ref = None
    if prefix:
        kc_ref, vc_ref = next(it), next(it)
    cq_ref = sq_ref = ck_ref = sk_ref = None
    if rope:
        cq_ref, sq_ref, ck_ref, sk_ref = next(it), next(it), next(it), next(it)
    gs_ref = next(it)
    o_ref = next(it)
    kall_ref, vt_ref, s_ref, lam_ref = next(it), next(it), next(it), next(it)
    c = kc_ref.shape[0] if prefix else 0
    s = k_ref.shape[0]

    first = (pl.program_id(0) == 0) & (pl.program_id(1) == 0) & (pl.program_id(2) == 0)

    @pl.when(first)
    def _():
        p = lp_ref[...]
        t1 = jnp.sum(p[0:1] * p[1:2], axis=-1, keepdims=True)
        t2 = jnp.sum(p[2:3] * p[3:4], axis=-1, keepdims=True)
        lam_ref[...] = jnp.broadcast_to(jnp.exp(t1) - jnp.exp(t2) + lam_init, lam_ref.shape)

    @pl.when(pl.program_id(2) == 0)
    def _():
        for hh in range(DIFF_HEADS_PER_STEP):
            cols = slice(hh * LANE, (hh + 1) * LANE)
            if prefix:
                kall_ref[hh, 0:c, :] = kc_ref[:, cols]
                vt_ref[hh, :, 0:c] = vc_ref[:, cols].astype(F32).T.astype(BF16)
            pr = min(s, PREP_ROWS)
            for r0 in range(0, s, pr):
                rows = slice(r0, r0 + pr)
                if rope:
                    k = _rope(k_ref[rows, cols].astype(F32), ck_ref[rows, :], sk_ref[rows, :]).astype(BF16)
                else:
                    k = k_ref[rows, cols]
                kall_ref[hh, c + r0:c + r0 + pr, :] = k
                vt_ref[hh, :, c + r0:c + r0 + pr] = v_ref[rows, cols].astype(F32).T.astype(BF16)

    lam = lam_ref[0:1, 0:1]

    head_qt = {}

    def query_t(hh, mp):
        if hh not in head_qt:
            q = q_ref[:, hh * LANE:(hh + 1) * LANE].astype(F32)
            if rope:
                q = _rope(q, cq_ref[...], sq_ref[...])
            head_qt[hh] = (q * (DIFF_QK ** -0.5 * LOG2E)).T
        qt = head_qt[hh]
        row = lax.broadcasted_iota(jnp.int32, qt.shape, 0)
        keep = (row < DIFF_QK) if mp == 0 else (row >= DIFF_QK)
        return jnp.where(keep, qt, 0.0).astype(BF16)

    first_map = {}

    def finish(i, acc, denom):
        hh, mp = divmod(i, 2)
        if mp == 0:
            first_map[hh] = acc / denom
            return
        ot = first_map.pop(hh) - acc * (lam / denom)
        ot = ot * lax.rsqrt(jnp.mean(ot * ot, axis=0, keepdims=True) + EPS) * (gs_ref[...] * (1.0 - lam_init))
        o_ref[:, hh * LANE:(hh + 1) * LANE] = ot.T.astype(o_ref.dtype)

    streams = [(functools.partial(query_t, hh, mp), kall_ref.at[hh], vt_ref.at[hh])
               for hh in range(DIFF_HEADS_PER_STEP) for mp in range(2)]
    _attend_streams(streams, s_ref, finish)


def _diff_attention(lam_params, hq, hk, kv_blk, hc, c_blk, tables, gsub, lam_init, tq):
    b, s, _ = hq.shape
    prefix = hc is not None
    rope = tables is not None
    hps = DIFF_HEADS_PER_STEP
    w = hps * LANE
    grid = (b, DIFF_HEADS // hps, s // tq)
    in_specs = [
        pl.BlockSpec((8, LANE), lambda bi, h, i: (0, 0)),
        pl.BlockSpec((None, tq, w), lambda bi, h, i: (bi, i, DQ_BLK // hps + h)),
        pl.BlockSpec((None, s, w), lambda bi, h, i: (bi, 0, (kv_blk + DK_REL) // hps + h)),
        pl.BlockSpec((None, s, w), lambda bi, h, i: (bi, 0, (kv_blk + DV_REL) // hps + h)),
    ]
    args = [lam_params, hq, hk, hk]
    c = 0
    if prefix:
        c = hc.shape[1]
        in_specs += [
            pl.BlockSpec((None, c, w), lambda bi, h, i: (bi, 0, (c_blk + DK_REL) // hps + h)),
            pl.BlockSpec((None, c, w), lambda bi, h, i: (bi, 0, (c_blk + DV_REL) // hps + h)),
        ]
        args += [hc, hc]
    if rope:
        cos, sin = tables
        in_specs += [
            pl.BlockSpec((tq, LANE), lambda bi, h, i: (i, 0)),
            pl.BlockSpec((tq, LANE), lambda bi, h, i: (i, 0)),
            pl.BlockSpec((s, LANE), lambda bi, h, i: (0, 0)),
            pl.BlockSpec((s, LANE), lambda bi, h, i: (0, 0)),
        ]
        args += [cos, sin, cos, sin]
    in_specs.append(pl.BlockSpec((LANE, 1), lambda bi, h, i: (0, 0)))
    args.append(gsub.reshape(LANE, 1))
    lk = s + c
    est = 8 * s * w * 2 + 4 * s * LANE * 4 + 4 * lk * w * 2 + 5 * tq * lk * 4
    return pl.pallas_call(
        functools.partial(_diff_kernel, rope=rope, prefix=prefix, lam_init=lam_init),
        grid=grid,
        in_specs=in_specs,
        out_specs=pl.BlockSpec((None, tq, w), lambda bi, h, i: (bi, i, h)),
        out_shape=jax.ShapeDtypeStruct((b, s, DIFF_W), BF16),
        scratch_shapes=[pltpu.VMEM((hps, lk, LANE), BF16), pltpu.VMEM((hps, LANE, lk), BF16),
                        pltpu.VMEM((2, lk, tq), F32), pltpu.VMEM((8, LANE), F32)],
        compiler_params=pltpu.CompilerParams(
            dimension_semantics=("arbitrary",) * 3,
            vmem_limit_bytes=_vmem_limit(est)),
        name="diff_attention",
    )(*args)


def _hy_filter_kernel(z_ref, w1_ref, b1_ref, w2_ref, b2_ref, w3_ref, b3_ref, wo_ref, fr_ref,
                      bias_ref, dl_ref, fp_hi_ref, fp_lo_ref, fq_hi_ref, fq_lo_ref,
                      hp_ref, hq_ref, hs_hi, hs_lo, hd_hi, hd_lo, alt_ref):
    j = pl.program_id(0)

    @pl.when(j == 0)
    def _():
        z = z_ref[...]
        h = jnp.sin(fr_ref[0:1, :] * (_dot3(z, w1_ref[...]) + b1_ref[...]))
        h = jnp.sin(fr_ref[1:2, :] * (_dot3(h, w2_ref[...]) + b2_ref[...]))
        h = jnp.sin(fr_ref[2:3, :] * (_dot3(h, w3_ref[...]) + b3_ref[...]))
        h = _dot3(h, wo_ref[...])
        decay = jnp.exp(-z[:, 0:1] * jnp.abs(dl_ref[...])) + HY_SHIFT
        hf = h[:, :HY_CH] * decay
        hb = h[:, HY_CH:] * decay
        row = lax.broadcasted_iota(jnp.int32, hf.shape, 0)
        hf = hf + jnp.where(row == 0, bias_ref[...], 0.0)
        hb = jnp.where(row == 0, 0.0, hb)
        hs = hf + hb
        hd = hf - hb
        sign = jnp.where(row % 2 == 0, 1.0, -1.0)
        alt_ref[...] = jnp.broadcast_to(jnp.sum(sign * hs, axis=0, keepdims=True), alt_ref.shape)
        a, bb = _split_bf16(hs)
        hs_hi[...] = a
        hs_lo[...] = bb
        a, bb = _split_bf16(hd)
        hd_hi[...] = a
        hd_lo[...] = bb

    xp = _dot(fp_hi_ref[...], hs_hi[...]) + (_dot(fp_lo_ref[...], hs_hi[...]) + _dot(fp_hi_ref[...], hs_lo[...]))
    xq = _dot(fq_hi_ref[...], hd_hi[...]) + (_dot(fq_lo_ref[...], hd_hi[...]) + _dot(fq_hi_ref[...], hd_lo[...]))
    row = lax.broadcasted_iota(jnp.int32, xq.shape, 0)
    is0 = (row == 0) & (j == 0)
    hp_ref[...] = xp
    hq_ref[...] = jnp.where(is0, alt_ref[0:1, :], xq)


def _hy_filters(consts, w1, b1, w2, b2, w3, b3, wout, freq, bias):
    n = consts["n"]
    fc = consts["fc"]
    nf = n // fc
    pad = LANE - HY_FFN
    w1p = jnp.pad(w1, ((0, LANE - HY_EMB), (0, pad)))
    w2p = jnp.pad(w2, ((0, pad), (0, pad)))
    w3p = jnp.pad(w3, ((0, pad), (0, pad)))
    wop = jnp.pad(wout, ((0, pad), (0, 0)))
    b1p = jnp.pad(b1, (0, pad)).reshape(1, LANE)
    b2p = jnp.pad(b2, (0, pad)).reshape(1, LANE)
    b3p = jnp.pad(b3, (0, pad)).reshape(1, LANE)
    frp = jnp.pad(freq, ((0, 0), (0, pad)))
    full = lambda shape: pl.BlockSpec(shape, lambda j: (0,) * len(shape))
    fspec_p = pl.BlockSpec((fc, n), lambda j: (j, 0))
    fspec_q = pl.BlockSpec((fc, n), lambda j: (nf + j, 0))
    est = n * LANE * 4 * 2 + 8 * fc * n * 2 + 4 * n * HY_CH * 2 + 4 * fc * HY_CH * 4 + 8 * n * HY_CH * 4
    return pl.pallas_call(
        _hy_filter_kernel,
        grid=(nf,),
        in_specs=[
            full((n, LANE)), full((LANE, LANE)), full((1, LANE)), full((LANE, LANE)), full((1, LANE)),
            full((LANE, LANE)), full((1, LANE)), full((LANE, 2 * HY_CH)), full((3, LANE)),
            full((1, HY_CH)), full((1, HY_CH)),
            fspec_p, fspec_p, fspec_q, fspec_q,
        ],
        out_specs=[pl.BlockSpec((fc, HY_CH), lambda j: (j, 0)), pl.BlockSpec((fc, HY_CH), lambda j: (j, 0))],
        out_shape=[jax.ShapeDtypeStruct((n, HY_CH), F32), jax.ShapeDtypeStruct((n, HY_CH), F32)],
        scratch_shapes=[pltpu.VMEM((n, HY_CH), BF16)] * 4 + [pltpu.VMEM((8, HY_CH), F32)],
        compiler_params=pltpu.CompilerParams(
            dimension_semantics=("arbitrary",),
            vmem_limit_bytes=_vmem_limit(est)),
        name="hyena_filters",
    )(consts["zfeat"], w1p, b1p, w2p, b2p, w3p, b3p, wop, frp, bias.reshape(1, HY_CH), consts["deltas"],
      consts["f_hi"], consts["f_lo"], consts["f_hi"], consts["f_lo"])


def _short_conv(hy_ref, cw_ref, cb_ref, c0, n):
    x = hy_ref[:, c0:c0 + LANE].astype(F32)
    row = lax.broadcasted_iota(jnp.int32, x.shape, 0)
    prv = jnp.where(row == 0, 0.0, pltpu.roll(x, 1, axis=0))
    nxt = jnp.where(row == n - 1, 0.0, pltpu.roll(x, n - 1, axis=0))
    w = cw_ref[:, c0:c0 + LANE]
    return prv * w[0:1] + x * w[1:2] + nxt * w[2:3] + cb_ref[:, c0:c0 + LANE]


def _hy_mixer_kernel(hy_ref, cw_ref, cb_ref, fp_ref, fq_ref, hp_ref, hq_ref, gp_ref, gq_ref, gn_ref,
                     o_ref, z_ref, y_ref, x0_ref, *, n, nf):
    j = pl.program_id(1)

    fc = fp_ref.shape[0]
    freq_halves = [slice(0, fc // 2), slice(fc // 2, fc)]
    all_cols = slice(0, HY_CH)
    row_chunks = [slice(r, r + ROW_CHUNK) for r in range(0, n, ROW_CHUNK)] if n > ROW_CHUNK else [slice(0, n)]

    def spectrum(cols, z, freqs, nyquist_row):
        xp = _dot(fp_ref[freqs, :], z)
        xq = _dot(fq_ref[freqs, :], z)
        hp = hp_ref[freqs, cols]
        hq = hq_ref[freqs, cols]
        qq = xq * hq
        if nyquist_row:
            is0 = lax.broadcasted_iota(jnp.int32, xp.shape, 0) == 0
            yp = xp * hp - jnp.where(is0, 0.0, qq)
            yq = jnp.where(is0, qq, xp * hq + xq * hp)
        else:
            yp = xp * hp - qq
            yq = xp * hq + xq * hp
        return yp.astype(BF16), yq.astype(BF16)

    def spectrum_with(cols, z, first_step, fillers):
        parts = []
        for i, freqs in enumerate(freq_halves):
            parts.append(spectrum(cols, z, freqs, first_step and i == 0))
            if fillers:
                fillers.pop(0)()
        return (jnp.concatenate([p[0] for p in parts], axis=0), jnp.concatenate([p[1] for p in parts], axis=0))

    def conv_z(c0):
        x1 = _short_conv(hy_ref, cw_ref, cb_ref, HY_CH + c0, n)
        v = _short_conv(hy_ref, cw_ref, cb_ref, 2 * HY_CH + c0, n)
        zc = (v * x1).astype(BF16)
        z_ref[:, c0:c0 + LANE] = zc
        return zc

    def conv_x0(c0):
        x0_ref[:, c0:c0 + LANE] = _short_conv(hy_ref, cw_ref, cb_ref, c0, n)

    def normed_rows(rows, y):
        prod = x0_ref[rows, :] * y
        inv = lax.rsqrt(jnp.mean(prod * prod, axis=-1, keepdims=True) + EPS)
        o_ref[rows, :] = (prod * inv * gn_ref[...]).astype(o_ref.dtype)

    @pl.when(j == 0)
    def _():
        lanes = list(range(0, HY_CH, LANE))
        zc = {c0: conv_z(c0) for c0 in lanes[:HY_HALF // LANE]}
        pending = lanes[HY_HALF // LANE:]
        fillers = [functools.partial(lambda c0: zc.__setitem__(c0, conv_z(c0)), c0) for c0 in pending]
        if nf <= 2:
            fillers += [functools.partial(conv_x0, c0) for c0 in lanes]
        for h0 in range(0, HY_CH, HY_HALF):
            cols = slice(h0, h0 + HY_HALF)
            z = jnp.concatenate([zc[c0] for c0 in range(h0, h0 + HY_HALF, LANE)], axis=1)
            yp, yq = spectrum_with(cols, z, True, fillers)
            for rows in row_chunks:
                y_ref[rows, cols] = _dot(gp_ref[rows, :], yp) + _dot(gq_ref[rows, :], yq)
                if h0 > 0 and fillers:
                    fillers.pop(0)()
        while fillers:
            fillers.pop(0)()
        if nf == 1:
            for rows in row_chunks:
                normed_rows(rows, y_ref[rows, :])

    def middle(with_x0):
        fillers = [functools.partial(conv_x0, c0) for c0 in range(0, HY_CH, LANE)] if with_x0 else []
        yp, yq = spectrum_with(all_cols, z_ref[...], False, fillers)
        half = max(len(row_chunks) // 2, 1)
        for i in range(0, len(row_chunks), half):
            rows = slice(row_chunks[i].start, row_chunks[min(i + half, len(row_chunks)) - 1].stop)
            y_ref[rows, :] += _dot(gp_ref[rows, :], yp) + _dot(gq_ref[rows, :], yq)
            if fillers:
                fillers.pop(0)()
        while fillers:
            fillers.pop(0)()

    if nf > 2:
        pl.when(j == 1)(functools.partial(middle, True))
    if nf > 3:
        pl.when((j > 1) & (j < nf - 1))(functools.partial(middle, False))

    if nf > 1:
        @pl.when(j == nf - 1)
        def _():
            yp, yq = spectrum_with(all_cols, z_ref[...], False, [])
            for rows in row_chunks:
                normed_rows(rows, y_ref[rows, :] + _dot(gp_ref[rows, :], yp) + _dot(gq_ref[rows, :], yq))


def _hy_mixer(h, consts, conv_w, conv_b, hp, hq, gnorm):
    b, n, _ = h.shape
    fc = consts["fc"]
    nf = n // fc
    est = (n * 3 * HY_CH * 2 + 8 * fc * n * 2 + 4 * fc * HY_CH * 4 + n * HY_CH * (2 + 4 + 4)
           + 6 * fc * HY_CH * 4 + 6 * n * LANE * 4)
    return pl.pallas_call(
        functools.partial(_hy_mixer_kernel, n=n, nf=nf),
        grid=(b, nf),
        in_specs=[
            pl.BlockSpec((None, n, 3 * HY_CH), lambda bi, j: (bi, 0, HY_OFF // (3 * HY_CH)),
                         pipeline_mode=pl.Buffered(1)),
            pl.BlockSpec((3, 3 * HY_CH), lambda bi, j: (0, 0)),
            pl.BlockSpec((1, 3 * HY_CH), lambda bi, j: (0, 0)),
            pl.BlockSpec((fc, n), lambda bi, j: (j, 0)),
            pl.BlockSpec((fc, n), lambda bi, j: (nf + j, 0)),
            pl.BlockSpec((fc, HY_CH), lambda bi, j: (j, 0)),
            pl.BlockSpec((fc, HY_CH), lambda bi, j: (j, 0)),
            pl.BlockSpec((n, fc), lambda bi, j: (0, j)),
            pl.BlockSpec((n, fc), lambda bi, j: (0, nf + j)),
            pl.BlockSpec((1, HY_CH), lambda bi, j: (0, 0)),
        ],
        out_specs=pl.BlockSpec((None, n, HY_CH), lambda bi, j: (bi, 0, 0)),
        out_shape=jax.ShapeDtypeStruct((b, n, HY_CH), BF16),
        scratch_shapes=[pltpu.VMEM((n, HY_CH), BF16), pltpu.VMEM((n, HY_CH), F32), pltpu.VMEM((n, HY_CH), F32)],
        compiler_params=pltpu.CompilerParams(
            dimension_semantics=("arbitrary", "arbitrary"),
            vmem_limit_bytes=_vmem_limit(est + n * HY_CH * 4)),
        name="hyena_mixer",
    )(h, conv_w, conv_b.reshape(1, 3 * HY_CH), consts["f_hi"], consts["f_hi"], hp, hq,
      consts["g"], consts["g"], gnorm.reshape(1, HY_CH))


def _hy_constants(n):
    big = 2 * n
    t = jnp.linspace(0.0, 1.0, n, dtype=F32)[:, None]
    w = 2.0 * math.pi * jnp.arange(n, dtype=F32)[:, None] / n
    f = jnp.linspace(1e-4, HY_BANDS - 1, HY_BANDS, dtype=F32)[None, :]
    zfeat = jnp.concatenate([t, jnp.cos(f * w), -jnp.sin(f * w)], axis=-1)
    zfeat = jnp.pad(zfeat, ((0, 0), (0, LANE - HY_EMB)))
    min_decay = math.log(HY_TARGET) / HY_SLOW_DECAY
    max_decay = math.log(HY_TARGET) / HY_FAST_DECAY
    deltas = jnp.linspace(min_decay, max_decay, HY_CH, dtype=F32)[None, :]

    k = jnp.arange(n, dtype=jnp.int32)

    def cos_sin(mult):
        theta = ((k[:, None] * mult[None, :]) % big).astype(F32) * (2.0 * math.pi / big)
        return jnp.cos(theta), jnp.sin(theta)

    ca, sa = cos_sin(jnp.arange(n // DFT_SPLIT, dtype=jnp.int32) * DFT_SPLIT)
    cb, sb = cos_sin(jnp.arange(DFT_SPLIT, dtype=jnp.int32))
    cosm = (ca[:, :, None] * cb[:, None, :] - sa[:, :, None] * sb[:, None, :]).reshape(n, n)
    sinm = (sa[:, :, None] * cb[:, None, :] + ca[:, :, None] * sb[:, None, :]).reshape(n, n)
    alt = jnp.where(k % 2 == 0, 1.0, -1.0).astype(F32)
    fq = jnp.where((k == 0)[:, None], alt[None, :], -sinm)
    fmat = jnp.concatenate([cosm, fq], axis=0)
    f_hi = fmat.astype(BF16)
    f_lo = (fmat - f_hi.astype(F32)).astype(BF16)
    wk = jnp.where(k == 0, 1.0, 2.0).astype(F32)[None, :] / big
    gp = cosm * wk
    gq = jnp.where((k == 0)[None, :], alt[:, None] / big, -sinm * wk)
    g = jnp.concatenate([gp, gq], axis=1).astype(BF16)
    return dict(n=n, fc=min(n, 512), zfeat=zfeat, deltas=deltas, f_hi=f_hi, f_lo=f_lo, g=g)


def _out_proj_kernel(od_ref, og_ref, oh_ref, w_ref, x_ref, g_ref, m_ref, o_ref):
    for r in range(0, o_ref.shape[0], ROW_CHUNK):
        rows = slice(r, r + ROW_CHUNK)
        mix = _dot(od_ref[rows, :], w_ref[0:DIFF_W, :])
        mix = mix + _dot(og_ref[rows, :], w_ref[DIFF_W:DIFF_W + GQA_W, :])
        mix = mix + _dot(oh_ref[rows, :], w_ref[DIFF_W + GQA_W:, :])
        o_ref[rows, :] = x_ref[rows, :] + m_ref[2:3, :] * (_rms(mix) * g_ref[...])


def _out_proj(o_d, o_g, o_h, w, layer, x, g, mods, per_batch):
    b, s, d = x.shape
    tm = min(s, 512)
    assert s % tm == 0 and tm % ROW_CHUNK == 0
    est = d * d * 2 + 2 * tm * d * 2 + 4 * tm * d * 4 + 2 * tm * d * 4
    return pl.pallas_call(
        _out_proj_kernel,
        grid=(b, s // tm),
        in_specs=[
            pl.BlockSpec((None, tm, DIFF_W), lambda bi, i: (bi, i, 0)),
            pl.BlockSpec((None, tm, GQA_W), lambda bi, i: (bi, i, 0)),
            pl.BlockSpec((None, tm, HY_CH), lambda bi, i: (bi, i, 0)),
            pl.BlockSpec((None, d, d), lambda bi, i: (layer, 0, 0), pipeline_mode=pl.Buffered(1)),
            pl.BlockSpec((None, tm, d), lambda bi, i: (bi, i, 0)),
            pl.BlockSpec((1, d), lambda bi, i: (0, 0)),
            pl.BlockSpec((None, N_MOD, d), lambda bi, i: (bi * per_batch, 0, 0)),
        ],
        out_specs=pl.BlockSpec((None, tm, d), lambda bi, i: (bi, i, 0)),
        out_shape=jax.ShapeDtypeStruct((b, s, d), F32),
        compiler_params=pltpu.CompilerParams(
            dimension_semantics=("parallel", "parallel"),
            vmem_limit_bytes=_vmem_limit(est)),
        name="out_proj",
    )(o_d, o_g, o_h, w, x, g.reshape(1, d), mods)


def _mlp_kernel(x_ref, g2_ref, g3_ref, m_ref, wu_ref, wd_ref, o_ref, xn_ref, *, nf, ncol):
    f = pl.program_id(2)
    tm, d = o_ref.shape
    row_chunks = [slice(r, r + ROW_CHUNK) for r in range(0, tm, ROW_CHUNK)]

    def hidden(xn):
        return jnp.square(jnp.maximum(_dot(xn, wu_ref[...]), 0.0)).astype(BF16)

    @pl.when(f == 0)
    def _():
        for rows in row_chunks:
            y = _rms(x_ref[rows, :]) * g2_ref[...]
            xn = (y * (1.0 + m_ref[4:5, :]) + m_ref[3:4, :]).astype(BF16)
            xn_ref[rows, :] = xn
            o_ref[rows, :] = _dot(hidden(xn), wd_ref[...])

    @pl.when((f > 0) & (f < nf - 1))
    def _():
        hmid = hidden(xn_ref[...])
        for c in range(d // ncol):
            o_ref[:, c * ncol:(c + 1) * ncol] += _dot(hmid, wd_ref[:, c * ncol:(c + 1) * ncol])

    @pl.when(f == nf - 1)
    def _():
        for rows in row_chunks:
            acc = o_ref[rows, :] + _dot(hidden(xn_ref[rows, :]), wd_ref[...])
            o_ref[rows, :] = x_ref[rows, :] + m_ref[5:6, :] * (_rms(acc) * g3_ref[...])


def _mlp(x, g2, g3, mods, w_up, w_down, layer, per_batch):
    b, s, d = x.shape
    dff = w_up.shape[2]
    tm = min(s, 1024)
    assert s % tm == 0 and tm % ROW_CHUNK == 0
    tf = MLP_HIDDEN_COLS
    nf = dff // tf
    est = 4 * tm * d * 4 + tm * d * 2 + 4 * d * tf * 2 + tm * tf * 6 + tm * 512 * 4
    return pl.pallas_call(
        functools.partial(_mlp_kernel, nf=nf, ncol=512),
        grid=(b, s // tm, nf),
        in_specs=[
            pl.BlockSpec((None, tm, d), lambda bi, i, f: (bi, i, 0)),
            pl.BlockSpec((1, d), lambda bi, i, f: (0, 0)),
            pl.BlockSpec((1, d), lambda bi, i, f: (0, 0)),
            pl.BlockSpec((None, N_MOD, d), lambda bi, i, f: (bi * per_batch, 0, 0)),
            pl.BlockSpec((None, d, tf), lambda bi, i, f: (layer, 0, f)),
            pl.BlockSpec((None, tf, d), lambda bi, i, f: (layer, f, 0)),
        ],
        out_specs=pl.BlockSpec((None, tm, d), lambda bi, i, f: (bi, i, 0)),
        out_shape=jax.ShapeDtypeStruct((b, s, d), F32),
        scratch_shapes=[pltpu.VMEM((tm, d), BF16)],
        compiler_params=pltpu.CompilerParams(
            dimension_semantics=("parallel", "parallel", "arbitrary"),
            vmem_limit_bytes=_vmem_limit(est)),
        name="mlp",
    )(x, g2.reshape(1, d), g3.reshape(1, d), mods, w_up, w_down)


def _rope_tables(n_rows, head_dim):
    t_row = jnp.repeat(jnp.arange(n_rows, dtype=F32), GRID_W)
    t_col = jnp.tile(jnp.arange(GRID_W, dtype=F32), n_rows)
    d_axis = head_dim // 2
    inv = ROPE_THETA ** (-jnp.arange(0, d_axis, 2, dtype=F32) / d_axis)
    ang = jnp.concatenate([t_row[:, None] * inv, t_col[:, None] * inv], axis=-1)
    cos = jnp.repeat(jnp.cos(ang), 2, axis=-1)
    sin = jnp.repeat(jnp.sin(ang), 2, axis=-1)
    sign = jnp.where(jnp.arange(head_dim) % 2 == 0, -1.0, 1.0).astype(F32)
    sin = sin * sign
    reps = LANE // head_dim
    return jnp.tile(cos, (1, reps)), jnp.tile(sin, (1, reps))


def kernel(x, c, ctx, c_ctx, w_mod, b_mod, g_norm, w_in, w_out, diff_lam, diff_subln, gqa_q_norm,
           gqa_k_norm, gqa_out_norm, hy_conv_w, hy_conv_b, hy_w1, hy_b1, hy_w2, hy_b2, hy_w3, hy_b3,
           hy_wout, hy_freq, hy_bias, hy_out_norm, w_up, w_down):
    bsz, n_lat, d = x.shape
    n_ctx = ctx.shape[1]
    depth = w_mod.shape[0]
    assert d == D_MODEL and bsz + 1 <= MOD_ROWS
    assert n_lat % GRID_W == 0 and n_lat % 256 == 0 and n_ctx % 128 == 0

    tab_d = _rope_tables(n_lat // GRID_W, DIFF_QK)
    tab_g = _rope_tables(n_lat // GRID_W, GQA_HD)
    hyc_lat = _hy_constants(n_lat)
    hyc_ctx = _hy_constants(n_ctx)

    cpad = jnp.zeros((MOD_ROWS, d), F32).at[:bsz].set(c).at[bsz].set(c_ctx)
    mods = _modulations(cpad, w_mod, b_mod)
    lam_pad = jnp.pad(diff_lam, ((0, 0), (0, 4), (0, LANE - DIFF_QK)))

    w_in_b, w_out_b, w_up_b, w_down_b = (w.astype(BF16) for w in (w_in, w_out, w_up, w_down))
    tq_lat = min(n_lat, 1024)
    tq_ctx = min(n_ctx, 256)
    xc = ctx
    for l in range(depth):
        last = l == depth - 1
        lam_init = 0.8 - 0.6 * math.exp(-0.3 * l)
        m_lat = mods[l, :bsz].reshape(bsz, N_MOD, d)
        m_ctx = mods[l, bsz:bsz + 1].reshape(1, N_MOD, d)
        filt = (hy_w1[l], hy_b1[l], hy_w2[l], hy_b2[l], hy_w3[l], hy_b3[l], hy_wout[l], hy_freq[l], hy_bias[l])

        h = _in_proj(x, g_norm[l, 0], m_lat, w_in_b, l, 0, N_IN, 1)
        xc_flat = xc.reshape(1, bsz * n_ctx, d)
        if last:
            hc = _in_proj(xc_flat, g_norm[l, 0], m_ctx, w_in_b, l, KV_OFF, KV_W, 0)
        else:
            hc = _in_proj(xc_flat, g_norm[l, 0], m_ctx, w_in_b, l, 0, N_IN, 0)
        hc = hc.reshape(bsz, n_ctx, -1)
        c_blk = 0 if last else KV_BLK

        o_d = _diff_attention(lam_pad[l], h, h, KV_BLK, hc, c_blk, tab_d, diff_subln[l], lam_init, tq_lat)
        o_g = _gqa_attention(h, h, KV_BLK, hc, c_blk, tab_g, gqa_q_norm[l], gqa_k_norm[l], gqa_out_norm[l],
                             tq_lat, 4)
        hp, hq = _hy_filters(hyc_lat, *filt)
        o_h = _hy_mixer(h, hyc_lat, hy_conv_w[l], hy_conv_b[l], hp, hq, hy_out_norm[l])
        x_new = _out_proj(o_d, o_g, o_h, w_out_b, l, x, g_norm[l, 1], m_lat, 1)

        if not last:
            oc_d = _diff_attention(lam_pad[l], hc, hc, KV_BLK, None, 0, None, diff_subln[l], lam_init, tq_ctx)
            oc_g = _gqa_attention(hc, hc, KV_BLK, None, 0, None, gqa_q_norm[l], gqa_k_norm[l],
                                  gqa_out_norm[l], tq_ctx, 4)
            hpc, hqc = _hy_filters(hyc_ctx, *filt)
            oc_h = _hy_mixer(hc, hyc_ctx, hy_conv_w[l], hy_conv_b[l], hpc, hqc, hy_out_norm[l])
            flat = lambda a: a.reshape(1, bsz * n_ctx, a.shape[-1])
            xc_flat = _out_proj(flat(oc_d), flat(oc_g), flat(oc_h), w_out_b, l, xc_flat, g_norm[l, 1],
                                m_ctx, 0)
        x = x_new

        x = _mlp(x, g_norm[l, 2], g_norm[l, 3], m_lat, w_up_b, w_down_b, l, 1)
        if not last:
            xc_flat = _mlp(xc_flat, g_norm[l, 2], g_norm[l, 3], m_ctx, w_up_b, w_down_b, l, 0)
            xc = xc_flat.reshape(bsz, n_ctx, d)
    return x
```

```python
import functools
import math

import jax
import jax.numpy as jnp
from jax import lax
from jax.experimental import pallas as pl
from jax.experimental.pallas import tpu as pltpu

F32 = jnp.float32
BF16 = jnp.bfloat16

D_MODEL = 2048
GRID_W = 64
DIFF_W = 512
GQA_W = 1024
HY_CH = 512
DIFF_V = 128
DIFF_QK = 64
DIFF_HEADS = 4
GQA_HD = 128
GQA_KV_HEADS = 2
GQA_GROUP = 4
HY_EMB = 33
HY_BANDS = 16
HY_FFN = 64
HY_FAST_DECAY = 0.3
HY_SLOW_DECAY = 1.5
HY_TARGET = 1e-2
HY_SHIFT = 0.0
D_FF = 4 * D_MODEL
N_MOD = 6
ROPE_THETA = 10000.0
EPS = 1e-6

LANE = 128
DQ_BLK = 0
GQ_BLK = 4
HY_OFF = 1536
KV_OFF = 3072
KV_BLK = KV_OFF // LANE
DK_REL, DV_REL, GK_REL, GV_REL = 0, 4, 8, 10
N_IN = 4608
KV_W = N_IN - KV_OFF

VMEM_CAP = 64 * 1024 * 1024
MOD_ROWS = 24
PREP_ROWS = 256
LOG2E = 1.4426950408889634
IN_PROJ_COLS = 1536
MLP_HIDDEN_COLS = 1024
DFT_SPLIT = 64
HY_HALF = 256
KEY_CHUNKS = (256, 128)
ROW_CHUNK = 256
DIFF_HEADS_PER_STEP = 4


def _vmem_limit(nbytes):
    return int(min(max(nbytes * 5 // 4 + (4 << 20), 32 << 20), VMEM_CAP - (6 << 20)))


def _rms(x):
    return x * lax.rsqrt(jnp.mean(x * x, axis=-1, keepdims=True) + EPS)


def _dot(a, b):
    return jnp.dot(a, b, preferred_element_type=F32)


def _dot_nt(a, b):
    return lax.dot_general(a, b, (((1,), (1,)), ((), ())), preferred_element_type=F32)


def _split_bf16(a):
    hi = a.astype(BF16)
    lo = (a - hi.astype(F32)).astype(BF16)
    return hi, lo


def _dot3(a, b):
    ah, al = _split_bf16(a)
    bh, bl = _split_bf16(b)
    return _dot(ah, bh) + (_dot(al, bh) + _dot(ah, bl))


def _rope(x, cos, sin_signed):
    lane = lax.broadcasted_iota(jnp.int32, x.shape, 1)
    nxt = pltpu.roll(x, LANE - 1, axis=1)
    prv = pltpu.roll(x, 1, axis=1)
    swapped = jnp.where(lane % 2 == 0, nxt, prv)
    return x * cos + swapped * sin_signed


def _mod_kernel(c_ref, w_ref, b_ref, o_ref):
    c = c_ref[...]
    s = c * (1.0 / (1.0 + jnp.exp(-c)))
    o_ref[0] = _dot(s.astype(BF16), w_ref[0].astype(BF16)) + b_ref[0]


def _modulations(cpad, w_mod, b_mod):
    nl, d, n = w_mod.shape
    tn = 1024
    return pl.pallas_call(
        _mod_kernel,
        grid=(nl, n // tn),
        in_specs=[
            pl.BlockSpec((MOD_ROWS, d), lambda l, j: (0, 0)),
            pl.BlockSpec((1, d, tn), lambda l, j: (l, 0, j)),
            pl.BlockSpec((1, 1, tn), lambda l, j: (l, 0, j)),
        ],
        out_specs=pl.BlockSpec((1, MOD_ROWS, tn), lambda l, j: (l, 0, j)),
        out_shape=jax.ShapeDtypeStruct((nl, MOD_ROWS, n), F32),
        compiler_params=pltpu.CompilerParams(
            dimension_semantics=("arbitrary", "arbitrary"),
            vmem_limit_bytes=_vmem_limit(2 * d * tn * 4 + d * tn * 2)),
        name="modulations",
    )(cpad, w_mod, b_mod.reshape(nl, 1, n))


def _in_proj_kernel(x_ref, g_ref, m_ref, w_ref, o_ref, xn_ref):
    j = pl.program_id(2)

    @pl.when(j == 0)
    def _():
        for r in range(0, o_ref.shape[0], ROW_CHUNK):
            rows = slice(r, r + ROW_CHUNK)
            y = _rms(x_ref[rows, :]) * g_ref[...]
            xn = (y * (1.0 + m_ref[1:2, :]) + m_ref[0:1, :]).astype(BF16)
            xn_ref[rows, :] = xn
            o_ref[rows, :] = _dot(xn, w_ref[...]).astype(o_ref.dtype)

    @pl.when(j > 0)
    def _():
        o_ref[...] = _dot(xn_ref[...], w_ref[...]).astype(o_ref.dtype)


def _in_proj(x, g, mods, w, layer, col0, n, per_batch):
    b, s, d = x.shape
    tm = min(s, 1024)
    assert s % tm == 0 and tm % ROW_CHUNK == 0
    tn = IN_PROJ_COLS
    assert n % tn == 0 and col0 % tn == 0
    jb = col0 // tn
    est = 2 * tm * d * 4 + tm * d * 2 + 2 * d * tn * 2 + 2 * tm * tn * 2 + tm * tn * 4
    return pl.pallas_call(
        _in_proj_kernel,
        grid=(b, s // tm, n // tn),
        in_specs=[
            pl.BlockSpec((None, tm, d), lambda bi, i, j: (bi, i, 0)),
            pl.BlockSpec((1, d), lambda bi, i, j: (0, 0)),
            pl.BlockSpec((None, N_MOD, d), lambda bi, i, j: (bi * per_batch, 0, 0)),
            pl.BlockSpec((None, d, tn), lambda bi, i, j: (layer, 0, jb + j)),
        ],
        out_specs=pl.BlockSpec((None, tm, tn), lambda bi, i, j: (bi, i, j)),
        out_shape=jax.ShapeDtypeStruct((b, s, n), BF16),
        scratch_shapes=[pltpu.VMEM((tm, d), BF16)],
        compiler_params=pltpu.CompilerParams(
            dimension_semantics=("parallel", "parallel", "arbitrary"),
            vmem_limit_bytes=_vmem_limit(est)),
        name="in_proj",
    )(x, g.reshape(1, d), mods, w)


def _attend_streams(streams, s_ref, finish):
    lk = streams[0][1].shape[0]
    ck = next(c for c in KEY_CHUNKS if lk % c == 0)
    chunks = [slice(r, r + ck) for r in range(0, lk, ck)]
    n = len(streams)
    qts = [make_qt() for make_qt, _, _ in streams]
    mx = None
    prev = None
    done = None
    for i in range(n + 1):
        lsum = acc = None
        for ci, rows in enumerate(chunks):
            if i < n:
                sj = _dot(streams[i][1][rows, :], qts[i])
                s_ref[i % 2, rows, :] = sj
                mj = jnp.max(sj, axis=0, keepdims=True)
                mx = mj if ci == 0 else jnp.maximum(mx, mj)
            if ci == 0 and done is not None:
                finish(*done)
                done = None
            if prev is not None:
                p = jnp.exp2(s_ref[prev[0] % 2, rows, :] - prev[1])
                lj = jnp.sum(p, axis=0, keepdims=True)
                oj = _dot(streams[prev[0]][2][:, rows], p.astype(BF16))
                lsum = lj if lsum is None else lsum + lj
                acc = oj if acc is None else acc + oj
        if prev is not None:
            done = (prev[0], acc, lsum)
        prev = (i, mx) if i < n else None
    finish(*done)


def _gqa_kernel(*refs, rope, prefix):
    it = iter(refs)
    q_refs = [next(it) for _ in range(GQA_KV_HEADS)]
    k_ref, v_ref = next(it), next(it)
    kc_ref = vc_ref = None
    if prefix:
        kc_ref, vc_ref = next(it), next(it)
    cq_ref = sq_ref = ck_ref = sk_ref = None
    if rope:
        cq_ref, sq_ref, ck_ref, sk_ref = next(it), next(it), next(it), next(it)
    gq_ref, gk_ref, go_ref = next(it), next(it), next(it)
    o_ref = next(it)
    kall_ref, vt_ref, s_ref = next(it), next(it), next(it)

    c = kc_ref.shape[0] if prefix else 0
    s = k_ref.shape[0]

    @pl.when(pl.program_id(1) == 0)
    def _():
        for g in range(GQA_KV_HEADS):
            cols = slice(g * LANE, (g + 1) * LANE)
            if prefix:
                kall_ref[g, 0:c, :] = (_rms(kc_ref[:, cols].astype(F32)) * gk_ref[...]).astype(BF16)
                vt_ref[g, :, 0:c] = vc_ref[:, cols].astype(F32).T.astype(BF16)
            pr = min(s, PREP_ROWS)
            for r0 in range(0, s, pr):
                rows = slice(r0, r0 + pr)
                k = _rms(k_ref[rows, cols].astype(F32)) * gk_ref[...]
                if rope:
                    k = _rope(k, ck_ref[rows, :], sk_ref[rows, :])
                kall_ref[g, c + r0:c + r0 + pr, :] = k.astype(BF16)
                vt_ref[g, :, c + r0:c + r0 + pr] = v_ref[rows, cols].astype(F32).T.astype(BF16)

    def query_t(hh):
        g, r = divmod(hh, GQA_GROUP)
        q = _rms(q_refs[g][:, r * LANE:(r + 1) * LANE].astype(F32)) * gq_ref[...]
        if rope:
            q = _rope(q, cq_ref[...], sq_ref[...])
        return (q * (GQA_HD ** -0.5 * LOG2E)).T.astype(BF16)

    def finish(hh, acc, denom):
        ot = acc / denom
        ot = ot * lax.rsqrt(jnp.mean(ot * ot, axis=0, keepdims=True) + EPS) * go_ref[...]
        o_ref[:, hh * LANE:(hh + 1) * LANE] = ot.T.astype(o_ref.dtype)

    streams = [(functools.partial(query_t, hh), kall_ref.at[hh // GQA_GROUP], vt_ref.at[hh // GQA_GROUP])
               for hh in range(GQA_KV_HEADS * GQA_GROUP)]
    _attend_streams(streams, s_ref, finish)


def _gqa_attention(hq, hk, kv_blk, hc, c_blk, tables, gq, gk, go, tq):
    b, s, _ = hq.shape
    prefix = hc is not None
    rope = tables is not None
    qw = GQA_GROUP * LANE
    kw = GQA_KV_HEADS * LANE
    grid = (b, s // tq)
    in_specs = [pl.BlockSpec((None, tq, qw), functools.partial(lambda g, bi, i: (bi, i, GQ_BLK // GQA_GROUP + g), g))
                for g in range(GQA_KV_HEADS)]
    in_specs += [
        pl.BlockSpec((None, s, kw), lambda bi, i: (bi, 0, (kv_blk + GK_REL) // GQA_KV_HEADS)),
        pl.BlockSpec((None, s, kw), lambda bi, i: (bi, 0, (kv_blk + GV_REL) // GQA_KV_HEADS)),
    ]
    args = [hq] * GQA_KV_HEADS + [hk, hk]
    c = 0
    if prefix:
        c = hc.shape[1]
        in_specs += [
            pl.BlockSpec((None, c, kw), lambda bi, i: (bi, 0, (c_blk + GK_REL) // GQA_KV_HEADS)),
            pl.BlockSpec((None, c, kw), lambda bi, i: (bi, 0, (c_blk + GV_REL) // GQA_KV_HEADS)),
        ]
        args += [hc, hc]
    if rope:
        cos, sin = tables
        in_specs += [
            pl.BlockSpec((tq, LANE), lambda bi, i: (i, 0)),
            pl.BlockSpec((tq, LANE), lambda bi, i: (i, 0)),
            pl.BlockSpec((s, LANE), lambda bi, i: (0, 0)),
            pl.BlockSpec((s, LANE), lambda bi, i: (0, 0)),
        ]
        args += [cos, sin, cos, sin]
    vec = pl.BlockSpec((1, LANE), lambda bi, i: (0, 0))
    in_specs += [vec, vec, pl.BlockSpec((LANE, 1), lambda bi, i: (0, 0))]
    args += [gq.reshape(1, LANE), gk.reshape(1, LANE), go.reshape(LANE, 1)]
    lk = s + c
    est = (4 * tq * GQA_W * 2 + 8 * s * kw * 2 + 4 * s * LANE * 4 + 4 * lk * kw * 2
           + 2 * lk * tq * 4 + 4 * 256 * tq * 4)
    return pl.pallas_call(
        functools.partial(_gqa_kernel, rope=rope, prefix=prefix),
        grid=grid,
        in_specs=in_specs,
        out_specs=pl.BlockSpec((None, tq, GQA_W), lambda bi, i: (bi, i, 0)),
        out_shape=jax.ShapeDtypeStruct((b, s, GQA_W), BF16),
        scratch_shapes=[pltpu.VMEM((GQA_KV_HEADS, lk, LANE), BF16), pltpu.VMEM((GQA_KV_HEADS, LANE, lk), BF16),
                        pltpu.VMEM((2, lk, tq), F32)],
        compiler_params=pltpu.CompilerParams(
            dimension_semantics=("arbitrary",) * 2,
            vmem_limit_bytes=_vmem_limit(est)),
        name="gqa_attention",
    )(*args)


def _diff_kernel(*refs, rope, prefix, lam_init):
    it = iter(refs)
    lp_ref, q_ref, k_ref, v_ref = next(it), next(it), next(it), next(it)
    kc_ref = vc_ref = None
    if prefix:
        kc_ref, vc_ref = next(it), next(it)
    cq_ref = sq_ref = ck_ref = sk_ref = None
    if rope:
        cq_ref, sq_ref, ck_ref, sk_ref = next(it), next(it), next(it), next(it)
    gs_ref = next(it)
    o_ref = next(it)
    kall_ref, vt_ref, s_ref, lam_ref = next(it), next(it), next(it), next(it)
    c = kc_ref.shape[0] if prefix else 0
    s = k_ref.shape[0]

    first = (pl.program_id(0) == 0) & (pl.program_id(1) == 0) & (pl.program_id(2) == 0)

    @pl.when(first)
    def _():
        p = lp_ref[...]
        t1 = jnp.sum(p[0:1] * p[1:2], axis=-1, keepdims=True)
        t2 = jnp.sum(p[2:3] * p[3:4], axis=-1, keepdims=True)
        lam_ref[...] = jnp.broadcast_to(jnp.exp(t1) - jnp.exp(t2) + lam_init, lam_ref.shape)

    @pl.when(pl.program_id(2) == 0)
    def _():
        for hh in range(DIFF_HEADS_PER_STEP):
            cols = slice(hh * LANE, (hh + 1) * LANE)
            if prefix:
                kall_ref[hh, 0:c, :] = kc_ref[:, cols]
                vt_ref[hh, :, 0:c] = vc_ref[:, cols].astype(F32).T.astype(BF16)
            pr = min(s, PREP_ROWS)
            for r0 in range(0, s, pr):
                rows = slice(r0, r0 + pr)
                if rope:
                    k = _rope(k_ref[rows, cols].astype(F32), ck_ref[rows, :], sk_ref[rows, :]).astype(BF16)
                else:
                    k = k_ref[rows, cols]
                kall_ref[hh, c + r0:c + r0 + pr, :] = k
                vt_ref[hh, :, c + r0:c + r0 + pr] = v_ref[rows, cols].astype(F32).T.astype(BF16)

    lam = lam_ref[0:1, 0:1]

    head_qt = {}

    def query_t(hh, mp):
        if hh not in head_qt:
            q = q_ref[:, hh * LANE:(hh + 1) * LANE].astype(F32)
            if rope:
                q = _rope(q, cq_ref[...], sq_ref[...])
            head_qt[hh] = (q * (DIFF_QK ** -0.5 * LOG2E)).T
        qt = head_qt[hh]
        row = lax.broadcasted_iota(jnp.int32, qt.shape, 0)
        keep = (row < DIFF_QK) if mp == 0 else (row >= DIFF_QK)
        return jnp.where(keep, qt, 0.0).astype(BF16)

    first_map = {}

    def finish(i, acc, denom):
        hh, mp = divmod(i, 2)
        if mp == 0:
            first_map[hh] = acc / denom
            return
        ot = first_map.pop(hh) - acc * (lam / denom)
        ot = ot * lax.rsqrt(jnp.mean(ot * ot, axis=0, keepdims=True) + EPS) * (gs_ref[...] * (1.0 - lam_init))
        o_ref[:, hh * LANE:(hh + 1) * LANE] = ot.T.astype(o_ref.dtype)

    streams = [(functools.partial(query_t, hh, mp), kall_ref.at[hh], vt_ref.at[hh])
               for hh in range(DIFF_HEADS_PER_STEP) for mp in range(2)]
    _attend_streams(streams, s_ref, finish)


def _diff_attention(lam_params, hq, hk, kv_blk, hc, c_blk, tables, gsub, lam_init, tq):
    b, s, _ = hq.shape
    prefix = hc is not None
    rope = tables is not None
    hps = DIFF_HEADS_PER_STEP
    w = hps * LANE
    grid = (b, DIFF_HEADS // hps, s // tq)
    in_specs = [
        pl.BlockSpec((8, LANE), lambda bi, h, i: (0, 0)),
        pl.BlockSpec((None, tq, w), lambda bi, h, i: (bi, i, DQ_BLK // hps + h)),
        pl.BlockSpec((None, s, w), lambda bi, h, i: (bi, 0, (kv_blk + DK_REL) // hps + h)),
        pl.BlockSpec((None, s, w), lambda bi, h, i: (bi, 0, (kv_blk + DV_REL) // hps + h)),
    ]
    args = [lam_params, hq, hk, hk]
    c = 0
    if prefix:
        c = hc.shape[1]
        in_specs += [
            pl.BlockSpec((None, c, w), lambda bi, h, i: (bi, 0, (c_blk + DK_REL) // hps + h)),
            pl.BlockSpec((None, c, w), lambda bi, h, i: (bi, 0, (c_blk + DV_REL) // hps + h)),
        ]
        args += [hc, hc]
    if rope:
        cos, sin = tables
        in_specs += [
            pl.BlockSpec((tq, LANE), lambda bi, h, i: (i, 0)),
            pl.BlockSpec((tq, LANE), lambda bi, h, i: (i, 0)),
            pl.BlockSpec((s, LANE), lambda bi, h, i: (0, 0)),
            pl.BlockSpec((s, LANE), lambda bi, h, i: (0, 0)),
        ]
        args += [cos, sin, cos, sin]
    in_specs.append(pl.BlockSpec((LANE, 1), lambda bi, h, i: (0, 0)))
    args.append(gsub.reshape(LANE, 1))
    lk = s + c
    est = 8 * s * w * 2 + 4 * s * LANE * 4 + 4 * lk * w * 2 + 5 * tq * lk * 4
    return pl.pallas_call(
        functools.partial(_diff_kernel, rope=rope, prefix=prefix, lam_init=lam_init),
        grid=grid,
        in_specs=in_specs,
        out_specs=pl.BlockSpec((None, tq, w), lambda bi, h, i: (bi, i, h)),
        out_shape=jax.ShapeDtypeStruct((b, s, DIFF_W), BF16),
        scratch_shapes=[pltpu.VMEM((hps, lk, LANE), BF16), pltpu.VMEM((hps, LANE, lk), BF16),
                        pltpu.VMEM((2, lk, tq), F32), pltpu.VMEM((8, LANE), F32)],
        compiler_params=pltpu.CompilerParams(
            dimension_semantics=("arbitrary",) * 3,
            vmem_limit_bytes=_vmem_limit(est)),
        name="diff_attention",
    )(*args)


def _hy_filter_kernel(z_ref, w1_ref, b1_ref, w2_ref, b2_ref, w3_ref, b3_ref, wo_ref, fr_ref,
                      bias_ref, dl_ref, fp_hi_ref, fp_lo_ref, fq_hi_ref, fq_lo_ref,
                      hp_ref, hq_ref, hs_hi, hs_lo, hd_hi, hd_lo, alt_ref):
    j = pl.program_id(0)

    @pl.when(j == 0)
    def _():
        z = z_ref[...]
        h = jnp.sin(fr_ref[0:1, :] * (_dot3(z, w1_ref[...]) + b1_ref[...]))
        h = jnp.sin(fr_ref[1:2, :] * (_dot3(h, w2_ref[...]) + b2_ref[...]))
        h = jnp.sin(fr_ref[2:3, :] * (_dot3(h, w3_ref[...]) + b3_ref[...]))
        h = _dot3(h, wo_ref[...])
        decay = jnp.exp(-z[:, 0:1] * jnp.abs(dl_ref[...])) + HY_SHIFT
        hf = h[:, :HY_CH] * decay
        hb = h[:, HY_CH:] * decay
        row = lax.broadcasted_iota(jnp.int32, hf.shape, 0)
        hf = hf + jnp.where(row == 0, bias_ref[...], 0.0)
        hb = jnp.where(row == 0, 0.0, hb)
        hs = hf + hb
        hd = hf - hb
        sign = jnp.where(row % 2 == 0, 1.0, -1.0)
        alt_ref[...] = jnp.broadcast_to(jnp.sum(sign * hs, axis=0, keepdims=True), alt_ref.shape)
        a, bb = _split_bf16(hs)
        hs_hi[...] = a
        hs_lo[...] = bb
        a, bb = _split_bf16(hd)
        hd_hi[...] = a
        hd_lo[...] = bb

    xp = _dot(fp_hi_ref[...], hs_hi[...]) + (_dot(fp_lo_ref[...], hs_hi[...]) + _dot(fp_hi_ref[...], hs_lo[...]))
    xq = _dot(fq_hi_ref[...], hd_hi[...]) + (_dot(fq_lo_ref[...], hd_hi[...]) + _dot(fq_hi_ref[...], hd_lo[...]))
    row = lax.broadcasted_iota(jnp.int32, xq.shape, 0)
    is0 = (row == 0) & (j == 0)
    hp_ref[...] = xp
    hq_ref[...] = jnp.where(is0, alt_ref[0:1, :], xq)


def _hy_filters(consts, w1, b1, w2, b2, w3, b3, wout, freq, bias):
    n = consts["n"]
    fc = consts["fc"]
    nf = n // fc
    pad = LANE - HY_FFN
    w1p = jnp.pad(w1, ((0, LANE - HY_EMB), (0, pad)))
    w2p = jnp.pad(w2, ((0, pad), (0, pad)))
    w3p = jnp.pad(w3, ((0, pad), (0, pad)))
    wop = jnp.pad(wout, ((0, pad), (0, 0)))
    b1p = jnp.pad(b1, (0, pad)).reshape(1, LANE)
    b2p = jnp.pad(b2, (0, pad)).reshape(1, LANE)
    b3p = jnp.pad(b3, (0, pad)).reshape(1, LANE)
    frp = jnp.pad(freq, ((0, 0), (0, pad)))
    full = lambda shape: pl.BlockSpec(shape, lambda j: (0,) * len(shape))
    fspec_p = pl.BlockSpec((fc, n), lambda j: (j, 0))
    fspec_q = pl.BlockSpec((fc, n), lambda j: (nf + j, 0))
    est = n * LANE * 4 * 2 + 8 * fc * n * 2 + 4 * n * HY_CH * 2 + 4 * fc * HY_CH * 4 + 8 * n * HY_CH * 4
    return pl.pallas_call(
        _hy_filter_kernel,
        grid=(nf,),
        in_specs=[
            full((n, LANE)), full((LANE, LANE)), full((1, LANE)), full((LANE, LANE)), full((1, LANE)),
            full((LANE, LANE)), full((1, LANE)), full((LANE, 2 * HY_CH)), full((3, LANE)),
            full((1, HY_CH)), full((1, HY_CH)),
            fspec_p, fspec_p, fspec_q, fspec_q,
        ],
        out_specs=[pl.BlockSpec((fc, HY_CH), lambda j: (j, 0)), pl.BlockSpec((fc, HY_CH), lambda j: (j, 0))],
        out_shape=[jax.ShapeDtypeStruct((n, HY_CH), F32), jax.ShapeDtypeStruct((n, HY_CH), F32)],
        scratch_shapes=[pltpu.VMEM((n, HY_CH), BF16)] * 4 + [pltpu.VMEM((8, HY_CH), F32)],
        compiler_params=pltpu.CompilerParams(
            dimension_semantics=("arbitrary",),
            vmem_limit_bytes=_vmem_limit(est)),
        name="hyena_filters",
    )(consts["zfeat"], w1p, b1p, w2p, b2p, w3p, b3p, wop, frp, bias.reshape(1, HY_CH), consts["deltas"],
      consts["f_hi"], consts["f_lo"], consts["f_hi"], consts["f_lo"])


def _short_conv(hy_ref, cw_ref, cb_ref, c0, n):
    x = hy_ref[:, c0:c0 + LANE].astype(F32)
    row = lax.broadcasted_iota(jnp.int32, x.shape, 0)
    prv = jnp.where(row == 0, 0.0, pltpu.roll(x, 1, axis=0))
    nxt = jnp.where(row == n - 1, 0.0, pltpu.roll(x, n - 1, axis=0))
    w = cw_ref[:, c0:c0 + LANE]
    return prv * w[0:1] + x * w[1:2] + nxt * w[2:3] + cb_ref[:, c0:c0 + LANE]


def _hy_mixer_kernel(hy_ref, cw_ref, cb_ref, fp_ref, fq_ref, hp_ref, hq_ref, gp_ref, gq_ref, gn_ref,
                     o_ref, z_ref, y_ref, x0_ref, *, n, nf):
    j = pl.program_id(1)

    fc = fp_ref.shape[0]
    freq_halves = [slice(0, fc // 2), slice(fc // 2, fc)]
    all_cols = slice(0, HY_CH)
    row_chunks = [slice(r, r + ROW_CHUNK) for r in range(0, n, ROW_CHUNK)] if n > ROW_CHUNK else [slice(0, n)]

    def spectrum(cols, z, freqs, nyquist_row):
        xp = _dot(fp_ref[freqs, :], z)
        xq = _dot(fq_ref[freqs, :], z)
        hp = hp_ref[freqs, cols]
        hq = hq_ref[freqs, cols]
        qq = xq * hq
        if nyquist_row:
            is0 = lax.broadcasted_iota(jnp.int32, xp.shape, 0) == 0
            yp = xp * hp - jnp.where(is0, 0.0, qq)
            yq = jnp.where(is0, qq, xp * hq + xq * hp)
        else:
            yp = xp * hp - qq
            yq = xp * hq + xq * hp
        return yp.astype(BF16), yq.astype(BF16)

    def spectrum_with(cols, z, first_step, fillers):
        parts = []
        for i, freqs in enumerate(freq_halves):
            parts.append(spectrum(cols, z, freqs, first_step and i == 0))
            if fillers:
                fillers.pop(0)()
        return (jnp.concatenate([p[0] for p in parts], axis=0), jnp.concatenate([p[1] for p in parts], axis=0))

    def conv_z(c0):
        x1 = _short_conv(hy_ref, cw_ref, cb_ref, HY_CH + c0, n)
        v = _short_conv(hy_ref, cw_ref, cb_ref, 2 * HY_CH + c0, n)
        zc = (v * x1).astype(BF16)
        z_ref[:, c0:c0 + LANE] = zc
        return zc

    def conv_x0(c0):
        x0_ref[:, c0:c0 + LANE] = _short_conv(hy_ref, cw_ref, cb_ref, c0, n)

    def normed_rows(rows, y):
        prod = x0_ref[rows, :] * y
        inv = lax.rsqrt(jnp.mean(prod * prod, axis=-1, keepdims=True) + EPS)
        o_ref[rows, :] = (prod * inv * gn_ref[...]).astype(o_ref.dtype)

    @pl.when(j == 0)
    def _():
        lanes = list(range(0, HY_CH, LANE))
        zc = {c0: conv_z(c0) for c0 in lanes[:HY_HALF // LANE]}
        pending = lanes[HY_HALF // LANE:]
        fillers = [functools.partial(lambda c0: zc.__setitem__(c0, conv_z(c0)), c0) for c0 in pending]
        if nf <= 2:
            fillers += [functools.partial(conv_x0, c0) for c0 in lanes]
        for h0 in range(0, HY_CH, HY_HALF):
            cols = slice(h0, h0 + HY_HALF)
            z = jnp.concatenate([zc[c0] for c0 in range(h0, h0 + HY_HALF, LANE)], axis=1)
            yp, yq = spectrum_with(cols, z, True, fillers)
            for rows in row_chunks:
                y_ref[rows, cols] = _dot(gp_ref[rows, :], yp) + _dot(gq_ref[rows, :], yq)
                if h0 > 0 and fillers:
                    fillers.pop(0)()
        while fillers:
            fillers.pop(0)()
        if nf == 1:
            for rows in row_chunks:
                normed_rows(rows, y_ref[rows, :])

    def middle(with_x0):
        fillers = [functools.partial(conv_x0, c0) for c0 in range(0, HY_CH, LANE)] if with_x0 else []
        yp, yq = spectrum_with(all_cols, z_ref[...], False, fillers)
        half = max(len(row_chunks) // 2, 1)
        for i in range(0, len(row_chunks), half):
            rows = slice(row_chunks[i].start, row_chunks[min(i + half, len(row_chunks)) - 1].stop)
            y_ref[rows, :] += _dot(gp_ref[rows, :], yp) + _dot(gq_ref[rows, :], yq)
            if fillers:
                fillers.pop(0)()
        while fillers:
            fillers.pop(0)()

    if nf > 2:
        pl.when(j == 1)(functools.partial(middle, True))
    if nf > 3:
        pl.when((j > 1) & (j < nf - 1))(functools.partial(middle, False))

    if nf > 1:
        @pl.when(j == nf - 1)
        def _():
            yp, yq = spectrum_with(all_cols, z_ref[...], False, [])
            for rows in row_chunks:
                normed_rows(rows, y_ref[rows, :] + _dot(gp_ref[rows, :], yp) + _dot(gq_ref[rows, :], yq))


def _hy_mixer(h, consts, conv_w, conv_b, hp, hq, gnorm):
    b, n, _ = h.shape
    fc = consts["fc"]
    nf = n // fc
    est = (2 * n * 3 * HY_CH * 2 + 8 * fc * n * 2 + 4 * fc * HY_CH * 4 + n * HY_CH * (2 + 4 + 4)
           + 6 * fc * HY_CH * 4 + 6 * n * LANE * 4)
    return pl.pallas_call(
        functools.partial(_hy_mixer_kernel, n=n, nf=nf),
        grid=(b, nf),
        in_specs=[
            pl.BlockSpec((None, n, 3 * HY_CH), lambda bi, j: (bi, 0, HY_OFF // (3 * HY_CH))),
            pl.BlockSpec((3, 3 * HY_CH), lambda bi, j: (0, 0)),
            pl.BlockSpec((1, 3 * HY_CH), lambda bi, j: (0, 0)),
            pl.BlockSpec((fc, n), lambda bi, j: (j, 0)),
            pl.BlockSpec((fc, n), lambda bi, j: (nf + j, 0)),
            pl.BlockSpec((fc, HY_CH), lambda bi, j: (j, 0)),
            pl.BlockSpec((fc, HY_CH), lambda bi, j: (j, 0)),
            pl.BlockSpec((n, fc), lambda bi, j: (0, j)),
            pl.BlockSpec((n, fc), lambda bi, j: (0, nf + j)),
            pl.BlockSpec((1, HY_CH), lambda bi, j: (0, 0)),
        ],
        out_specs=pl.BlockSpec((None, n, HY_CH), lambda bi, j: (bi, 0, 0)),
        out_shape=jax.ShapeDtypeStruct((b, n, HY_CH), BF16),
        scratch_shapes=[pltpu.VMEM((n, HY_CH), BF16), pltpu.VMEM((n, HY_CH), F32), pltpu.VMEM((n, HY_CH), F32)],
        compiler_params=pltpu.CompilerParams(
            dimension_semantics=("arbitrary", "arbitrary"),
            vmem_limit_bytes=_vmem_limit(est + n * HY_CH * 4)),
        name="hyena_mixer",
    )(h, conv_w, conv_b.reshape(1, 3 * HY_CH), consts["f_hi"], consts["f_hi"], hp, hq,
      consts["g"], consts["g"], gnorm.reshape(1, HY_CH))


def _hy_constants(n):
    big = 2 * n
    t = jnp.linspace(0.0, 1.0, n, dtype=F32)[:, None]
    w = 2.0 * math.pi * jnp.arange(n, dtype=F32)[:, None] / n
    f = jnp.linspace(1e-4, HY_BANDS - 1, HY_BANDS, dtype=F32)[None, :]
    zfeat = jnp.concatenate([t, jnp.cos(f * w), -jnp.sin(f * w)], axis=-1)
    zfeat = jnp.pad(zfeat, ((0, 0), (0, LANE - HY_EMB)))
    min_decay = math.log(HY_TARGET) / HY_SLOW_DECAY
    max_decay = math.log(HY_TARGET) / HY_FAST_DECAY
    deltas = jnp.linspace(min_decay, max_decay, HY_CH, dtype=F32)[None, :]

    k = jnp.arange(n, dtype=jnp.int32)

    def cos_sin(mult):
        theta = ((k[:, None] * mult[None, :]) % big).astype(F32) * (2.0 * math.pi / big)
        return jnp.cos(theta), jnp.sin(theta)

    ca, sa = cos_sin(jnp.arange(n // DFT_SPLIT, dtype=jnp.int32) * DFT_SPLIT)
    cb, sb = cos_sin(jnp.arange(DFT_SPLIT, dtype=jnp.int32))
    cosm = (ca[:, :, None] * cb[:, None, :] - sa[:, :, None] * sb[:, None, :]).reshape(n, n)
    sinm = (sa[:, :, None] * cb[:, None, :] + ca[:, :, None] * sb[:, None, :]).reshape(n, n)
    alt = jnp.where(k % 2 == 0, 1.0, -1.0).astype(F32)
    fq = jnp.where((k == 0)[:, None], alt[None, :], -sinm)
    fmat = jnp.concatenate([cosm, fq], axis=0)
    f_hi = fmat.astype(BF16)
    f_lo = (fmat - f_hi.astype(F32)).astype(BF16)
    wk = jnp.where(k == 0, 1.0, 2.0).astype(F32)[None, :] / big
    gp = cosm * wk
    gq = jnp.where((k == 0)[None, :], alt[:, None] / big, -sinm * wk)
    g = jnp.concatenate([gp, gq], axis=1).astype(BF16)
    return dict(n=n, fc=min(n, 512), zfeat=zfeat, deltas=deltas, f_hi=f_hi, f_lo=f_lo, g=g)


def _out_proj_kernel(od_ref, og_ref, oh_ref, w_ref, x_ref, g_ref, m_ref, o_ref):
    for r in range(0, o_ref.shape[0], ROW_CHUNK):
        rows = slice(r, r + ROW_CHUNK)
        mix = _dot(od_ref[rows, :], w_ref[0:DIFF_W, :])
        mix = mix + _dot(og_ref[rows, :], w_ref[DIFF_W:DIFF_W + GQA_W, :])
        mix = mix + _dot(oh_ref[rows, :], w_ref[DIFF_W + GQA_W:, :])
        o_ref[rows, :] = x_ref[rows, :] + m_ref[2:3, :] * (_rms(mix) * g_ref[...])


def _out_proj(o_d, o_g, o_h, w, layer, x, g, mods, per_batch):
    b, s, d = x.shape
    tm = min(s, 512)
    assert s % tm == 0 and tm % ROW_CHUNK == 0
    est = d * d * 2 + 2 * tm * d * 2 + 4 * tm * d * 4 + 2 * tm * d * 4
    return pl.pallas_call(
        _out_proj_kernel,
        grid=(b, s // tm),
        in_specs=[
            pl.BlockSpec((None, tm, DIFF_W), lambda bi, i: (bi, i, 0)),
            pl.BlockSpec((None, tm, GQA_W), lambda bi, i: (bi, i, 0)),
            pl.BlockSpec((None, tm, HY_CH), lambda bi, i: (bi, i, 0)),
            pl.BlockSpec((None, d, d), lambda bi, i: (layer, 0, 0), pipeline_mode=pl.Buffered(1)),
            pl.BlockSpec((None, tm, d), lambda bi, i: (bi, i, 0)),
            pl.BlockSpec((1, d), lambda bi, i: (0, 0)),
            pl.BlockSpec((None, N_MOD, d), lambda bi, i: (bi * per_batch, 0, 0)),
        ],
        out_specs=pl.BlockSpec((None, tm, d), lambda bi, i: (bi, i, 0)),
        out_shape=jax.ShapeDtypeStruct((b, s, d), F32),
        compiler_params=pltpu.CompilerParams(
            dimension_semantics=("parallel", "parallel"),
            vmem_limit_bytes=_vmem_limit(est)),
        name="out_proj",
    )(o_d, o_g, o_h, w, x, g.reshape(1, d), mods)


def _mlp_kernel(x_ref, g2_ref, g3_ref, m_ref, wu_ref, wd_ref, o_ref, xn_ref, *, nf, ncol):
    f = pl.program_id(2)
    tm, d = o_ref.shape
    row_chunks = [slice(r, r + ROW_CHUNK) for r in range(0, tm, ROW_CHUNK)]

    def hidden(xn):
        return jnp.square(jnp.maximum(_dot(xn, wu_ref[...]), 0.0)).astype(BF16)

    @pl.when(f == 0)
    def _():
        for rows in row_chunks:
            y = _rms(x_ref[rows, :]) * g2_ref[...]
            xn = (y * (1.0 + m_ref[4:5, :]) + m_ref[3:4, :]).astype(BF16)
            xn_ref[rows, :] = xn
            o_ref[rows, :] = _dot(hidden(xn), wd_ref[...])

    @pl.when((f > 0) & (f < nf - 1))
    def _():
        hmid = hidden(xn_ref[...])
        for c in range(d // ncol):
            o_ref[:, c * ncol:(c + 1) * ncol] += _dot(hmid, wd_ref[:, c * ncol:(c + 1) * ncol])

    @pl.when(f == nf - 1)
    def _():
        for rows in row_chunks:
            acc = o_ref[rows, :] + _dot(hidden(xn_ref[rows, :]), wd_ref[...])
            o_ref[rows, :] = x_ref[rows, :] + m_ref[5:6, :] * (_rms(acc) * g3_ref[...])


def _mlp(x, g2, g3, mods, w_up, w_down, layer, per_batch):
    b, s, d = x.shape
    dff = w_up.shape[2]
    tm = min(s, 1024)
    assert s % tm == 0 and tm % ROW_CHUNK == 0
    tf = MLP_HIDDEN_COLS
    nf = dff // tf
    est = 4 * tm * d * 4 + tm * d * 2 + 4 * d * tf * 2 + tm * tf * 6 + tm * 512 * 4
    return pl.pallas_call(
        functools.partial(_mlp_kernel, nf=nf, ncol=512),
        grid=(b, s // tm, nf),
        in_specs=[
            pl.BlockSpec((None, tm, d), lambda bi, i, f: (bi, i, 0)),
            pl.BlockSpec((1, d), lambda bi, i, f: (0, 0)),
            pl.BlockSpec((1, d), lambda bi, i, f: (0, 0)),
            pl.BlockSpec((None, N_MOD, d), lambda bi, i, f: (bi * per_batch, 0, 0)),
            pl.BlockSpec((None, d, tf), lambda bi, i, f: (layer, 0, f)),
            pl.BlockSpec((None, tf, d), lambda bi, i, f: (layer, f, 0)),
        ],
        out_specs=pl.BlockSpec((None, tm, d), lambda bi, i, f: (bi, i, 0)),
        out_shape=jax.ShapeDtypeStruct((b, s, d), F32),
        scratch_shapes=[pltpu.VMEM((tm, d), BF16)],
        compiler_params=pltpu.CompilerParams(
            dimension_semantics=("parallel", "parallel", "arbitrary"),
            vmem_limit_bytes=_vmem_limit(est)),
        name="mlp",
    )(x, g2.reshape(1, d), g3.reshape(1, d), mods, w_up, w_down)


def _rope_tables(n_rows, head_dim):
    t_row = jnp.repeat(jnp.arange(n_rows, dtype=F32), GRID_W)
    t_col = jnp.tile(jnp.arange(GRID_W, dtype=F32), n_rows)
    d_axis = head_dim // 2
    inv = ROPE_THETA ** (-jnp.arange(0, d_axis, 2, dtype=F32) / d_axis)
    ang = jnp.concatenate([t_row[:, None] * inv, t_col[:, None] * inv], axis=-1)
    cos = jnp.repeat(jnp.cos(ang), 2, axis=-1)
    sin = jnp.repeat(jnp.sin(ang), 2, axis=-1)
    sign = jnp.where(jnp.arange(head_dim) % 2 == 0, -1.0, 1.0).astype(F32)
    sin = sin * sign
    reps = LANE // head_dim
    return jnp.tile(cos, (1, reps)), jnp.tile(sin, (1, reps))


def kernel(x, c, ctx, c_ctx, w_mod, b_mod, g_norm, w_in, w_out, diff_lam, diff_subln, gqa_q_norm,
           gqa_k_norm, gqa_out_norm, hy_conv_w, hy_conv_b, hy_w1, hy_b1, hy_w2, hy_b2, hy_w3, hy_b3,
           hy_wout, hy_freq, hy_bias, hy_out_norm, w_up, w_down):
    bsz, n_lat, d = x.shape
    n_ctx = ctx.shape[1]
    depth = w_mod.shape[0]
    assert d == D_MODEL and bsz + 1 <= MOD_ROWS
    assert n_lat % GRID_W == 0 and n_lat % 256 == 0 and n_ctx % 128 == 0

    tab_d = _rope_tables(n_lat // GRID_W, DIFF_QK)
    tab_g = _rope_tables(n_lat // GRID_W, GQA_HD)
    hyc_lat = _hy_constants(n_lat)
    hyc_ctx = _hy_constants(n_ctx)

    cpad = jnp.zeros((MOD_ROWS, d), F32).at[:bsz].set(c).at[bsz].set(c_ctx)
    mods = _modulations(cpad, w_mod, b_mod)
    lam_pad = jnp.pad(diff_lam, ((0, 0), (0, 4), (0, LANE - DIFF_QK)))

    w_in_b, w_out_b, w_up_b, w_down_b = (w.astype(BF16) for w in (w_in, w_out, w_up, w_down))
    tq_lat = min(n_lat, 1024)
    tq_ctx = min(n_ctx, 256)
    xc = ctx
    for l in range(depth):
        last = l == depth - 1
        lam_init = 0.8 - 0.6 * math.exp(-0.3 * l)
        m_lat = mods[l, :bsz].reshape(bsz, N_MOD, d)
        m_ctx = mods[l, bsz:bsz + 1].reshape(1, N_MOD, d)
        filt = (hy_w1[l], hy_b1[l], hy_w2[l], hy_b2[l], hy_w3[l], hy_b3[l], hy_wout[l], hy_freq[l], hy_bias[l])

        h = _in_proj(x, g_norm[l, 0], m_lat, w_in_b, l, 0, N_IN, 1)
        xc_flat = xc.reshape(1, bsz * n_ctx, d)
        if last:
            hc = _in_proj(xc_flat, g_norm[l, 0], m_ctx, w_in_b, l, KV_OFF, KV_W, 0)
        else:
            hc = _in_proj(xc_flat, g_norm[l, 0], m_ctx, w_in_b, l, 0, N_IN, 0)
        hc = hc.reshape(bsz, n_ctx, -1)
        c_blk = 0 if last else KV_BLK

        o_d = _diff_attention(lam_pad[l], h, h, KV_BLK, hc, c_blk, tab_d, diff_subln[l], lam_init, tq_lat)
        o_g = _gqa_attention(h, h, KV_BLK, hc, c_blk, tab_g, gqa_q_norm[l], gqa_k_norm[l], gqa_out_norm[l],
                             tq_lat)
        hp, hq = _hy_filters(hyc_lat, *filt)
        o_h = _hy_mixer(h, hyc_lat, hy_conv_w[l], hy_conv_b[l], hp, hq, hy_out_norm[l])
        x_new = _out_proj(o_d, o_g, o_h, w_out_b, l, x, g_norm[l, 1], m_lat, 1)

        if not last:
            oc_d = _diff_attention(lam_pad[l], hc, hc, KV_BLK, None, 0, None, diff_subln[l], lam_init, tq_ctx)
            oc_g = _gqa_attention(hc, hc, KV_BLK, None, 0, None, gqa_q_norm[l], gqa_k_norm[l],
                                  gqa_out_norm[l], tq_ctx)
            hpc, hqc = _hy_filters(hyc_ctx, *filt)
            oc_h = _hy_mixer(hc, hyc_ctx, hy_conv_w[l], hy_conv_b[l], hpc, hqc, hy_out_norm[l])
            flat = lambda a: a.reshape(1, bsz * n_ctx, a.shape[-1])
            xc_flat = _out_proj(flat(oc_d), flat(oc_g), flat(oc_h), w_out_b, l, xc_flat, g_norm[l, 1],
                                m_ctx, 0)
        x = x_new

        x = _mlp(x, g_norm[l, 2], g_norm[l, 3], m_lat, w_up_b, w_down_b, l, 1)
        if not last:
            xc_flat = _mlp(xc_flat, g_norm[l, 2], g_norm[l, 3], m_ctx, w_up_b, w_down_b, l, 0)
            xc = xc_flat.reshape(bsz, n_ctx, d)
    return x
```

```python
import functools
import math

import jax
import jax.numpy as jnp
from jax import lax
from jax.experimental import pallas as pl
from jax.experimental.pallas import tpu as pltpu

F32 = jnp.float32
BF16 = jnp.bfloat16

D_MODEL = 2048
GRID_W = 64
DIFF_W = 512
GQA_W = 1024
HY_CH = 512
DIFF_V = 128
DIFF_QK = 64
DIFF_HEADS = 4
GQA_HD = 128
GQA_KV_HEADS = 2
GQA_GROUP = 4
HY_EMB = 33
HY_BANDS = 16
HY_FFN = 64
HY_FAST_DECAY = 0.3
HY_SLOW_DECAY = 1.5
HY_TARGET = 1e-2
HY_SHIFT = 0.0
D_FF = 4 * D_MODEL
N_MOD = 6
ROPE_THETA = 10000.0
EPS = 1e-6

LANE = 128
DQ_BLK = 0
GQ_BLK = 4
HY_OFF = 1536
KV_OFF = 3072
KV_BLK = KV_OFF // LANE
DK_REL, DV_REL, GK_REL, GV_REL = 0, 4, 8, 10
N_IN = 4608
KV_W = N_IN - KV_OFF

VMEM_CAP = 64 * 1024 * 1024
MOD_ROWS = 24
PREP_ROWS = 256
LOG2E = 1.4426950408889634
IN_PROJ_COLS = (2304, 1536)
MLP_HIDDEN_COLS = 1024
DFT_SPLIT = 64
HY_HALF = 256
KEY_CHUNKS = (256, 128)
ROW_CHUNK = 256
DIFF_HEADS_PER_STEP = 4


def _vmem_limit(nbytes):
    return int(min(max(nbytes * 5 // 4 + (4 << 20), 32 << 20), VMEM_CAP - (6 << 20)))


def _rms(x):
    return x * lax.rsqrt(jnp.mean(x * x, axis=-1, keepdims=True) + EPS)


def _dot(a, b):
    return jnp.dot(a, b, preferred_element_type=F32)


def _split_bf16(a):
    hi = a.astype(BF16)
    lo = (a - hi.astype(F32)).astype(BF16)
    return hi, lo


def _dot3(a, b):
    ah, al = _split_bf16(a)
    bh, bl = _split_bf16(b)
    return _dot(ah, bh) + (_dot(al, bh) + _dot(ah, bl))


def _rope(x, cos, sin_signed):
    lane = lax.broadcasted_iota(jnp.int32, x.shape, 1)
    nxt = pltpu.roll(x, LANE - 1, axis=1)
    prv = pltpu.roll(x, 1, axis=1)
    swapped = jnp.where(lane % 2 == 0, nxt, prv)
    return x * cos + swapped * sin_signed


def _mod_kernel(c_ref, w_ref, b_ref, o_ref):
    c = c_ref[...]
    s = c * (1.0 / (1.0 + jnp.exp(-c)))
    o_ref[0] = _dot(s.astype(BF16), w_ref[0].astype(BF16)) + b_ref[0]


def _modulations(cpad, w_mod, b_mod):
    nl, d, n = w_mod.shape
    tn = 1024
    return pl.pallas_call(
        _mod_kernel,
        grid=(nl, n // tn),
        in_specs=[
            pl.BlockSpec((MOD_ROWS, d), lambda l, j: (0, 0)),
            pl.BlockSpec((1, d, tn), lambda l, j: (l, 0, j)),
            pl.BlockSpec((1, 1, tn), lambda l, j: (l, 0, j)),
        ],
        out_specs=pl.BlockSpec((1, MOD_ROWS, tn), lambda l, j: (l, 0, j)),
        out_shape=jax.ShapeDtypeStruct((nl, MOD_ROWS, n), F32),
        compiler_params=pltpu.CompilerParams(
            dimension_semantics=("arbitrary", "arbitrary"),
            vmem_limit_bytes=_vmem_limit(2 * d * tn * 4 + d * tn * 2)),
        name="modulations",
    )(cpad, w_mod, b_mod.reshape(nl, 1, n))


def _in_proj_kernel(x_ref, g_ref, m_ref, w_ref, o_ref, xn_ref):
    j = pl.program_id(2)

    @pl.when(j == 0)
    def _():
        for r in range(0, o_ref.shape[0], ROW_CHUNK):
            rows = slice(r, r + ROW_CHUNK)
            y = _rms(x_ref[rows, :]) * g_ref[...]
            xn = (y * (1.0 + m_ref[1:2, :]) + m_ref[0:1, :]).astype(BF16)
            xn_ref[rows, :] = xn
            o_ref[rows, :] = _dot(xn, w_ref[...]).astype(o_ref.dtype)

    @pl.when(j > 0)
    def _():
        o_ref[...] = _dot(xn_ref[...], w_ref[...]).astype(o_ref.dtype)


def _in_proj(x, g, mods, w, layer, col0, n, per_batch):
    b, s, d = x.shape
    tm = min(s, 1024)
    assert s % tm == 0 and tm % ROW_CHUNK == 0
    tn = next(t for t in IN_PROJ_COLS if n % t == 0 and col0 % t == 0)
    jb = col0 // tn
    est = 2 * tm * d * 4 + tm * d * 2 + 2 * d * tn * 2 + 2 * tm * tn * 2 + tm * tn * 4
    return pl.pallas_call(
        _in_proj_kernel,
        grid=(b, s // tm, n // tn),
        in_specs=[
            pl.BlockSpec((None, tm, d), lambda bi, i, j: (bi, i, 0)),
            pl.BlockSpec((1, d), lambda bi, i, j: (0, 0)),
            pl.BlockSpec((None, N_MOD, d), lambda bi, i, j: (bi * per_batch, 0, 0)),
            pl.BlockSpec((None, d, tn), lambda bi, i, j: (layer, 0, jb + j)),
        ],
        out_specs=pl.BlockSpec((None, tm, tn), lambda bi, i, j: (bi, i, j)),
        out_shape=jax.ShapeDtypeStruct((b, s, n), BF16),
        scratch_shapes=[pltpu.VMEM((tm, d), BF16)],
        compiler_params=pltpu.CompilerParams(
            dimension_semantics=("parallel", "parallel", "arbitrary"),
            vmem_limit_bytes=_vmem_limit(est)),
        name="in_proj",
    )(x, g.reshape(1, d), mods, w)


def _attend_streams(streams, s_ref, finish):
    lk = streams[0][1].shape[0]
    ck = next(c for c in KEY_CHUNKS if lk % c == 0)
    chunks = [slice(r, r + ck) for r in range(0, lk, ck)]
    n = len(streams)
    qts = [make_qt() for make_qt, _, _ in streams]
    mx = None
    prev = None
    done = None
    for i in range(n + 1):
        lsum = acc = None
        for ci, rows in enumerate(chunks):
            if i < n:
                sj = _dot(streams[i][1][rows, :], qts[i])
                s_ref[i % 2, rows, :] = sj
                mj = jnp.max(sj, axis=0, keepdims=True)
                mx = mj if ci == 0 else jnp.maximum(mx, mj)
            if ci == 0 and done is not None:
                finish(*done)
                done = None
            if prev is not None:
                p = jnp.exp2(s_ref[prev[0] % 2, rows, :] - prev[1])
                lj = jnp.sum(p, axis=0, keepdims=True)
                oj = _dot(streams[prev[0]][2][:, rows], p.astype(BF16))
                lsum = lj if lsum is None else lsum + lj
                acc = oj if acc is None else acc + oj
        if prev is not None:
            done = (prev[0], acc, lsum)
        prev = (i, mx) if i < n else None
    finish(*done)


def _gqa_kernel(*refs, rope, prefix):
    it = iter(refs)
    q_refs = [next(it) for _ in range(GQA_KV_HEADS)]
    k_ref, v_ref = next(it), next(it)
    kc_ref = vc_ref = None
    if prefix:
        kc_ref, vc_ref = next(it), next(it)
    cq_ref = sq_ref = ck_ref = sk_ref = None
    if rope:
        cq_ref, sq_ref, ck_ref, sk_ref = next(it), next(it), next(it), next(it)
    gq_ref, gk_ref, go_ref = next(it), next(it), next(it)
    o_ref = next(it)
    kall_ref, vt_ref, s_ref = next(it), next(it), next(it)

    c = kc_ref.shape[0] if prefix else 0
    s = k_ref.shape[0]

    @pl.when(pl.program_id(1) == 0)
    def _():
        for g in range(GQA_KV_HEADS):
            cols = slice(g * LANE, (g + 1) * LANE)
            if prefix:
                kall_ref[g, 0:c, :] = (_rms(kc_ref[:, cols].astype(F32)) * gk_ref[...]).astype(BF16)
                vt_ref[g, :, 0:c] = vc_ref[:, cols].astype(F32).T.astype(BF16)
            pr = min(s, PREP_ROWS)
            for r0 in range(0, s, pr):
                rows = slice(r0, r0 + pr)
                k = _rms(k_ref[rows, cols].astype(F32)) * gk_ref[...]
                if rope:
                    k = _rope(k, ck_ref[rows, :], sk_ref[rows, :])
                kall_ref[g, c + r0:c + r0 + pr, :] = k.astype(BF16)
                vt_ref[g, :, c + r0:c + r0 + pr] = v_ref[rows, cols].astype(F32).T.astype(BF16)

    def query_t(hh):
        g, r = divmod(hh, GQA_GROUP)
        q = _rms(q_refs[g][:, r * LANE:(r + 1) * LANE].astype(F32)) * gq_ref[...]
        if rope:
            q = _rope(q, cq_ref[...], sq_ref[...])
        return (q * (GQA_HD ** -0.5 * LOG2E)).T.astype(BF16)

    def finish(hh, acc, denom):
        ot = acc / denom
        ot = ot * lax.rsqrt(jnp.mean(ot * ot, axis=0, keepdims=True) + EPS) * go_ref[...]
        o_ref[:, hh * LANE:(hh + 1) * LANE] = ot.T.astype(o_ref.dtype)

    streams = [(functools.partial(query_t, hh), kall_ref.at[hh // GQA_GROUP], vt_ref.at[hh // GQA_GROUP])
               for hh in range(GQA_KV_HEADS * GQA_GROUP)]
    _attend_streams(streams, s_ref, finish)


def _gqa_attention(hq, hk, kv_blk, hc, c_blk, tables, gq, gk, go, tq):
    b, s, _ = hq.shape
    prefix = hc is not None
    rope = tables is not None
    qw = GQA_GROUP * LANE
    kw = GQA_KV_HEADS * LANE
    grid = (b, s // tq)
    in_specs = [pl.BlockSpec((None, tq, qw), functools.partial(lambda g, bi, i: (bi, i, GQ_BLK // GQA_GROUP + g), g))
                for g in range(GQA_KV_HEADS)]
    in_specs += [
        pl.BlockSpec((None, s, kw), lambda bi, i: (bi, 0, (kv_blk + GK_REL) // GQA_KV_HEADS)),
        pl.BlockSpec((None, s, kw), lambda bi, i: (bi, 0, (kv_blk + GV_REL) // GQA_KV_HEADS)),
    ]
    args = [hq] * GQA_KV_HEADS + [hk, hk]
    c = 0
    if prefix:
        c = hc.shape[1]
        in_specs += [
            pl.BlockSpec((None, c, kw), lambda bi, i: (bi, 0, (c_blk + GK_REL) // GQA_KV_HEADS)),
            pl.BlockSpec((None, c, kw), lambda bi, i: (bi, 0, (c_blk + GV_REL) // GQA_KV_HEADS)),
        ]
        args += [hc, hc]
    if rope:
        cos, sin = tables
        in_specs += [
            pl.BlockSpec((tq, LANE), lambda bi, i: (i, 0)),
            pl.BlockSpec((tq, LANE), lambda bi, i: (i, 0)),
            pl.BlockSpec((s, LANE), lambda bi, i: (0, 0)),
            pl.BlockSpec((s, LANE), lambda bi, i: (0, 0)),
        ]
        args += [cos, sin, cos, sin]
    vec = pl.BlockSpec((1, LANE), lambda bi, i: (0, 0))
    in_specs += [vec, vec, pl.BlockSpec((LANE, 1), lambda bi, i: (0, 0))]
    args += [gq.reshape(1, LANE), gk.reshape(1, LANE), go.reshape(LANE, 1)]
    lk = s + c
    est = (4 * tq * GQA_W * 2 + 8 * s * kw * 2 + 4 * s * LANE * 4 + 4 * lk * kw * 2
           + 2 * lk * tq * 4 + 4 * 256 * tq * 4)
    return pl.pallas_call(
        functools.partial(_gqa_kernel, rope=rope, prefix=prefix),
        grid=grid,
        in_specs=in_specs,
        out_specs=pl.BlockSpec((None, tq, GQA_W), lambda bi, i: (bi, i, 0)),
        out_shape=jax.ShapeDtypeStruct((b, s, GQA_W), BF16),
        scratch_shapes=[pltpu.VMEM((GQA_KV_HEADS, lk, LANE), BF16), pltpu.VMEM((GQA_KV_HEADS, LANE, lk), BF16),
                        pltpu.VMEM((2, lk, tq), F32)],
        compiler_params=pltpu.CompilerParams(
            dimension_semantics=("arbitrary",) * 2,
            vmem_limit_bytes=_vmem_limit(est)),
        name="gqa_attention",
    )(*args)


def _diff_kernel(*refs, rope, prefix, lam_init):
    it = iter(refs)
    lp_ref, q_ref, k_ref, v_ref = next(it), next(it), next(it), next(it)
    kc_ref = vc_ref = None
    if prefix:
        kc_ref, vc_ref = next(it), next(it)
    cq_ref = sq_ref = ck_ref = sk_ref = None
    if rope:
        cq_ref, sq_ref, ck_ref, sk_ref = next(it), next(it), next(it), next(it)
    gs_ref = next(it)
    o_ref = next(it)
    kall_ref, vt_ref, s_ref, lam_ref = next(it), next(it), next(it), next(it)
    c = kc_ref.shape[0] if prefix else 0
    s = k_ref.shape[0]

    first = (pl.program_id(0) == 0) & (pl.program_id(1) == 0) & (pl.program_id(2) == 0)

    @pl.when(first)
    def _():
        p = lp_ref[...]
        t1 = jnp.sum(p[0:1] * p[1:2], axis=-1, keepdims=True)
        t2 = jnp.sum(p[2:3] * p[3:4], axis=-1, keepdims=True)
        lam_ref[...] = jnp.broadcast_to(jnp.exp(t1) - jnp.exp(t2) + lam_init, lam_ref.shape)

    @pl.when(pl.program_id(2) == 0)
    def _():
        for hh in range(DIFF_HEADS_PER_STEP):
            cols = slice(hh * LANE, (hh + 1) * LANE)
            if prefix:
                kall_ref[hh, 0:c, :] = kc_ref[:, cols]
                vt_ref[hh, :, 0:c] = vc_ref[:, cols].astype(F32).T.astype(BF16)
            pr = min(s, PREP_ROWS)
            for r0 in range(0, s, pr):
                rows = slice(r0, r0 + pr)
                if rope:
                    k = _rope(k_ref[rows, cols].astype(F32), ck_ref[rows, :], sk_ref[rows, :]).astype(BF16)
                else:
                    k = k_ref[rows, cols]
                kall_ref[hh, c + r0:c + r0 + pr, :] = k
                vt_ref[hh, :, c + r0:c + r0 + pr] = v_ref[rows, cols].astype(F32).T.astype(BF16)

    lam = lam_ref[0:1, 0:1]

    head_qt = {}

    def query_t(hh, mp):
        if hh not in head_qt:
            q = q_ref[:, hh * LANE:(hh + 1) * LANE].astype(F32)
            if rope:
                q = _rope(q, cq_ref[...], sq_ref[...])
            head_qt[hh] = (q * (DIFF_QK ** -0.5 * LOG2E)).T
        qt = head_qt[hh]
        row = lax.broadcasted_iota(jnp.int32, qt.shape, 0)
        keep = (row < DIFF_QK) if mp == 0 else (row >= DIFF_QK)
        return jnp.where(keep, qt, 0.0).astype(BF16)

    first_map = {}

    def finish(i, acc, denom):
        hh, mp = divmod(i, 2)
        if mp == 0:
            first_map[hh] = acc / denom
            return
        ot = first_map.pop(hh) - acc * (lam / denom)
        ot = ot * lax.rsqrt(jnp.mean(ot * ot, axis=0, keepdims=True) + EPS) * (gs_ref[...] * (1.0 - lam_init))
        o_ref[:, hh * LANE:(hh + 1) * LANE] = ot.T.astype(o_ref.dtype)

    streams = [(functools.partial(query_t, hh, mp), kall_ref.at[hh], vt_ref.at[hh])
               for hh in range(DIFF_HEADS_PER_STEP) for mp in range(2)]
    _attend_streams(streams, s_ref, finish)


def _diff_attention(lam_params, hq, hk, kv_blk, hc, c_blk, tables, gsub, lam_init, tq):
    b, s, _ = hq.shape
    prefix = hc is not None
    rope = tables is not None
    hps = DIFF_HEADS_PER_STEP
    w = hps * LANE
    grid = (b, DIFF_HEADS // hps, s // tq)
    in_specs = [
        pl.BlockSpec((8, LANE), lambda bi, h, i: (0, 0)),
        pl.BlockSpec((None, tq, w), lambda bi, h, i: (bi, i, DQ_BLK // hps + h)),
        pl.BlockSpec((None, s, w), lambda bi, h, i: (bi, 0, (kv_blk + DK_REL) // hps + h)),
        pl.BlockSpec((None, s, w), lambda bi, h, i: (bi, 0, (kv_blk + DV_REL) // hps + h)),
    ]
    args = [lam_params, hq, hk, hk]
    c = 0
    if prefix:
        c = hc.shape[1]
        in_specs += [
            pl.BlockSpec((None, c, w), lambda bi, h, i: (bi, 0, (c_blk + DK_REL) // hps + h)),
            pl.BlockSpec((None, c, w), lambda bi, h, i: (bi, 0, (c_blk + DV_REL) // hps + h)),
        ]
        args += [hc, hc]
    if rope:
        cos, sin = tables
        in_specs += [
            pl.BlockSpec((tq, LANE), lambda bi, h, i: (i, 0)),
            pl.BlockSpec((tq, LANE), lambda bi, h, i: (i, 0)),
            pl.BlockSpec((s, LANE), lambda bi, h, i: (0, 0)),
            pl.BlockSpec((s, LANE), lambda bi, h, i: (0, 0)),
        ]
        args += [cos, sin, cos, sin]
    in_specs.append(pl.BlockSpec((LANE, 1), lambda bi, h, i: (0, 0)))
    args.append(gsub.reshape(LANE, 1))
    lk = s + c
    est = 8 * s * w * 2 + 4 * s * LANE * 4 + 4 * lk * w * 2 + 5 * tq * lk * 4
    return pl.pallas_call(
        functools.partial(_diff_kernel, rope=rope, prefix=prefix, lam_init=lam_init),
        grid=grid,
        in_specs=in_specs,
        out_specs=pl.BlockSpec((None, tq, w), lambda bi, h, i: (bi, i, h)),
        out_shape=jax.ShapeDtypeStruct((b, s, DIFF_W), BF16),
        scratch_shapes=[pltpu.VMEM((hps, lk, LANE), BF16), pltpu.VMEM((hps, LANE, lk), BF16),
                        pltpu.VMEM((2, lk, tq), F32), pltpu.VMEM((8, LANE), F32)],
        compiler_params=pltpu.CompilerParams(
            dimension_semantics=("arbitrary",) * 3,
            vmem_limit_bytes=_vmem_limit(est)),
        name="diff_attention",
    )(*args)


def _hy_filter_kernel(z_ref, w1_ref, b1_ref, w2_ref, b2_ref, w3_ref, b3_ref, wo_ref, fr_ref,
                      bias_ref, dl_ref, fp_hi_ref, fp_lo_ref, fq_hi_ref, fq_lo_ref,
                      hp_ref, hq_ref, hs_hi, hs_lo, hd_hi, hd_lo, alt_ref):
    j = pl.program_id(0)

    @pl.when(j == 0)
    def _():
        z = z_ref[...]
        h = jnp.sin(fr_ref[0:1, :] * (_dot3(z, w1_ref[...]) + b1_ref[...]))
        h = jnp.sin(fr_ref[1:2, :] * (_dot3(h, w2_ref[...]) + b2_ref[...]))
        h = jnp.sin(fr_ref[2:3, :] * (_dot3(h, w3_ref[...]) + b3_ref[...]))
        h = _dot3(h, wo_ref[...])
        decay = jnp.exp(-z[:, 0:1] * jnp.abs(dl_ref[...])) + HY_SHIFT
        hf = h[:, :HY_CH] * decay
        hb = h[:, HY_CH:] * decay
        row = lax.broadcasted_iota(jnp.int32, hf.shape, 0)
        hf = hf + jnp.where(row == 0, bias_ref[...], 0.0)
        hb = jnp.where(row == 0, 0.0, hb)
        hs = hf + hb
        hd = hf - hb
        sign = jnp.where(row % 2 == 0, 1.0, -1.0)
        alt_ref[...] = jnp.broadcast_to(jnp.sum(sign * hs, axis=0, keepdims=True), alt_ref.shape)
        a, bb = _split_bf16(hs)
        hs_hi[...] = a
        hs_lo[...] = bb
        a, bb = _split_bf16(hd)
        hd_hi[...] = a
        hd_lo[...] = bb

    xp = _dot(fp_hi_ref[...], hs_hi[...]) + (_dot(fp_lo_ref[...], hs_hi[...]) + _dot(fp_hi_ref[...], hs_lo[...]))
    xq = _dot(fq_hi_ref[...], hd_hi[...]) + (_dot(fq_lo_ref[...], hd_hi[...]) + _dot(fq_hi_ref[...], hd_lo[...]))
    row = lax.broadcasted_iota(jnp.int32, xq.shape, 0)
    is0 = (row == 0) & (j == 0)
    hp_ref[...] = xp
    hq_ref[...] = jnp.where(is0, alt_ref[0:1, :], xq)


def _hy_filters(consts, w1, b1, w2, b2, w3, b3, wout, freq, bias):
    n = consts["n"]
    fc = consts["fc"]
    nf = n // fc
    pad = LANE - HY_FFN
    w1p = jnp.pad(w1, ((0, LANE - HY_EMB), (0, pad)))
    w2p = jnp.pad(w2, ((0, pad), (0, pad)))
    w3p = jnp.pad(w3, ((0, pad), (0, pad)))
    wop = jnp.pad(wout, ((0, pad), (0, 0)))
    b1p = jnp.pad(b1, (0, pad)).reshape(1, LANE)
    b2p = jnp.pad(b2, (0, pad)).reshape(1, LANE)
    b3p = jnp.pad(b3, (0, pad)).reshape(1, LANE)
    frp = jnp.pad(freq, ((0, 0), (0, pad)))
    full = lambda shape: pl.BlockSpec(shape, lambda j: (0,) * len(shape))
    fspec_p = pl.BlockSpec((fc, n), lambda j: (j, 0))
    fspec_q = pl.BlockSpec((fc, n), lambda j: (nf + j, 0))
    est = n * LANE * 4 * 2 + 8 * fc * n * 2 + 4 * n * HY_CH * 2 + 4 * fc * HY_CH * 4 + 8 * n * HY_CH * 4
    return pl.pallas_call(
        _hy_filter_kernel,
        grid=(nf,),
        in_specs=[
            full((n, LANE)), full((LANE, LANE)), full((1, LANE)), full((LANE, LANE)), full((1, LANE)),
            full((LANE, LANE)), full((1, LANE)), full((LANE, 2 * HY_CH)), full((3, LANE)),
            full((1, HY_CH)), full((1, HY_CH)),
            fspec_p, fspec_p, fspec_q, fspec_q,
        ],
        out_specs=[pl.BlockSpec((fc, HY_CH), lambda j: (j, 0)), pl.BlockSpec((fc, HY_CH), lambda j: (j, 0))],
        out_shape=[jax.ShapeDtypeStruct((n, HY_CH), F32), jax.ShapeDtypeStruct((n, HY_CH), F32)],
        scratch_shapes=[pltpu.VMEM((n, HY_CH), BF16)] * 4 + [pltpu.VMEM((8, HY_CH), F32)],
        compiler_params=pltpu.CompilerParams(
            dimension_semantics=("arbitrary",),
            vmem_limit_bytes=_vmem_limit(est)),
        name="hyena_filters",
    )(consts["zfeat"], w1p, b1p, w2p, b2p, w3p, b3p, wop, frp, bias.reshape(1, HY_CH), consts["deltas"],
      consts["f_hi"], consts["f_lo"], consts["f_hi"], consts["f_lo"])


def _short_conv(hy_ref, cw_ref, cb_ref, c0, n):
    x = hy_ref[:, c0:c0 + LANE].astype(F32)
    row = lax.broadcasted_iota(jnp.int32, x.shape, 0)
    prv = jnp.where(row == 0, 0.0, pltpu.roll(x, 1, axis=0))
    nxt = jnp.where(row == n - 1, 0.0, pltpu.roll(x, n - 1, axis=0))
    w = cw_ref[:, c0:c0 + LANE]
    return prv * w[0:1] + x * w[1:2] + nxt * w[2:3] + cb_ref[:, c0:c0 + LANE]


def _hy_mixer_kernel(hy_ref, cw_ref, cb_ref, fp_ref, fq_ref, hp_ref, hq_ref, gp_ref, gq_ref, gn_ref,
                     o_ref, z_ref, y_ref, x0_ref, *, n, nf):
    j = pl.program_id(1)

    fc = fp_ref.shape[0]
    freq_halves = [slice(0, fc // 2), slice(fc // 2, fc)]
    all_cols = slice(0, HY_CH)
    row_chunks = [slice(r, r + ROW_CHUNK) for r in range(0, n, ROW_CHUNK)] if n > ROW_CHUNK else [slice(0, n)]

    def spectrum(cols, z, freqs, nyquist_row):
        xp = _dot(fp_ref[freqs, :], z)
        xq = _dot(fq_ref[freqs, :], z)
        hp = hp_ref[freqs, cols]
        hq = hq_ref[freqs, cols]
        qq = xq * hq
        if nyquist_row:
            is0 = lax.broadcasted_iota(jnp.int32, xp.shape, 0) == 0
            yp = xp * hp - jnp.where(is0, 0.0, qq)
            yq = jnp.where(is0, qq, xp * hq + xq * hp)
        else:
            yp = xp * hp - qq
            yq = xp * hq + xq * hp
        return yp.astype(BF16), yq.astype(BF16)

    def spectrum_with(cols, z, first_step, fillers):
        parts = []
        for i, freqs in enumerate(freq_halves):
            parts.append(spectrum(cols, z, freqs, first_step and i == 0))
            if fillers:
                fillers.pop(0)()
        return (jnp.concatenate([p[0] for p in parts], axis=0), jnp.concatenate([p[1] for p in parts], axis=0))

    def conv_z(c0):
        x1 = _short_conv(hy_ref, cw_ref, cb_ref, HY_CH + c0, n)
        v = _short_conv(hy_ref, cw_ref, cb_ref, 2 * HY_CH + c0, n)
        zc = (v * x1).astype(BF16)
        z_ref[:, c0:c0 + LANE] = zc
        return zc

    def conv_x0(c0):
        x0_ref[:, c0:c0 + LANE] = _short_conv(hy_ref, cw_ref, cb_ref, c0, n)

    def normed_rows(rows, y):
        prod = x0_ref[rows, :] * y
        inv = lax.rsqrt(jnp.mean(prod * prod, axis=-1, keepdims=True) + EPS)
        o_ref[rows, :] = (prod * inv * gn_ref[...]).astype(o_ref.dtype)

    @pl.when(j == 0)
    def _():
        lanes = list(range(0, HY_CH, LANE))
        zc = {c0: conv_z(c0) for c0 in lanes[:HY_HALF // LANE]}
        pending = lanes[HY_HALF // LANE:]
        fillers = [functools.partial(lambda c0: zc.__setitem__(c0, conv_z(c0)), c0) for c0 in pending]
        if nf <= 2:
            fillers += [functools.partial(conv_x0, c0) for c0 in lanes]
        for h0 in range(0, HY_CH, HY_HALF):
            cols = slice(h0, h0 + HY_HALF)
            z = jnp.concatenate([zc[c0] for c0 in range(h0, h0 + HY_HALF, LANE)], axis=1)
            yp, yq = spectrum_with(cols, z, True, fillers)
            for rows in row_chunks:
                y_ref[rows, cols] = _dot(gp_ref[rows, :], yp) + _dot(gq_ref[rows, :], yq)
                if h0 > 0 and fillers:
                    fillers.pop(0)()
        while fillers:
            fillers.pop(0)()
        if nf == 1:
            for rows in row_chunks:
                normed_rows(rows, y_ref[rows, :])

    def middle(with_x0):
        fillers = [functools.partial(conv_x0, c0) for c0 in range(0, HY_CH, LANE)] if with_x0 else []
        yp, yq = spectrum_with(all_cols, z_ref[...], False, fillers)
        half = max(len(row_chunks) // 2, 1)
        for i in range(0, len(row_chunks), half):
            rows = slice(row_chunks[i].start, row_chunks[min(i + half, len(row_chunks)) - 1].stop)
            y_ref[rows, :] += _dot(gp_ref[rows, :], yp) + _dot(gq_ref[rows, :], yq)
            if fillers:
                fillers.pop(0)()
        while fillers:
            fillers.pop(0)()

    if nf > 2:
        pl.when(j == 1)(functools.partial(middle, True))
    if nf > 3:
        pl.when((j > 1) & (j < nf - 1))(functools.partial(middle, False))

    if nf > 1:
        @pl.when(j == nf - 1)
        def _():
            yp, yq = spectrum_with(all_cols, z_ref[...], False, [])
            for rows in row_chunks:
                normed_rows(rows, y_ref[rows, :] + _dot(gp_ref[rows, :], yp) + _dot(gq_ref[rows, :], yq))


def _hy_mixer(h, consts, conv_w, conv_b, hp, hq, gnorm):
    b, n, _ = h.shape
    fc = consts["fc"]
    nf = n // fc
    est = (2 * n * 3 * HY_CH * 2 + 8 * fc * n * 2 + 4 * fc * HY_CH * 4 + n * HY_CH * (2 + 4 + 4)
           + 6 * fc * HY_CH * 4 + 6 * n * LANE * 4)
    return pl.pallas_call(
        functools.partial(_hy_mixer_kernel, n=n, nf=nf),
        grid=(b, nf),
        in_specs=[
            pl.BlockSpec((None, n, 3 * HY_CH), lambda bi, j: (bi, 0, HY_OFF // (3 * HY_CH))),
            pl.BlockSpec((3, 3 * HY_CH), lambda bi, j: (0, 0)),
            pl.BlockSpec((1, 3 * HY_CH), lambda bi, j: (0, 0)),
            pl.BlockSpec((fc, n), lambda bi, j: (j, 0)),
            pl.BlockSpec((fc, n), lambda bi, j: (nf + j, 0)),
            pl.BlockSpec((fc, HY_CH), lambda bi, j: (j, 0)),
            pl.BlockSpec((fc, HY_CH), lambda bi, j: (j, 0)),
            pl.BlockSpec((n, fc), lambda bi, j: (0, j)),
            pl.BlockSpec((n, fc), lambda bi, j: (0, nf + j)),
            pl.BlockSpec((1, HY_CH), lambda bi, j: (0, 0)),
        ],
        out_specs=pl.BlockSpec((None, n, HY_CH), lambda bi, j: (bi, 0, 0)),
        out_shape=jax.ShapeDtypeStruct((b, n, HY_CH), BF16),
        scratch_shapes=[pltpu.VMEM((n, HY_CH), BF16), pltpu.VMEM((n, HY_CH), F32), pltpu.VMEM((n, HY_CH), F32)],
        compiler_params=pltpu.CompilerParams(
            dimension_semantics=("arbitrary", "arbitrary"),
            vmem_limit_bytes=_vmem_limit(est + n * HY_CH * 4)),
        name="hyena_mixer",
    )(h, conv_w, conv_b.reshape(1, 3 * HY_CH), consts["f_hi"], consts["f_hi"], hp, hq,
      consts["g"], consts["g"], gnorm.reshape(1, HY_CH))


def _hy_constants(n):
    big = 2 * n
    t = jnp.linspace(0.0, 1.0, n, dtype=F32)[:, None]
    w = 2.0 * math.pi * jnp.arange(n, dtype=F32)[:, None] / n
    f = jnp.linspace(1e-4, HY_BANDS - 1, HY_BANDS, dtype=F32)[None, :]
    zfeat = jnp.concatenate([t, jnp.cos(f * w), -jnp.sin(f * w)], axis=-1)
    zfeat = jnp.pad(zfeat, ((0, 0), (0, LANE - HY_EMB)))
    min_decay = math.log(HY_TARGET) / HY_SLOW_DECAY
    max_decay = math.log(HY_TARGET) / HY_FAST_DECAY
    deltas = jnp.linspace(min_decay, max_decay, HY_CH, dtype=F32)[None, :]

    k = jnp.arange(n, dtype=jnp.int32)

    def cos_sin(mult):
        theta = ((k[:, None] * mult[None, :]) % big).astype(F32) * (2.0 * math.pi / big)
        return jnp.cos(theta), jnp.sin(theta)

    ca, sa = cos_sin(jnp.arange(n // DFT_SPLIT, dtype=jnp.int32) * DFT_SPLIT)
    cb, sb = cos_sin(jnp.arange(DFT_SPLIT, dtype=jnp.int32))
    cosm = (ca[:, :, None] * cb[:, None, :] - sa[:, :, None] * sb[:, None, :]).reshape(n, n)
    sinm = (sa[:, :, None] * cb[:, None, :] + ca[:, :, None] * sb[:, None, :]).reshape(n, n)
    alt = jnp.where(k % 2 == 0, 1.0, -1.0).astype(F32)
    fq = jnp.where((k == 0)[:, None], alt[None, :], -sinm)
    fmat = jnp.concatenate([cosm, fq], axis=0)
    f_hi = fmat.astype(BF16)
    f_lo = (fmat - f_hi.astype(F32)).astype(BF16)
    wk = jnp.where(k == 0, 1.0, 2.0).astype(F32)[None, :] / big
    gp = cosm * wk
    gq = jnp.where((k == 0)[None, :], alt[:, None] / big, -sinm * wk)
    g = jnp.concatenate([gp, gq], axis=1).astype(BF16)
    return dict(n=n, fc=min(n, 512), zfeat=zfeat, deltas=deltas, f_hi=f_hi, f_lo=f_lo, g=g)


def _out_proj_kernel(od_ref, og_ref, oh_ref, w_ref, x_ref, g_ref, m_ref, o_ref):
    for r in range(0, o_ref.shape[0], ROW_CHUNK):
        rows = slice(r, r + ROW_CHUNK)
        mix = _dot(od_ref[rows, :], w_ref[0:DIFF_W, :])
        mix = mix + _dot(og_ref[rows, :], w_ref[DIFF_W:DIFF_W + GQA_W, :])
        mix = mix + _dot(oh_ref[rows, :], w_ref[DIFF_W + GQA_W:, :])
        o_ref[rows, :] = x_ref[rows, :] + m_ref[2:3, :] * (_rms(mix) * g_ref[...])


def _out_proj(o_d, o_g, o_h, w, layer, x, g, mods, per_batch):
    b, s, d = x.shape
    tm = min(s, 1024)
    assert s % tm == 0 and tm % ROW_CHUNK == 0
    est = d * d * 2 + 2 * tm * d * 2 + 4 * tm * d * 4 + 2 * ROW_CHUNK * d * 4
    return pl.pallas_call(
        _out_proj_kernel,
        grid=(b, s // tm),
        in_specs=[
            pl.BlockSpec((None, tm, DIFF_W), lambda bi, i: (bi, i, 0)),
            pl.BlockSpec((None, tm, GQA_W), lambda bi, i: (bi, i, 0)),
            pl.BlockSpec((None, tm, HY_CH), lambda bi, i: (bi, i, 0)),
            pl.BlockSpec((None, d, d), lambda bi, i: (layer, 0, 0), pipeline_mode=pl.Buffered(1)),
            pl.BlockSpec((None, tm, d), lambda bi, i: (bi, i, 0)),
            pl.BlockSpec((1, d), lambda bi, i: (0, 0)),
            pl.BlockSpec((None, N_MOD, d), lambda bi, i: (bi * per_batch, 0, 0)),
        ],
        out_specs=pl.BlockSpec((None, tm, d), lambda bi, i: (bi, i, 0)),
        out_shape=jax.ShapeDtypeStruct((b, s, d), F32),
        compiler_params=pltpu.CompilerParams(
            dimension_semantics=("parallel", "parallel"),
            vmem_limit_bytes=_vmem_limit(est)),
        name="out_proj",
    )(o_d, o_g, o_h, w, x, g.reshape(1, d), mods)


def _mlp_kernel(x_ref, g2_ref, g3_ref, m_ref, wu_ref, wd_ref, o_ref, xn_ref, *, nf, ncol):
    f = pl.program_id(2)
    tm, d = o_ref.shape
    row_chunks = [slice(r, r + ROW_CHUNK) for r in range(0, tm, ROW_CHUNK)]

    def hidden(xn):
        return jnp.square(jnp.maximum(_dot(xn, wu_ref[...]), 0.0)).astype(BF16)

    @pl.when(f == 0)
    def _():
        for rows in row_chunks:
            y = _rms(x_ref[rows, :]) * g2_ref[...]
            xn = (y * (1.0 + m_ref[4:5, :]) + m_ref[3:4, :]).astype(BF16)
            xn_ref[rows, :] = xn
            o_ref[rows, :] = _dot(hidden(xn), wd_ref[...])

    @pl.when((f > 0) & (f < nf - 1))
    def _():
        hmid = hidden(xn_ref[...])
        for c in range(d // ncol):
            o_ref[:, c * ncol:(c + 1) * ncol] += _dot(hmid, wd_ref[:, c * ncol:(c + 1) * ncol])

    @pl.when(f == nf - 1)
    def _():
        for rows in row_chunks:
            acc = o_ref[rows, :] + _dot(hidden(xn_ref[rows, :]), wd_ref[...])
            o_ref[rows, :] = x_ref[rows, :] + m_ref[5:6, :] * (_rms(acc) * g3_ref[...])


def _mlp(x, g2, g3, mods, w_up, w_down, layer, per_batch):
    b, s, d = x.shape
    dff = w_up.shape[2]
    tm = min(s, 1024)
    assert s % tm == 0 and tm % ROW_CHUNK == 0
    tf = MLP_HIDDEN_COLS
    nf = dff // tf
    est = 4 * tm * d * 4 + tm * d * 2 + 4 * d * tf * 2 + tm * tf * 6 + tm * 512 * 4
    return pl.pallas_call(
        functools.partial(_mlp_kernel, nf=nf, ncol=512),
        grid=(b, s // tm, nf),
        in_specs=[
            pl.BlockSpec((None, tm, d), lambda bi, i, f: (bi, i, 0)),
            pl.BlockSpec((1, d), lambda bi, i, f: (0, 0)),
            pl.BlockSpec((1, d), lambda bi, i, f: (0, 0)),
            pl.BlockSpec((None, N_MOD, d), lambda bi, i, f: (bi * per_batch, 0, 0)),
            pl.BlockSpec((None, d, tf), lambda bi, i, f: (layer, 0, f)),
            pl.BlockSpec((None, tf, d), lambda bi, i, f: (layer, f, 0)),
        ],
        out_specs=pl.BlockSpec((None, tm, d), lambda bi, i, f: (bi, i, 0)),
        out_shape=jax.ShapeDtypeStruct((b, s, d), F32),
        scratch_shapes=[pltpu.VMEM((tm, d), BF16)],
        compiler_params=pltpu.CompilerParams(
            dimension_semantics=("parallel", "parallel", "arbitrary"),
            vmem_limit_bytes=_vmem_limit(est)),
        name="mlp",
    )(x, g2.reshape(1, d), g3.reshape(1, d), mods, w_up, w_down)


def _rope_tables(n_rows, head_dim):
    t_row = jnp.repeat(jnp.arange(n_rows, dtype=F32), GRID_W)
    t_col = jnp.tile(jnp.arange(GRID_W, dtype=F32), n_rows)
    d_axis = head_dim // 2
    inv = ROPE_THETA ** (-jnp.arange(0, d_axis, 2, dtype=F32) / d_axis)
    ang = jnp.concatenate([t_row[:, None] * inv, t_col[:, None] * inv], axis=-1)
    cos = jnp.repeat(jnp.cos(ang), 2, axis=-1)
    sin = jnp.repeat(jnp.sin(ang), 2, axis=-1)
    sign = jnp.where(jnp.arange(head_dim) % 2 == 0, -1.0, 1.0).astype(F32)
    sin = sin * sign
    reps = LANE // head_dim
    return jnp.tile(cos, (1, reps)), jnp.tile(sin, (1, reps))


def kernel(x, c, ctx, c_ctx, w_mod, b_mod, g_norm, w_in, w_out, diff_lam, diff_subln, gqa_q_norm,
           gqa_k_norm, gqa_out_norm, hy_conv_w, hy_conv_b, hy_w1, hy_b1, hy_w2, hy_b2, hy_w3, hy_b3,
           hy_wout, hy_freq, hy_bias, hy_out_norm, w_up, w_down):
    bsz, n_lat, d = x.shape
    n_ctx = ctx.shape[1]
    depth = w_mod.shape[0]
    assert d == D_MODEL and bsz + 1 <= MOD_ROWS
    assert n_lat % GRID_W == 0 and n_lat % 256 == 0 and n_ctx % 128 == 0

    tab_d = _rope_tables(n_lat // GRID_W, DIFF_QK)
    tab_g = _rope_tables(n_lat // GRID_W, GQA_HD)
    hyc_lat = _hy_constants(n_lat)
    hyc_ctx = _hy_constants(n_ctx)

    cpad = jnp.zeros((MOD_ROWS, d), F32).at[:bsz].set(c).at[bsz].set(c_ctx)
    mods = _modulations(cpad, w_mod, b_mod)
    lam_pad = jnp.pad(diff_lam, ((0, 0), (0, 4), (0, LANE - DIFF_QK)))

    w_in_b, w_out_b, w_up_b, w_down_b = (w.astype(BF16) for w in (w_in, w_out, w_up, w_down))
    tq_lat = min(n_lat, 1024)
    tq_ctx = min(n_ctx, 256)
    xc = ctx
    for l in range(depth):
        last = l == depth - 1
        lam_init = 0.8 - 0.6 * math.exp(-0.3 * l)
        m_lat = mods[l, :bsz].reshape(bsz, N_MOD, d)
        m_ctx = mods[l, bsz:bsz + 1].reshape(1, N_MOD, d)
        filt = (hy_w1[l], hy_b1[l], hy_w2[l], hy_b2[l], hy_w3[l], hy_b3[l], hy_wout[l], hy_freq[l], hy_bias[l])

        h = _in_proj(x, g_norm[l, 0], m_lat, w_in_b, l, 0, N_IN, 1)
        xc_flat = xc.reshape(1, bsz * n_ctx, d)
        if last:
            hc = _in_proj(xc_flat, g_norm[l, 0], m_ctx, w_in_b, l, KV_OFF, KV_W, 0)
        else:
            hc = _in_proj(xc_flat, g_norm[l, 0], m_ctx, w_in_b, l, 0, N_IN, 0)
        hc = hc.reshape(bsz, n_ctx, -1)
        c_blk = 0 if last else KV_BLK

        o_d = _diff_attention(lam_pad[l], h, h, KV_BLK, hc, c_blk, tab_d, diff_subln[l], lam_init, tq_lat)
        o_g = _gqa_attention(h, h, KV_BLK, hc, c_blk, tab_g, gqa_q_norm[l], gqa_k_norm[l], gqa_out_norm[l],
                             tq_lat)
        hp, hq = _hy_filters(hyc_lat, *filt)
        o_h = _hy_mixer(h, hyc_lat, hy_conv_w[l], hy_conv_b[l], hp, hq, hy_out_norm[l])
        x_new = _out_proj(o_d, o_g, o_h, w_out_b, l, x, g_norm[l, 1], m_lat, 1)

        if not last:
            oc_d = _diff_attention(lam_pad[l], hc, hc, KV_BLK, None, 0, None, diff_subln[l], lam_init, tq_ctx)
            oc_g = _gqa_attention(hc, hc, KV_BLK, None, 0, None, gqa_q_norm[l], gqa_k_norm[l],
                                  gqa_out_norm[l], tq_ctx)
            hpc, hqc = _hy_filters(hyc_ctx, *filt)
            oc_h = _hy_mixer(hc, hyc_ctx, hy_conv_w[l], hy_conv_b[l], hpc, hqc, hy_out_norm[l])
            flat = lambda a: a.reshape(1, bsz * n_ctx, a.shape[-1])
            xc_flat = _out_proj(flat(oc_d), flat(oc_g), flat(oc_h), w_out_b, l, xc_flat, g_norm[l, 1],
                                m_ctx, 0)
        x = x_new

        x = _mlp(x, g_norm[l, 2], g_norm[l, 3], m_lat, w_up_b, w_down_b, l, 1)
        if not last:
            xc_flat = _mlp(xc_flat, g_norm[l, 2], g_norm[l, 3], m_ctx, w_up_b, w_down_b, l, 0)
            xc = xc_flat.reshape(bsz, n_ctx, d)
    return x
```

```python
import functools
import math

import jax
import jax.numpy as jnp
from jax import lax
from jax.experimental import pallas as pl
from jax.experimental.pallas import tpu as pltpu

F32 = jnp.float32
BF16 = jnp.bfloat16

D_MODEL = 2048
GRID_W = 64
DIFF_W = 512
GQA_W = 1024
HY_CH = 512
DIFF_V = 128
DIFF_QK = 64
DIFF_HEADS = 4
GQA_HD = 128
GQA_KV_HEADS = 2
GQA_GROUP = 4
HY_EMB = 33
HY_BANDS = 16
HY_FFN = 64
HY_FAST_DECAY = 0.3
HY_SLOW_DECAY = 1.5
HY_TARGET = 1e-2
HY_SHIFT = 0.0
D_FF = 4 * D_MODEL
N_MOD = 6
ROPE_THETA = 10000.0
EPS = 1e-6

LANE = 128
DQ_BLK = 0
GQ_BLK = 4
HY_OFF = 1536
KV_OFF = 3072
KV_BLK = KV_OFF // LANE
DK_REL, DV_REL, GK_REL, GV_REL = 0, 4, 8, 10
N_IN = 4608
KV_W = N_IN - KV_OFF

VMEM_CAP = 64 * 1024 * 1024
MOD_ROWS = 24
PREP_ROWS = 256
LOG2E = 1.4426950408889634
IN_PROJ_COLS = (2304, 1536)
MLP_HIDDEN_COLS = 1024
DFT_SPLIT = 64
HY_HALF = 256
KEY_CHUNKS = (256, 128)
ROW_CHUNK = 256
DIFF_HEADS_PER_STEP = 4


def _vmem_limit(nbytes):
    return int(min(max(nbytes * 5 // 4 + (4 << 20), 32 << 20), VMEM_CAP - (6 << 20)))


def _rms(x):
    return x * lax.rsqrt(jnp.mean(x * x, axis=-1, keepdims=True) + EPS)


def _dot(a, b):
    return jnp.dot(a, b, preferred_element_type=F32)


def _split_bf16(a):
    hi = a.astype(BF16)
    lo = (a - hi.astype(F32)).astype(BF16)
    return hi, lo


def _dot3(a, b):
    ah, al = _split_bf16(a)
    bh, bl = _split_bf16(b)
    return _dot(ah, bh) + (_dot(al, bh) + _dot(ah, bl))


def _rope(x, cos, sin_signed):
    lane = lax.broadcasted_iota(jnp.int32, x.shape, 1)
    nxt = pltpu.roll(x, LANE - 1, axis=1)
    prv = pltpu.roll(x, 1, axis=1)
    swapped = jnp.where(lane % 2 == 0, nxt, prv)
    return x * cos + swapped * sin_signed


def _mod_kernel(c_ref, w_ref, b_ref, o_ref):
    c = c_ref[...]
    s = c * (1.0 / (1.0 + jnp.exp(-c)))
    o_ref[0] = _dot(s.astype(BF16), w_ref[0].astype(BF16)) + b_ref[0]


def _modulations(cpad, w_mod, b_mod):
    nl, d, n = w_mod.shape
    tn = 1024
    return pl.pallas_call(
        _mod_kernel,
        grid=(nl, n // tn),
        in_specs=[
            pl.BlockSpec((MOD_ROWS, d), lambda l, j: (0, 0)),
            pl.BlockSpec((1, d, tn), lambda l, j: (l, 0, j)),
            pl.BlockSpec((1, 1, tn), lambda l, j: (l, 0, j)),
        ],
        out_specs=pl.BlockSpec((1, MOD_ROWS, tn), lambda l, j: (l, 0, j)),
        out_shape=jax.ShapeDtypeStruct((nl, MOD_ROWS, n), F32),
        compiler_params=pltpu.CompilerParams(
            dimension_semantics=("arbitrary", "arbitrary"),
            vmem_limit_bytes=_vmem_limit(2 * d * tn * 4 + d * tn * 2)),
        name="modulations",
    )(cpad, w_mod, b_mod.reshape(nl, 1, n))


def _in_proj_kernel(x_ref, g_ref, m_ref, w_ref, o_ref, xn_ref):
    j = pl.program_id(2)

    @pl.when(j == 0)
    def _():
        for r in range(0, o_ref.shape[0], ROW_CHUNK):
            rows = slice(r, r + ROW_CHUNK)
            y = _rms(x_ref[rows, :]) * g_ref[...]
            xn = (y * (1.0 + m_ref[1:2, :]) + m_ref[0:1, :]).astype(BF16)
            xn_ref[rows, :] = xn
            o_ref[rows, :] = _dot(xn, w_ref[...]).astype(o_ref.dtype)

    @pl.when(j > 0)
    def _():
        o_ref[...] = _dot(xn_ref[...], w_ref[...]).astype(o_ref.dtype)


def _in_proj(x, g, mods, w, layer, col0, n, per_batch):
    b, s, d = x.shape
    tm = min(s, 1024)
    assert s % tm == 0 and tm % ROW_CHUNK == 0
    tn = next(t for t in IN_PROJ_COLS if n % t == 0 and col0 % t == 0)
    jb = col0 // tn
    est = 2 * tm * d * 4 + tm * d * 2 + 2 * d * tn * 2 + 2 * tm * tn * 2 + tm * tn * 4
    return pl.pallas_call(
        _in_proj_kernel,
        grid=(b, s // tm, n // tn),
        in_specs=[
            pl.BlockSpec((None, tm, d), lambda bi, i, j: (bi, i, 0)),
            pl.BlockSpec((1, d), lambda bi, i, j: (0, 0)),
            pl.BlockSpec((None, N_MOD, d), lambda bi, i, j: (bi * per_batch, 0, 0)),
            pl.BlockSpec((None, d, tn), lambda bi, i, j: (layer, 0, jb + j)),
        ],
        out_specs=pl.BlockSpec((None, tm, tn), lambda bi, i, j: (bi, i, j)),
        out_shape=jax.ShapeDtypeStruct((b, s, n), BF16),
        scratch_shapes=[pltpu.VMEM((tm, d), BF16)],
        compiler_params=pltpu.CompilerParams(
            dimension_semantics=("parallel", "parallel", "arbitrary"),
            vmem_limit_bytes=_vmem_limit(est)),
        name="in_proj",
    )(x, g.reshape(1, d), mods, w)


def _attend_streams(streams, s_ref, finish):
    lk = streams[0][1].shape[0]
    ck = next(c for c in KEY_CHUNKS if lk % c == 0)
    chunks = [slice(r, r + ck) for r in range(0, lk, ck)]
    n = len(streams)
    qts = [make_qt() for make_qt, _, _ in streams]
    tq = qts[0].shape[1]
    mx = None
    prev = None
    done = None
    for i in range(n + 1):
        lsum = acc = None
        for ci, rows in enumerate(chunks):
            if i < n:
                sj = _dot(streams[i][1][rows, :], qts[i])
                s_ref[i % 2, rows, :] = sj
                mj = jnp.max(sj.reshape(ck // 8, 8, tq), axis=0)
                mx = mj if ci == 0 else jnp.maximum(mx, mj)
            if ci == 0 and done is not None:
                finish(*done)
                done = None
            if prev is not None:
                p = jnp.exp2(s_ref[prev[0] % 2, rows, :] - prev[1])
                lj = jnp.sum(p.reshape(ck // 8, 8, tq), axis=0)
                oj = _dot(streams[prev[0]][2][:, rows], p.astype(BF16))
                lsum = lj if lsum is None else lsum + lj
                acc = oj if acc is None else acc + oj
        if prev is not None:
            done = (prev[0], acc, jnp.sum(lsum, axis=0, keepdims=True))
        prev = (i, jnp.max(mx, axis=0, keepdims=True)) if i < n else None
    finish(*done)


def _gqa_kernel(*refs, rope, prefix):
    it = iter(refs)
    q_refs = [next(it) for _ in range(GQA_KV_HEADS)]
    k_ref, v_ref = next(it), next(it)
    kc_ref = vc_ref = None
    if prefix:
        kc_ref, vc_ref = next(it), next(it)
    cq_ref = sq_ref = ck_ref = sk_ref = None
    if rope:
        cq_ref, sq_ref, ck_ref, sk_ref = next(it), next(it), next(it), next(it)
    gq_ref, gk_ref, go_ref = next(it), next(it), next(it)
    o_ref = next(it)
    kall_ref, vt_ref, s_ref = next(it), next(it), next(it)

    c = kc_ref.shape[0] if prefix else 0
    s = k_ref.shape[0]

    @pl.when(pl.program_id(1) == 0)
    def _():
        for g in range(GQA_KV_HEADS):
            cols = slice(g * LANE, (g + 1) * LANE)
            if prefix:
                kall_ref[g, 0:c, :] = (_rms(kc_ref[:, cols].astype(F32)) * gk_ref[...]).astype(BF16)
                vt_ref[g, :, 0:c] = vc_ref[:, cols].astype(F32).T.astype(BF16)
            pr = min(s, PREP_ROWS)
            for r0 in range(0, s, pr):
                rows = slice(r0, r0 + pr)
                k = _rms(k_ref[rows, cols].astype(F32)) * gk_ref[...]
                if rope:
                    k = _rope(k, ck_ref[rows, :], sk_ref[rows, :])
                kall_ref[g, c + r0:c + r0 + pr, :] = k.astype(BF16)
                vt_ref[g, :, c + r0:c + r0 + pr] = v_ref[rows, cols].astype(F32).T.astype(BF16)

    def query_t(hh):
        g, r = divmod(hh, GQA_GROUP)
        q = _rms(q_refs[g][:, r * LANE:(r + 1) * LANE].astype(F32)) * gq_ref[...]
        if rope:
            q = _rope(q, cq_ref[...], sq_ref[...])
        return (q * (GQA_HD ** -0.5 * LOG2E)).T.astype(BF16)

    def finish(hh, acc, denom):
        ot = acc / denom
        ot = ot * lax.rsqrt(jnp.mean(ot * ot, axis=0, keepdims=True) + EPS) * go_ref[...]
        o_ref[:, hh * LANE:(hh + 1) * LANE] = ot.T.astype(o_ref.dtype)

    streams = [(functools.partial(query_t, hh), kall_ref.at[hh // GQA_GROUP], vt_ref.at[hh // GQA_GROUP])
               for hh in range(GQA_KV_HEADS * GQA_GROUP)]
    _attend_streams(streams, s_ref, finish)


def _gqa_attention(hq, hk, kv_blk, hc, c_blk, tables, gq, gk, go, tq):
    b, s, _ = hq.shape
    prefix = hc is not None
    rope = tables is not None
    qw = GQA_GROUP * LANE
    kw = GQA_KV_HEADS * LANE
    grid = (b, s // tq)
    in_specs = [pl.BlockSpec((None, tq, qw), functools.partial(lambda g, bi, i: (bi, i, GQ_BLK // GQA_GROUP + g), g))
                for g in range(GQA_KV_HEADS)]
    in_specs += [
        pl.BlockSpec((None, s, kw), lambda bi, i: (bi, 0, (kv_blk + GK_REL) // GQA_KV_HEADS)),
        pl.BlockSpec((None, s, kw), lambda bi, i: (bi, 0, (kv_blk + GV_REL) // GQA_KV_HEADS)),
    ]
    args = [hq] * GQA_KV_HEADS + [hk, hk]
    c = 0
    if prefix:
        c = hc.shape[1]
        in_specs += [
            pl.BlockSpec((None, c, kw), lambda bi, i: (bi, 0, (c_blk + GK_REL) // GQA_KV_HEADS)),
            pl.BlockSpec((None, c, kw), lambda bi, i: (bi, 0, (c_blk + GV_REL) // GQA_KV_HEADS)),
        ]
        args += [hc, hc]
    if rope:
        cos, sin = tables
        in_specs += [
            pl.BlockSpec((tq, LANE), lambda bi, i: (i, 0)),
            pl.BlockSpec((tq, LANE), lambda bi, i: (i, 0)),
            pl.BlockSpec((s, LANE), lambda bi, i: (0, 0)),
            pl.BlockSpec((s, LANE), lambda bi, i: (0, 0)),
        ]
        args += [cos, sin, cos, sin]
    vec = pl.BlockSpec((1, LANE), lambda bi, i: (0, 0))
    in_specs += [vec, vec, pl.BlockSpec((LANE, 1), lambda bi, i: (0, 0))]
    args += [gq.reshape(1, LANE), gk.reshape(1, LANE), go.reshape(LANE, 1)]
    lk = s + c
    est = (4 * tq * GQA_W * 2 + 8 * s * kw * 2 + 4 * s * LANE * 4 + 4 * lk * kw * 2
           + 2 * lk * tq * 4 + 4 * 256 * tq * 4)
    return pl.pallas_call(
        functools.partial(_gqa_kernel, rope=rope, prefix=prefix),
        grid=grid,
        in_specs=in_specs,
        out_specs=pl.BlockSpec((None, tq, GQA_W), lambda bi, i: (bi, i, 0)),
        out_shape=jax.ShapeDtypeStruct((b, s, GQA_W), BF16),
        scratch_shapes=[pltpu.VMEM((GQA_KV_HEADS, lk, LANE), BF16), pltpu.VMEM((GQA_KV_HEADS, LANE, lk), BF16),
                        pltpu.VMEM((2, lk, tq), F32)],
        compiler_params=pltpu.CompilerParams(
            dimension_semantics=("arbitrary",) * 2,
            vmem_limit_bytes=_vmem_limit(est)),
        name="gqa_attention",
    )(*args)


def _diff_kernel(*refs, rope, prefix, lam_init):
    it = iter(refs)
    lp_ref, q_ref, k_ref, v_ref = next(it), next(it), next(it), next(it)
    kc_ref = vc_ref = None
    if prefix:
        kc_ref, vc_ref = next(it), next(it)
    cq_ref = sq_ref = ck_ref = sk_ref = None
    if rope:
        cq_ref, sq_ref, ck_ref, sk_ref = next(it), next(it), next(it), next(it)
    gs_ref = next(it)
    o_ref = next(it)
    kall_ref, vt_ref, s_ref, lam_ref = next(it), next(it), next(it), next(it)
    c = kc_ref.shape[0] if prefix else 0
    s = k_ref.shape[0]

    first = (pl.program_id(0) == 0) & (pl.program_id(1) == 0) & (pl.program_id(2) == 0)

    @pl.when(first)
    def _():
        p = lp_ref[...]
        t1 = jnp.sum(p[0:1] * p[1:2], axis=-1, keepdims=True)
        t2 = jnp.sum(p[2:3] * p[3:4], axis=-1, keepdims=True)
        lam_ref[...] = jnp.broadcast_to(jnp.exp(t1) - jnp.exp(t2) + lam_init, lam_ref.shape)

    @pl.when(pl.program_id(2) == 0)
    def _():
        for hh in range(DIFF_HEADS_PER_STEP):
            cols = slice(hh * LANE, (hh + 1) * LANE)
            if prefix:
                kall_ref[hh, 0:c, :] = kc_ref[:, cols]
                vt_ref[hh, :, 0:c] = vc_ref[:, cols].astype(F32).T.astype(BF16)
            pr = min(s, PREP_ROWS)
            for r0 in range(0, s, pr):
                rows = slice(r0, r0 + pr)
                if rope:
                    k = _rope(k_ref[rows, cols].astype(F32), ck_ref[rows, :], sk_ref[rows, :]).astype(BF16)
                else:
                    k = k_ref[rows, cols]
                kall_ref[hh, c + r0:c + r0 + pr, :] = k
                vt_ref[hh, :, c + r0:c + r0 + pr] = v_ref[rows, cols].astype(F32).T.astype(BF16)

    lam = lam_ref[0:1, 0:1]

    head_qt = {}

    def query_t(hh, mp):
        if hh not in head_qt:
            q = q_ref[:, hh * LANE:(hh + 1) * LANE].astype(F32)
            if rope:
                q = _rope(q, cq_ref[...], sq_ref[...])
            head_qt[hh] = (q * (DIFF_QK ** -0.5 * LOG2E)).T
        qt = head_qt[hh]
        row = lax.broadcasted_iota(jnp.int32, qt.shape, 0)
        keep = (row < DIFF_QK) if mp == 0 else (row >= DIFF_QK)
        return jnp.where(keep, qt, 0.0).astype(BF16)

    first_map = {}

    def finish(i, acc, denom):
        hh, mp = divmod(i, 2)
        if mp == 0:
            first_map[hh] = acc / denom
            return
        ot = first_map.pop(hh) - acc * (lam / denom)
        ot = ot * lax.rsqrt(jnp.mean(ot * ot, axis=0, keepdims=True) + EPS) * (gs_ref[...] * (1.0 - lam_init))
        o_ref[:, hh * LANE:(hh + 1) * LANE] = ot.T.astype(o_ref.dtype)

    streams = [(functools.partial(query_t, hh, mp), kall_ref.at[hh], vt_ref.at[hh])
               for hh in range(DIFF_HEADS_PER_STEP) for mp in range(2)]
    _attend_streams(streams, s_ref, finish)


def _diff_attention(lam_params, hq, hk, kv_blk, hc, c_blk, tables, gsub, lam_init, tq):
    b, s, _ = hq.shape
    prefix = hc is not None
    rope = tables is not None
    hps = DIFF_HEADS_PER_STEP
    w = hps * LANE
    grid = (b, DIFF_HEADS // hps, s // tq)
    in_specs = [
        pl.BlockSpec((8, LANE), lambda bi, h, i: (0, 0)),
        pl.BlockSpec((None, tq, w), lambda bi, h, i: (bi, i, DQ_BLK // hps + h)),
        pl.BlockSpec((None, s, w), lambda bi, h, i: (bi, 0, (kv_blk + DK_REL) // hps + h)),
        pl.BlockSpec((None, s, w), lambda bi, h, i: (bi, 0, (kv_blk + DV_REL) // hps + h)),
    ]
    args = [lam_params, hq, hk, hk]
    c = 0
    if prefix:
        c = hc.shape[1]
        in_specs += [
            pl.BlockSpec((None, c, w), lambda bi, h, i: (bi, 0, (c_blk + DK_REL) // hps + h)),
            pl.BlockSpec((None, c, w), lambda bi, h, i: (bi, 0, (c_blk + DV_REL) // hps + h)),
        ]
        args += [hc, hc]
    if rope:
        cos, sin = tables
        in_specs += [
            pl.BlockSpec((tq, LANE), lambda bi, h, i: (i, 0)),
            pl.BlockSpec((tq, LANE), lambda bi, h, i: (i, 0)),
            pl.BlockSpec((s, LANE), lambda bi, h, i: (0, 0)),
            pl.BlockSpec((s, LANE), lambda bi, h, i: (0, 0)),
        ]
        args += [cos, sin, cos, sin]
    in_specs.append(pl.BlockSpec((LANE, 1), lambda bi, h, i: (0, 0)))
    args.append(gsub.reshape(LANE, 1))
    lk = s + c
    est = 8 * s * w * 2 + 4 * s * LANE * 4 + 4 * lk * w * 2 + 5 * tq * lk * 4
    return pl.pallas_call(
        functools.partial(_diff_kernel, rope=rope, prefix=prefix, lam_init=lam_init),
        grid=grid,
        in_specs=in_specs,
        out_specs=pl.BlockSpec((None, tq, w), lambda bi, h, i: (bi, i, h)),
        out_shape=jax.ShapeDtypeStruct((b, s, DIFF_W), BF16),
        scratch_shapes=[pltpu.VMEM((hps, lk, LANE), BF16), pltpu.VMEM((hps, LANE, lk), BF16),
                        pltpu.VMEM((2, lk, tq), F32), pltpu.VMEM((8, LANE), F32)],
        compiler_params=pltpu.CompilerParams(
            dimension_semantics=("arbitrary",) * 3,
            vmem_limit_bytes=_vmem_limit(est)),
        name="diff_attention",
    )(*args)


def _hy_filter_kernel(z_ref, w1_ref, b1_ref, w2_ref, b2_ref, w3_ref, b3_ref, wo_ref, fr_ref,
                      bias_ref, dl_ref, fp_hi_ref, fp_lo_ref, fq_hi_ref, fq_lo_ref,
                      hp_ref, hq_ref, hs_hi, hs_lo, hd_hi, hd_lo, alt_ref):
    j = pl.program_id(0)

    @pl.when(j == 0)
    def _():
        z = z_ref[...]
        h = jnp.sin(fr_ref[0:1, :] * (_dot3(z, w1_ref[...]) + b1_ref[...]))
        h = jnp.sin(fr_ref[1:2, :] * (_dot3(h, w2_ref[...]) + b2_ref[...]))
        h = jnp.sin(fr_ref[2:3, :] * (_dot3(h, w3_ref[...]) + b3_ref[...]))
        h = _dot3(h, wo_ref[...])
        decay = jnp.exp(-z[:, 0:1] * jnp.abs(dl_ref[...])) + HY_SHIFT
        hf = h[:, :HY_CH] * decay
        hb = h[:, HY_CH:] * decay
        row = lax.broadcasted_iota(jnp.int32, hf.shape, 0)
        hf = hf + jnp.where(row == 0, bias_ref[...], 0.0)
        hb = jnp.where(row == 0, 0.0, hb)
        hs = hf + hb
        hd = hf - hb
        sign = jnp.where(row % 2 == 0, 1.0, -1.0)
        alt_ref[...] = jnp.broadcast_to(jnp.sum(sign * hs, axis=0, keepdims=True), alt_ref.shape)
        a, bb = _split_bf16(hs)
        hs_hi[...] = a
        hs_lo[...] = bb
        a, bb = _split_bf16(hd)
        hd_hi[...] = a
        hd_lo[...] = bb

    xp = _dot(fp_hi_ref[...], hs_hi[...]) + (_dot(fp_lo_ref[...], hs_hi[...]) + _dot(fp_hi_ref[...], hs_lo[...]))
    xq = _dot(fq_hi_ref[...], hd_hi[...]) + (_dot(fq_lo_ref[...], hd_hi[...]) + _dot(fq_hi_ref[...], hd_lo[...]))
    row = lax.broadcasted_iota(jnp.int32, xq.shape, 0)
    is0 = (row == 0) & (j == 0)
    hp_ref[...] = xp
    hq_ref[...] = jnp.where(is0, alt_ref[0:1, :], xq)


def _hy_filters(consts, w1, b1, w2, b2, w3, b3, wout, freq, bias):
    n = consts["n"]
    fc = consts["fc"]
    nf = n // fc
    pad = LANE - HY_FFN
    w1p = jnp.pad(w1, ((0, LANE - HY_EMB), (0, pad)))
    w2p = jnp.pad(w2, ((0, pad), (0, pad)))
    w3p = jnp.pad(w3, ((0, pad), (0, pad)))
    wop = jnp.pad(wout, ((0, pad), (0, 0)))
    b1p = jnp.pad(b1, (0, pad)).reshape(1, LANE)
    b2p = jnp.pad(b2, (0, pad)).reshape(1, LANE)
    b3p = jnp.pad(b3, (0, pad)).reshape(1, LANE)
    frp = jnp.pad(freq, ((0, 0), (0, pad)))
    full = lambda shape: pl.BlockSpec(shape, lambda j: (0,) * len(shape))
    fspec_p = pl.BlockSpec((fc, n), lambda j: (j, 0))
    fspec_q = pl.BlockSpec((fc, n), lambda j: (nf + j, 0))
    est = n * LANE * 4 * 2 + 8 * fc * n * 2 + 4 * n * HY_CH * 2 + 4 * fc * HY_CH * 4 + 8 * n * HY_CH * 4
    return pl.pallas_call(
        _hy_filter_kernel,
        grid=(nf,),
        in_specs=[
            full((n, LANE)), full((LANE, LANE)), full((1, LANE)), full((LANE, LANE)), full((1, LANE)),
            full((LANE, LANE)), full((1, LANE)), full((LANE, 2 * HY_CH)), full((3, LANE)),
            full((1, HY_CH)), full((1, HY_CH)),
            fspec_p, fspec_p, fspec_q, fspec_q,
        ],
        out_specs=[pl.BlockSpec((fc, HY_CH), lambda j: (j, 0)), pl.BlockSpec((fc, HY_CH), lambda j: (j, 0))],
        out_shape=[jax.ShapeDtypeStruct((n, HY_CH), F32), jax.ShapeDtypeStruct((n, HY_CH), F32)],
        scratch_shapes=[pltpu.VMEM((n, HY_CH), BF16)] * 4 + [pltpu.VMEM((8, HY_CH), F32)],
        compiler_params=pltpu.CompilerParams(
            dimension_semantics=("arbitrary",),
            vmem_limit_bytes=_vmem_limit(est)),
        name="hyena_filters",
    )(consts["zfeat"], w1p, b1p, w2p, b2p, w3p, b3p, wop, frp, bias.reshape(1, HY_CH), consts["deltas"],
      consts["f_hi"], consts["f_lo"], consts["f_hi"], consts["f_lo"])


def _short_conv(hy_ref, cw_ref, cb_ref, c0, n):
    x = hy_ref[:, c0:c0 + LANE].astype(F32)
    row = lax.broadcasted_iota(jnp.int32, x.shape, 0)
    prv = jnp.where(row == 0, 0.0, pltpu.roll(x, 1, axis=0))
    nxt = jnp.where(row == n - 1, 0.0, pltpu.roll(x, n - 1, axis=0))
    w = cw_ref[:, c0:c0 + LANE]
    return prv * w[0:1] + x * w[1:2] + nxt * w[2:3] + cb_ref[:, c0:c0 + LANE]


def _hy_mixer_kernel(hy_ref, cw_ref, cb_ref, fp_ref, fq_ref, hp_ref, hq_ref, gp_ref, gq_ref, gn_ref,
                     o_ref, z_ref, y_ref, x0_ref, *, n, nf):
    j = pl.program_id(1)

    fc = fp_ref.shape[0]
    freq_halves = [slice(0, fc // 2), slice(fc // 2, fc)]
    all_cols = slice(0, HY_CH)
    row_chunks = [slice(r, r + ROW_CHUNK) for r in range(0, n, ROW_CHUNK)] if n > ROW_CHUNK else [slice(0, n)]

    def spectrum(cols, z, freqs, nyquist_row):
        xp = _dot(fp_ref[freqs, :], z)
        xq = _dot(fq_ref[freqs, :], z)
        hp = hp_ref[freqs, cols]
        hq = hq_ref[freqs, cols]
        qq = xq * hq
        if nyquist_row:
            is0 = lax.broadcasted_iota(jnp.int32, xp.shape, 0) == 0
            yp = xp * hp - jnp.where(is0, 0.0, qq)
            yq = jnp.where(is0, qq, xp * hq + xq * hp)
        else:
            yp = xp * hp - qq
            yq = xp * hq + xq * hp
        return yp.astype(BF16), yq.astype(BF16)

    def spectrum_with(cols, z, first_step, fillers):
        parts = []
        for i, freqs in enumerate(freq_halves):
            parts.append(spectrum(cols, z, freqs, first_step and i == 0))
            if fillers:
                fillers.pop(0)()
        return (jnp.concatenate([p[0] for p in parts], axis=0), jnp.concatenate([p[1] for p in parts], axis=0))

    def conv_z(c0):
        x1 = _short_conv(hy_ref, cw_ref, cb_ref, HY_CH + c0, n)
        v = _short_conv(hy_ref, cw_ref, cb_ref, 2 * HY_CH + c0, n)
        zc = (v * x1).astype(BF16)
        z_ref[:, c0:c0 + LANE] = zc
        return zc

    def conv_x0(c0):
        x0_ref[:, c0:c0 + LANE] = _short_conv(hy_ref, cw_ref, cb_ref, c0, n)

    def normed_rows(rows, y):
        prod = x0_ref[rows, :] * y
        inv = lax.rsqrt(jnp.mean(prod * prod, axis=-1, keepdims=True) + EPS)
        o_ref[rows, :] = (prod * inv * gn_ref[...]).astype(o_ref.dtype)

    @pl.when(j == 0)
    def _():
        lanes = list(range(0, HY_CH, LANE))
        zc = {c0: conv_z(c0) for c0 in lanes[:HY_HALF // LANE]}
        pending = lanes[HY_HALF // LANE:]
        fillers = [functools.partial(lambda c0: zc.__setitem__(c0, conv_z(c0)), c0) for c0 in pending]
        if nf <= 2:
            fillers += [functools.partial(conv_x0, c0) for c0 in lanes]
        for h0 in range(0, HY_CH, HY_HALF):
            cols = slice(h0, h0 + HY_HALF)
            z = jnp.concatenate([zc[c0] for c0 in range(h0, h0 + HY_HALF, LANE)], axis=1)
            yp, yq = spectrum_with(cols, z, True, fillers)
            for rows in row_chunks:
                y_ref[rows, cols] = _dot(gp_ref[rows, :], yp) + _dot(gq_ref[rows, :], yq)
                if h0 > 0 and fillers:
                    fillers.pop(0)()
        while fillers:
            fillers.pop(0)()
        if nf == 1:
            for rows in row_chunks:
                normed_rows(rows, y_ref[rows, :])

    def middle(with_x0):
        fillers = [functools.partial(conv_x0, c0) for c0 in range(0, HY_CH, LANE)] if with_x0 else []
        yp, yq = spectrum_with(all_cols, z_ref[...], False, fillers)
        half = max(len(row_chunks) // 2, 1)
        for i in range(0, len(row_chunks), half):
            rows = slice(row_chunks[i].start, row_chunks[min(i + half, len(row_chunks)) - 1].stop)
            y_ref[rows, :] += _dot(gp_ref[rows, :], yp) + _dot(gq_ref[rows, :], yq)
            if fillers:
                fillers.pop(0)()
        while fillers:
            fillers.pop(0)()

    if nf > 2:
        pl.when(j == 1)(functools.partial(middle, True))
    if nf > 3:
        pl.when((j > 1) & (j < nf - 1))(functools.partial(middle, False))

    if nf > 1:
        @pl.when(j == nf - 1)
        def _():
            yp, yq = spectrum_with(all_cols, z_ref[...], False, [])
            for rows in row_chunks:
                normed_rows(rows, y_ref[rows, :] + _dot(gp_ref[rows, :], yp) + _dot(gq_ref[rows, :], yq))


def _hy_mixer(h, consts, conv_w, conv_b, hp, hq, gnorm):
    b, n, _ = h.shape
    fc = consts["fc"]
    nf = n // fc
    est = (2 * n * 3 * HY_CH * 2 + 8 * fc * n * 2 + 4 * fc * HY_CH * 4 + n * HY_CH * (2 + 4 + 4)
           + 6 * fc * HY_CH * 4 + 6 * n * LANE * 4)
    return pl.pallas_call(
        functools.partial(_hy_mixer_kernel, n=n, nf=nf),
        grid=(b, nf),
        in_specs=[
            pl.BlockSpec((None, n, 3 * HY_CH), lambda bi, j: (bi, 0, HY_OFF // (3 * HY_CH))),
            pl.BlockSpec((3, 3 * HY_CH), lambda bi, j: (0, 0)),
            pl.BlockSpec((1, 3 * HY_CH), lambda bi, j: (0, 0)),
            pl.BlockSpec((fc, n), lambda bi, j: (j, 0)),
            pl.BlockSpec((fc, n), lambda bi, j: (nf + j, 0)),
            pl.BlockSpec((fc, HY_CH), lambda bi, j: (j, 0)),
            pl.BlockSpec((fc, HY_CH), lambda bi, j: (j, 0)),
            pl.BlockSpec((n, fc), lambda bi, j: (0, j)),
            pl.BlockSpec((n, fc), lambda bi, j: (0, nf + j)),
            pl.BlockSpec((1, HY_CH), lambda bi, j: (0, 0)),
        ],
        out_specs=pl.BlockSpec((None, n, HY_CH), lambda bi, j: (bi, 0, 0)),
        out_shape=jax.ShapeDtypeStruct((b, n, HY_CH), BF16),
        scratch_shapes=[pltpu.VMEM((n, HY_CH), BF16), pltpu.VMEM((n, HY_CH), F32), pltpu.VMEM((n, HY_CH), F32)],
        compiler_params=pltpu.CompilerParams(
            dimension_semantics=("arbitrary", "arbitrary"),
            vmem_limit_bytes=_vmem_limit(est + n * HY_CH * 4)),
        name="hyena_mixer",
    )(h, conv_w, conv_b.reshape(1, 3 * HY_CH), consts["f_hi"], consts["f_hi"], hp, hq,
      consts["g"], consts["g"], gnorm.reshape(1, HY_CH))


def _hy_constants(n):
    big = 2 * n
    t = jnp.linspace(0.0, 1.0, n, dtype=F32)[:, None]
    w = 2.0 * math.pi * jnp.arange(n, dtype=F32)[:, None] / n
    f = jnp.linspace(1e-4, HY_BANDS - 1, HY_BANDS, dtype=F32)[None, :]
    zfeat = jnp.concatenate([t, jnp.cos(f * w), -jnp.sin(f * w)], axis=-1)
    zfeat = jnp.pad(zfeat, ((0, 0), (0, LANE - HY_EMB)))
    min_decay = math.log(HY_TARGET) / HY_SLOW_DECAY
    max_decay = math.log(HY_TARGET) / HY_FAST_DECAY
    deltas = jnp.linspace(min_decay, max_decay, HY_CH, dtype=F32)[None, :]

    k = jnp.arange(n, dtype=jnp.int32)

    def cos_sin(mult):
        theta = ((k[:, None] * mult[None, :]) % big).astype(F32) * (2.0 * math.pi / big)
        return jnp.cos(theta), jnp.sin(theta)

    ca, sa = cos_sin(jnp.arange(n // DFT_SPLIT, dtype=jnp.int32) * DFT_SPLIT)
    cb, sb = cos_sin(jnp.arange(DFT_SPLIT, dtype=jnp.int32))
    cosm = (ca[:, :, None] * cb[:, None, :] - sa[:, :, None] * sb[:, None, :]).reshape(n, n)
    sinm = (sa[:, :, None] * cb[:, None, :] + ca[:, :, None] * sb[:, None, :]).reshape(n, n)
    alt = jnp.where(k % 2 == 0, 1.0, -1.0).astype(F32)
    fq = jnp.where((k == 0)[:, None], alt[None, :], -sinm)
    fmat = jnp.concatenate([cosm, fq], axis=0)
    f_hi = fmat.astype(BF16)
    f_lo = (fmat - f_hi.astype(F32)).astype(BF16)
    wk = jnp.where(k == 0, 1.0, 2.0).astype(F32)[None, :] / big
    gp = cosm * wk
    gq = jnp.where((k == 0)[None, :], alt[:, None] / big, -sinm * wk)
    g = jnp.concatenate([gp, gq], axis=1).astype(BF16)
    return dict(n=n, fc=min(n, 512), zfeat=zfeat, deltas=deltas, f_hi=f_hi, f_lo=f_lo, g=g)


def _out_proj_kernel(od_ref, og_ref, oh_ref, w_ref, x_ref, g_ref, m_ref, o_ref):
    for r in range(0, o_ref.shape[0], ROW_CHUNK):
        rows = slice(r, r + ROW_CHUNK)
        mix = _dot(od_ref[rows, :], w_ref[0:DIFF_W, :])
        mix = mix + _dot(og_ref[rows, :], w_ref[DIFF_W:DIFF_W + GQA_W, :])
        mix = mix + _dot(oh_ref[rows, :], w_ref[DIFF_W + GQA_W:, :])
        o_ref[rows, :] = x_ref[rows, :] + m_ref[2:3, :] * (_rms(mix) * g_ref[...])


def _out_proj(o_d, o_g, o_h, w, layer, x, g, mods, per_batch):
    b, s, d = x.shape
    tm = min(s, 1024)
    assert s % tm == 0 and tm % ROW_CHUNK == 0
    est = d * d * 2 + 2 * tm * d * 2 + 4 * tm * d * 4 + 2 * ROW_CHUNK * d * 4
    return pl.pallas_call(
        _out_proj_kernel,
        grid=(b, s // tm),
        in_specs=[
            pl.BlockSpec((None, tm, DIFF_W), lambda bi, i: (bi, i, 0)),
            pl.BlockSpec((None, tm, GQA_W), lambda bi, i: (bi, i, 0)),
            pl.BlockSpec((None, tm, HY_CH), lambda bi, i: (bi, i, 0)),
            pl.BlockSpec((None, d, d), lambda bi, i: (layer, 0, 0), pipeline_mode=pl.Buffered(1)),
            pl.BlockSpec((None, tm, d), lambda bi, i: (bi, i, 0)),
            pl.BlockSpec((1, d), lambda bi, i: (0, 0)),
            pl.BlockSpec((None, N_MOD, d), lambda bi, i: (bi * per_batch, 0, 0)),
        ],
        out_specs=pl.BlockSpec((None, tm, d), lambda bi, i: (bi, i, 0)),
        out_shape=jax.ShapeDtypeStruct((b, s, d), F32),
        compiler_params=pltpu.CompilerParams(
            dimension_semantics=("parallel", "parallel"),
            vmem_limit_bytes=_vmem_limit(est)),
        name="out_proj",
    )(o_d, o_g, o_h, w, x, g.reshape(1, d), mods)


def _mlp_kernel(x_ref, g2_ref, g3_ref, m_ref, wu_ref, wd_ref, o_ref, xn_ref, *, nf, ncol):
    f = pl.program_id(2)
    tm, d = o_ref.shape
    row_chunks = [slice(r, r + ROW_CHUNK) for r in range(0, tm, ROW_CHUNK)]

    def hidden(xn):
        return jnp.square(jnp.maximum(_dot(xn, wu_ref[...]), 0.0)).astype(BF16)

    @pl.when(f == 0)
    def _():
        for rows in row_chunks:
            y = _rms(x_ref[rows, :]) * g2_ref[...]
            xn = (y * (1.0 + m_ref[4:5, :]) + m_ref[3:4, :]).astype(BF16)
            xn_ref[rows, :] = xn
            o_ref[rows, :] = _dot(hidden(xn), wd_ref[...])

    @pl.when((f > 0) & (f < nf - 1))
    def _():
        hmid = hidden(xn_ref[...])
        for c in range(d // ncol):
            o_ref[:, c * ncol:(c + 1) * ncol] += _dot(hmid, wd_ref[:, c * ncol:(c + 1) * ncol])

    @pl.when(f == nf - 1)
    def _():
        for rows in row_chunks:
            acc = o_ref[rows, :] + _dot(hidden(xn_ref[rows, :]), wd_ref[...])
            o_ref[rows, :] = x_ref[rows, :] + m_ref[5:6, :] * (_rms(acc) * g3_ref[...])


def _mlp(x, g2, g3, mods, w_up, w_down, layer, per_batch):
    b, s, d = x.shape
    dff = w_up.shape[2]
    tm = min(s, 1024)
    assert s % tm == 0 and tm % ROW_CHUNK == 0
    tf = MLP_HIDDEN_COLS
    nf = dff // tf
    est = 4 * tm * d * 4 + tm * d * 2 + 4 * d * tf * 2 + tm * tf * 6 + tm * 512 * 4
    return pl.pallas_call(
        functools.partial(_mlp_kernel, nf=nf, ncol=512),
        grid=(b, s // tm, nf),
        in_specs=[
            pl.BlockSpec((None, tm, d), lambda bi, i, f: (bi, i, 0)),
            pl.BlockSpec((1, d), lambda bi, i, f: (0, 0)),
            pl.BlockSpec((1, d), lambda bi, i, f: (0, 0)),
            pl.BlockSpec((None, N_MOD, d), lambda bi, i, f: (bi * per_batch, 0, 0)),
            pl.BlockSpec((None, d, tf), lambda bi, i, f: (layer, 0, f)),
            pl.BlockSpec((None, tf, d), lambda bi, i, f: (layer, f, 0)),
        ],
        out_specs=pl.BlockSpec((None, tm, d), lambda bi, i, f: (bi, i, 0)),
        out_shape=jax.ShapeDtypeStruct((b, s, d), F32),
        scratch_shapes=[pltpu.VMEM((tm, d), BF16)],
        compiler_params=pltpu.CompilerParams(
            dimension_semantics=("parallel", "parallel", "arbitrary"),
            vmem_limit_bytes=_vmem_limit(est)),
        name="mlp",
    )(x, g2.reshape(1, d), g3.reshape(1, d), mods, w_up, w_down)


def _rope_tables(n_rows, head_dim):
    t_row = jnp.repeat(jnp.arange(n_rows, dtype=F32), GRID_W)
    t_col = jnp.tile(jnp.arange(GRID_W, dtype=F32), n_rows)
    d_axis = head_dim // 2
    inv = ROPE_THETA ** (-jnp.arange(0, d_axis, 2, dtype=F32) / d_axis)
    ang = jnp.concatenate([t_row[:, None] * inv, t_col[:, None] * inv], axis=-1)
    cos = jnp.repeat(jnp.cos(ang), 2, axis=-1)
    sin = jnp.repeat(jnp.sin(ang), 2, axis=-1)
    sign = jnp.where(jnp.arange(head_dim) % 2 == 0, -1.0, 1.0).astype(F32)
    sin = sin * sign
    reps = LANE // head_dim
    return jnp.tile(cos, (1, reps)), jnp.tile(sin, (1, reps))


def kernel(x, c, ctx, c_ctx, w_mod, b_mod, g_norm, w_in, w_out, diff_lam, diff_subln, gqa_q_norm,
           gqa_k_norm, gqa_out_norm, hy_conv_w, hy_conv_b, hy_w1, hy_b1, hy_w2, hy_b2, hy_w3, hy_b3,
           hy_wout, hy_freq, hy_bias, hy_out_norm, w_up, w_down):
    bsz, n_lat, d = x.shape
    n_ctx = ctx.shape[1]
    depth = w_mod.shape[0]
    assert d == D_MODEL and bsz + 1 <= MOD_ROWS
    assert n_lat % GRID_W == 0 and n_lat % 256 == 0 and n_ctx % 128 == 0

    tab_d = _rope_tables(n_lat // GRID_W, DIFF_QK)
    tab_g = _rope_tables(n_lat // GRID_W, GQA_HD)
    hyc_lat = _hy_constants(n_lat)
    hyc_ctx = _hy_constants(n_ctx)

    cpad = jnp.zeros((MOD_ROWS, d), F32).at[:bsz].set(c).at[bsz].set(c_ctx)
    mods = _modulations(cpad, w_mod, b_mod)
    lam_pad = jnp.pad(diff_lam, ((0, 0), (0, 4), (0, LANE - DIFF_QK)))

    w_in_b, w_out_b, w_up_b, w_down_b = (w.astype(BF16) for w in (w_in, w_out, w_up, w_down))
    tq_lat = min(n_lat, 1024)
    tq_ctx = min(n_ctx, 256)
    xc = ctx
    for l in range(depth):
        last = l == depth - 1
        lam_init = 0.8 - 0.6 * math.exp(-0.3 * l)
        m_lat = mods[l, :bsz].reshape(bsz, N_MOD, d)
        m_ctx = mods[l, bsz:bsz + 1].reshape(1, N_MOD, d)
        filt = (hy_w1[l], hy_b1[l], hy_w2[l], hy_b2[l], hy_w3[l], hy_b3[l], hy_wout[l], hy_freq[l], hy_bias[l])

        h = _in_proj(x, g_norm[l, 0], m_lat, w_in_b, l, 0, N_IN, 1)
        xc_flat = xc.reshape(1, bsz * n_ctx, d)
        if last:
            hc = _in_proj(xc_flat, g_norm[l, 0], m_ctx, w_in_b, l, KV_OFF, KV_W, 0)
        else:
            hc = _in_proj(xc_flat, g_norm[l, 0], m_ctx, w_in_b, l, 0, N_IN, 0)
        hc = hc.reshape(bsz, n_ctx, -1)
        c_blk = 0 if last else KV_BLK

        o_d = _diff_attention(lam_pad[l], h, h, KV_BLK, hc, c_blk, tab_d, diff_subln[l], lam_init, tq_lat)
        o_g = _gqa_attention(h, h, KV_BLK, hc, c_blk, tab_g, gqa_q_norm[l], gqa_k_norm[l], gqa_out_norm[l],
                             tq_lat)
        hp, hq = _hy_filters(hyc_lat, *filt)
        o_h = _hy_mixer(h, hyc_lat, hy_conv_w[l], hy_conv_b[l], hp, hq, hy_out_norm[l])
        x_new = _out_proj(o_d, o_g, o_h, w_out_b, l, x, g_norm[l, 1], m_lat, 1)

        if not last:
            oc_d = _diff_attention(lam_pad[l], hc, hc, KV_BLK, None, 0, None, diff_subln[l], lam_init, tq_ctx)
            oc_g = _gqa_attention(hc, hc, KV_BLK, None, 0, None, gqa_q_norm[l], gqa_k_norm[l],
                                  gqa_out_norm[l], tq_ctx)
            hpc, hqc = _hy_filters(hyc_ctx, *filt)
            oc_h = _hy_mixer(hc, hyc_ctx, hy_conv_w[l], hy_conv_b[l], hpc, hqc, hy_out_norm[l])
            flat = lambda a: a.reshape(1, bsz * n_ctx, a.shape[-1])
            xc_flat = _out_proj(flat(oc_d), flat(oc_g), flat(oc_h), w_out_b, l, xc_flat, g_norm[l, 1],
                                m_ctx, 0)
        x = x_new

        x = _mlp(x, g_norm[l, 2], g_norm[l, 3], m_lat, w_up_b, w_down_b, l, 1)
        if not last:
            xc_flat = _mlp(xc_flat, g_norm[l, 2], g_norm[l, 3], m_ctx, w_up_b, w_down_b, l, 0)
            xc = xc_flat.reshape(bsz, n_ctx, d)
    return x
```
